```python
import math
import jax, jax.numpy as jnp
from jax import lax
import numpy as np

D_MODEL = 4096
BATCH = 1
SEQ = 16384
DEPTH = 1

MIX_WIDTH = D_MODEL
ATTN_WIDTH = MIX_WIDTH // 2
SSM_WIDTH = MIX_WIDTH - ATTN_WIDTH
HEAD_DIM = 128
N_HEADS = ATTN_WIDTH // HEAD_DIM
IN_PROJ_WIDTH = 3 * ATTN_WIDTH + SSM_WIDTH

ROPE_DIM = HEAD_DIM // 4
ROPE_THETA = 500000.0

DILATED_PATTERNS = ((128, 1), (512, 4), (2048, 16))
NEG_INF = -1e30

SSM_GROUP = 16
SSM_GROUPS = SSM_WIDTH // SSM_GROUP
SSM_STATE = 64
SSM_CHUNK = 128
DT_MIN = 1e-3
DT_MAX = 1e-1

N_EXPERTS = 16
EXPERT_FF = D_MODEL // 2
CAPACITY_FACTOR = 2

EPS = 1e-6

kernel_name = "hybrid_s5_dilated_attn_ec_moe_block"


def rmsnorm(x, gain):
    x32 = x.astype(jnp.float32)
    y = x32 * lax.rsqrt(jnp.mean(x32 * x32, axis=-1, keepdims=True) + EPS)
    return y.astype(x.dtype) * gain


def modulate(h, shift, scale):
    return h * (1.0 + scale[:, None, :]) + shift[:, None, :]


def apply_partial_rope(t, positions):
    half = ROPE_DIM // 2
    inv_freq = ROPE_THETA ** (-jnp.arange(half, dtype=jnp.float32) / half)
    ang = positions.astype(jnp.float32)[..., None] * inv_freq
    cos = jnp.cos(ang)[:, :, None, :]
    sin = jnp.sin(ang)[:, :, None, :]
    t32 = t.astype(jnp.float32)
    t1 = t32[..., :half]
    t2 = t32[..., half:ROPE_DIM]
    rot = jnp.concatenate([t1 * cos - t2 * sin, t2 * cos + t1 * sin, t32[..., ROPE_DIM:]], axis=-1)
    return rot.astype(t.dtype)


def _neighbour_blocks(t, blk, seg, nb):
    pw = [(0, 0)] * t.ndim
    pw[2] = (blk, blk)
    tp = jnp.pad(t, pw)
    parts = [tp[:, :, o * blk:o * blk + seg].reshape(t.shape[:2] + (nb, blk) + t.shape[3:]) for o in range(3)]
    return jnp.concatenate(parts, axis=3)


def dilated_window_attention(q, k, v, window, dilation):
    b, s, h, dh = q.shape
    half = window // (2 * dilation)
    blk = half
    seg = -(-s // dilation)
    seg = -(-seg // blk) * blk
    sp = seg * dilation
    pad = sp - s
    nb = seg // blk

    def to_residues(t):
        t = jnp.pad(t, ((0, 0), (0, pad), (0, 0), (0, 0)))
        t = t.reshape(b, seg, dilation, h, dh)
        return jnp.transpose(t, (0, 2, 1, 3, 4))

    qr, kr, vr = to_residues(q), to_residues(k), to_residues(v)
    valid = (jnp.arange(sp) < s).reshape(seg, dilation).T

    qb = qr.reshape(b, dilation, nb, blk, h, dh)
    kb = _neighbour_blocks(kr, blk, seg, nb)
    vb = _neighbour_blocks(vr, blk, seg, nb)
    kvalid = _neighbour_blocks(valid[None], blk, seg, nb)[0]

    rel = jnp.arange(3 * blk)[None, :] - blk - jnp.arange(blk)[:, None]
    band = jnp.abs(rel) <= half
    mask = band[None, None] & kvalid[:, :, None, :]

    scores = jnp.einsum('brnqhe,brnkhe->brnhqk', qb, kb).astype(jnp.float32) * (dh ** -0.5)
    scores = jnp.where(mask[None, :, :, None], scores, NEG_INF)
    lse = jax.nn.logsumexp(scores, axis=-1)
    p = jnp.exp(scores - lse[..., None])
    o = jnp.einsum('brnhqk,brnkhe->brnqhe', p.astype(v.dtype), vb)

    o = o.reshape(b, dilation, seg, h, dh).transpose(0, 2, 1, 3, 4).reshape(b, sp, h, dh)[:, :s]
    lse = jnp.transpose(lse, (0, 1, 2, 4, 3)).reshape(b, dilation, seg, h)
    lse = jnp.transpose(lse, (0, 2, 1, 3)).reshape(b, sp, h)[:, :s]
    return o, lse


def dilated_mixture_attention(q, k, v):
    outs, lses = [], []
    for window, dilation in DILATED_PATTERNS:
        o, l = dilated_window_attention(q, k, v, window, dilation)
        outs.append(o)
        lses.append(l)
    w = jax.nn.softmax(jnp.stack(lses, axis=0), axis=0)
    return jnp.einsum('pbsh,pbshe->bshe', w.astype(q.dtype), jnp.stack(outs, axis=0))


def ssm_discretize(lam_re, lam_im, log_dt, b_re, b_im):
    lam_re = lam_re.astype(jnp.float32)
    lam_im = lam_im.astype(jnp.float32)
    b_re = b_re.astype(jnp.float32)
    b_im = b_im.astype(jnp.float32)
    dt = jnp.exp(log_dt.astype(jnp.float32))[:, None]
    mag = jnp.exp(lam_re * dt)
    ang = lam_im * dt
    lb_re = mag * jnp.cos(ang)
    lb_im = mag * jnp.sin(ang)
    den = lam_re * lam_re + lam_im * lam_im
    ar = lb_re - 1.0
    ai = lb_im
    coef_re = (ar * lam_re + ai * lam_im) / den
    coef_im = (ai * lam_re - ar * lam_im) / den
    bb_re = coef_re[..., None] * b_re - coef_im[..., None] * b_im
    bb_im = coef_re[..., None] * b_im + coef_im[..., None] * b_re
    return lb_re, lb_im, bb_re, bb_im


def _complex_affine_combine(e1, e2):
    a1r, a1i, b1r, b1i = e1
    a2r, a2i, b2r, b2i = e2
    return (a2r * a1r - a2i * a1i,
            a2r * a1i + a2i * a1r,
            a2r * b1r - a2i * b1i + b2r,
            a2r * b1i + a2i * b1r + b2i)


def ssm_scan(u, lb_re, lb_im, bb_re, bb_im, c_re, c_im):
    b, s, g, hh = u.shape
    nc = s // SSM_CHUNK
    uc = jnp.moveaxis(u.reshape(b, nc, SSM_CHUNK, g, hh), 1, 0)
    a_re = jnp.broadcast_to(lb_re, (b, SSM_CHUNK, g, SSM_STATE))
    a_im = jnp.broadcast_to(lb_im, (b, SSM_CHUNK, g, SSM_STATE))
    c_re = c_re.astype(jnp.float32)
    c_im = c_im.astype(jnp.float32)

    def step(carry, u_blk):
        h_re, h_im = carry
        bu_re = jnp.einsum('bcgh,gph->bcgp', u_blk, bb_re)
        bu_im = jnp.einsum('bcgh,gph->bcgp', u_blk, bb_im)
        A_re, A_im, X_re, X_im = lax.associative_scan(
            _complex_affine_combine, (a_re, a_im, bu_re, bu_im), axis=1)
        x_re = X_re + A_re * h_re[:, None] - A_im * h_im[:, None]
        x_im = X_im + A_re * h_im[:, None] + A_im * h_re[:, None]
        y = jnp.einsum('bcgp,ghp->bcgh', x_re, c_re) - jnp.einsum('bcgp,ghp->bcgh', x_im, c_im)
        return (x_re[:, -1], x_im[:, -1]), y

    init = (jnp.zeros((b, g, SSM_STATE), jnp.float32), jnp.zeros((b, g, SSM_STATE), jnp.float32))
    _, ys = lax.scan(step, init, uc)
    return jnp.moveaxis(ys, 0, 1).reshape(b, s, g, hh)


def s5_bidirectional(u, lam_re_fwd, lam_im_fwd, log_dt_fwd, lam_re_bwd, lam_im_bwd, log_dt_bwd,
                     ssm_b_re, ssm_b_im, ssm_c_re, ssm_c_im, ssm_d, w_glu, b_glu):
    b, s, _ = u.shape
    u32 = u.astype(jnp.float32)
    ug = u32.reshape(b, s, SSM_GROUPS, SSM_GROUP)
    fwd = ssm_discretize(lam_re_fwd, lam_im_fwd, log_dt_fwd, ssm_b_re, ssm_b_im)
    bwd = ssm_discretize(lam_re_bwd, lam_im_bwd, log_dt_bwd, ssm_b_re, ssm_b_im)
    y_f = ssm_scan(ug, *fwd, ssm_c_re, ssm_c_im)
    y_b = jnp.flip(ssm_scan(jnp.flip(ug, axis=1), *bwd, ssm_c_re, ssm_c_im), axis=1)
    y = (y_f + y_b).reshape(b, s, SSM_WIDTH) + ssm_d.astype(jnp.float32) * u32
    g = jax.nn.gelu(y)
    out = g * jax.nn.sigmoid(g @ w_glu.astype(jnp.float32) + b_glu.astype(jnp.float32))
    return out.astype(u.dtype)


def hybrid_mixer(xm, positions, w_in, lam_re_fwd, lam_im_fwd, log_dt_fwd, lam_re_bwd, lam_im_bwd,
                 log_dt_bwd, ssm_b_re, ssm_b_im, ssm_c_re, ssm_c_im, ssm_d, w_glu, b_glu,
                 norm_attn_out_gain, norm_ssm_out_gain, w_out):
    b, s, _ = xm.shape
    proj = xm @ w_in
    q, k, v, u = jnp.split(proj, [ATTN_WIDTH, 2 * ATTN_WIDTH, 3 * ATTN_WIDTH], axis=-1)
    q = apply_partial_rope(q.reshape(b, s, N_HEADS, HEAD_DIM), positions)
    k = apply_partial_rope(k.reshape(b, s, N_HEADS, HEAD_DIM), positions)
    v = v.reshape(b, s, N_HEADS, HEAD_DIM)
    attn = dilated_mixture_attention(q, k, v).reshape(b, s, ATTN_WIDTH)
    ssm = s5_bidirectional(u, lam_re_fwd, lam_im_fwd, log_dt_fwd, lam_re_bwd, lam_im_bwd, log_dt_bwd,
                           ssm_b_re, ssm_b_im, ssm_c_re, ssm_c_im, ssm_d, w_glu, b_glu)
    merged = jnp.concatenate([rmsnorm(attn, norm_attn_out_gain), rmsnorm(ssm, norm_ssm_out_gain)], axis=-1)
    return merged @ w_out


def expert_choice_ffn(xf, w_router, w_exp_gate, w_exp_up, w_exp_down):
    b, s, _ = xf.shape
    cap = max(1, CAPACITY_FACTOR * s // N_EXPERTS)
    aff = jax.nn.softmax((xf @ w_router).astype(jnp.float32), axis=-1)
    gates, idx = lax.top_k(jnp.swapaxes(aff, 1, 2), cap)
    bidx = jnp.arange(b)[:, None, None]
    xg = xf[bidx, idx]
    hdn = jax.nn.silu(jnp.einsum('becd,edf->becf', xg, w_exp_gate)) * jnp.einsum('becd,edf->becf', xg, w_exp_up)
    y = jnp.einsum('becf,efd->becd', hdn, w_exp_down) * gates[..., None].astype(xf.dtype)
    return jnp.zeros_like(xf).at[bidx, idx].add(y)


def setup_inputs(seed: int = 0) -> dict:
    key = jax.random.key(seed)
    ks = jax.random.split(key, 32)
    f32 = jnp.float32

    def nrm(k, shape, scale):
        return jax.random.normal(k, shape, f32) * scale

    n_idx = jnp.arange(SSM_STATE, dtype=f32)
    G, P = SSM_GROUPS, SSM_STATE
    x = nrm(ks[0], (BATCH, SEQ, D_MODEL), 1.0)
    c = nrm(ks[1], (BATCH, D_MODEL), 1.0)
    positions = (jnp.arange(SEQ, dtype=jnp.int32)[None, :]
                 + jax.random.randint(ks[2], (BATCH, 1), 0, 1024, dtype=jnp.int32))
    w_ada = nrm(ks[3], (DEPTH, D_MODEL, 6 * D_MODEL), 0.5 * D_MODEL ** -0.5)
    b_ada = nrm(ks[4], (DEPTH, 6 * D_MODEL), 0.02)
    norm_mix_gain = 1.0 + nrm(ks[5], (DEPTH, D_MODEL), 0.02)
    w_in = nrm(ks[6], (DEPTH, D_MODEL, IN_PROJ_WIDTH), D_MODEL ** -0.5)
    lam_re_fwd = -0.5 + nrm(ks[7], (DEPTH, G, P), 0.01)
    lam_im_fwd = math.pi * n_idx + nrm(ks[8], (DEPTH, G, P), 0.01)
    log_dt_fwd = jax.random.uniform(ks[9], (DEPTH, G), f32, math.log(DT_MIN), math.log(DT_MAX))
    lam_re_bwd = -0.5 + nrm(ks[10], (DEPTH, G, P), 0.01)
    lam_im_bwd = math.pi * n_idx + nrm(ks[11], (DEPTH, G, P), 0.01)
    log_dt_bwd = jax.random.uniform(ks[12], (DEPTH, G), f32, math.log(DT_MIN), math.log(DT_MAX))
    ssm_b_re = nrm(ks[13], (DEPTH, G, P, SSM_GROUP), (2 * SSM_GROUP) ** -0.5)
    ssm_b_im = nrm(ks[14], (DEPTH, G, P, SSM_GROUP), (2 * SSM_GROUP) ** -0.5)
    ssm_c_re = nrm(ks[15], (DEPTH, G, SSM_GROUP, P), P ** -0.5)
    ssm_c_im = nrm(ks[16], (DEPTH, G, SSM_GROUP, P), P ** -0.5)
    ssm_d = nrm(ks[17], (DEPTH, SSM_WIDTH), 1.0)
    w_glu = nrm(ks[18], (DEPTH, SSM_WIDTH, SSM_WIDTH), SSM_WIDTH ** -0.5)
    b_glu = nrm(ks[19], (DEPTH, SSM_WIDTH), 0.02)
    norm_attn_out_gain = 1.0 + nrm(ks[20], (DEPTH, ATTN_WIDTH), 0.02)
    norm_ssm_out_gain = 1.0 + nrm(ks[21], (DEPTH, SSM_WIDTH), 0.02)
    w_out = nrm(ks[22], (DEPTH, MIX_WIDTH, D_MODEL), MIX_WIDTH ** -0.5)
    norm_ffn_gain = 1.0 + nrm(ks[23], (DEPTH, D_MODEL), 0.02)
    w_router = nrm(ks[24], (DEPTH, D_MODEL, N_EXPERTS), D_MODEL ** -0.5)
    w_exp_gate = nrm(ks[25], (DEPTH, N_EXPERTS, D_MODEL, EXPERT_FF), D_MODEL ** -0.5)
    w_exp_up = nrm(ks[26], (DEPTH, N_EXPERTS, D_MODEL, EXPERT_FF), D_MODEL ** -0.5)
    w_exp_down = nrm(ks[27], (DEPTH, N_EXPERTS, EXPERT_FF, D_MODEL), EXPERT_FF ** -0.5)
    norm_final_gain = 1.0 + nrm(ks[28], (D_MODEL,), 0.02)
    return {"x": x, "c": c, "positions": positions, "w_ada": w_ada, "b_ada": b_ada,
            "norm_mix_gain": norm_mix_gain, "w_in": w_in,
            "lam_re_fwd": lam_re_fwd, "lam_im_fwd": lam_im_fwd, "log_dt_fwd": log_dt_fwd,
            "lam_re_bwd": lam_re_bwd, "lam_im_bwd": lam_im_bwd, "log_dt_bwd": log_dt_bwd,
            "ssm_b_re": ssm_b_re, "ssm_b_im": ssm_b_im, "ssm_c_re": ssm_c_re, "ssm_c_im": ssm_c_im,
            "ssm_d": ssm_d, "w_glu": w_glu, "b_glu": b_glu,
            "norm_attn_out_gain": norm_attn_out_gain, "norm_ssm_out_gain": norm_ssm_out_gain,
            "w_out": w_out, "norm_ffn_gain": norm_ffn_gain, "w_router": w_router,
            "w_exp_gate": w_exp_gate, "w_exp_up": w_exp_up, "w_exp_down": w_exp_down,
            "norm_final_gain": norm_final_gain}


def reference(x, c, positions, w_ada, b_ada, norm_mix_gain, w_in,
              lam_re_fwd, lam_im_fwd, log_dt_fwd, lam_re_bwd, lam_im_bwd, log_dt_bwd,
              ssm_b_re, ssm_b_im, ssm_c_re, ssm_c_im, ssm_d, w_glu, b_glu,
              norm_attn_out_gain, norm_ssm_out_gain, w_out, norm_ffn_gain, w_router,
              w_exp_gate, w_exp_up, w_exp_down, norm_final_gain):
    h = x
    c_act = jax.nn.silu(c)
    for layer in range(DEPTH):
        mod = c_act @ w_ada[layer] + b_ada[layer]
        sh_m, sc_m, g_m, sh_f, sc_f, g_f = jnp.split(mod, 6, axis=-1)
        xm = modulate(rmsnorm(h, norm_mix_gain[layer]), sh_m, sc_m)
        mix = hybrid_mixer(xm, positions, w_in[layer],
                           lam_re_fwd[layer], lam_im_fwd[layer], log_dt_fwd[layer],
                           lam_re_bwd[layer], lam_im_bwd[layer], log_dt_bwd[layer],
                           ssm_b_re[layer], ssm_b_im[layer], ssm_c_re[layer], ssm_c_im[layer],
                           ssm_d[layer], w_glu[layer], b_glu[layer],
                           norm_attn_out_gain[layer], norm_ssm_out_gain[layer], w_out[layer])
        h = h + g_m[:, None, :] * mix
        xf = modulate(rmsnorm(h, norm_ffn_gain[layer]), sh_f, sc_f)
        ffn = expert_choice_ffn(xf, w_router[layer], w_exp_gate[layer], w_exp_up[layer], w_exp_down[layer])
        h = h + g_f[:, None, :] * ffn
    return rmsnorm(h, norm_final_gain)
```

```python
import functools
import math

import jax
import jax.numpy as jnp
from jax import lax
from jax.experimental import pallas as pl
from jax.experimental.pallas import tpu as pltpu

F32 = jnp.float32
BF16 = jnp.bfloat16
I32 = jnp.int32
HIGHEST = lax.Precision.HIGHEST

D_MODEL = 4096
ATTN_WIDTH = 2048
SSM_WIDTH = 2048
HEAD_DIM = 128
N_HEADS = 16
IN_PROJ_WIDTH = 3 * ATTN_WIDTH + SSM_WIDTH
ROPE_DIM = 32
ROPE_THETA = 500000.0
DILATIONS = (1, 4, 16)
ATTN_HALF = 64
NEG_INF = -1e30
SSM_GROUP = 16
SSM_GROUPS = 128
SSM_STATE = 64
SSM_CHUNK = 16
N_EXPERTS = 16
EXPERT_FF = 2048
CAPACITY_FACTOR = 2
EPS = 1e-6

LANES = 128
MIB = 1024 * 1024


def _params(semantics, vmem_mib):
    return pltpu.CompilerParams(dimension_semantics=semantics, vmem_limit_bytes=vmem_mib * MIB)


def _ada_kernel(c_ref, w_ref, b_ref, o_ref):
    c = c_ref[...]
    ca = c * jax.nn.sigmoid(c)
    o_ref[...] = jnp.dot(ca, w_ref[...], precision=HIGHEST, preferred_element_type=F32) + b_ref[...]


def _adaln(c, w_ada, b_ada):
    d, n = w_ada.shape
    tn = 512
    c8 = jnp.pad(c, ((0, 7), (0, 0)))
    out = pl.pallas_call(
        _ada_kernel,
        grid=(n // tn,),
        in_specs=[pl.BlockSpec((8, d), lambda j: (0, 0)),
                  pl.BlockSpec((d, tn), lambda j: (0, j)),
                  pl.BlockSpec((1, tn), lambda j: (0, j))],
        out_specs=pl.BlockSpec((8, tn), lambda j: (0, j)),
        out_shape=jax.ShapeDtypeStruct((8, n), F32),
        compiler_params=_params(("parallel",), 40),
        name="adaln",
    )(c8, w_ada, b_ada.reshape(1, n))
    return out[0:1]


def _norm_mod(x, gain, scale, shift):
    ms = jnp.mean(x * x, axis=-1, keepdims=True)
    y = x * lax.rsqrt(ms + EPS) * gain
    return y * (1.0 + scale) + shift


def _norm_mod_kernel(x_ref, g_ref, sc_ref, sh_ref, o_ref):
    o_ref[...] = _norm_mod(x_ref[...], g_ref[...], sc_ref[...], sh_ref[...]).astype(o_ref.dtype)


def _norm_mod_call(x, gain, scale, shift, out_dtype):
    s, d = x.shape
    tm = 256
    row = pl.BlockSpec((1, d), lambda i: (0, 0))
    return pl.pallas_call(
        _norm_mod_kernel,
        grid=(s // tm,),
        in_specs=[pl.BlockSpec((tm, d), lambda i: (i, 0)), row, row, row],
        out_specs=pl.BlockSpec((tm, d), lambda i: (i, 0)),
        out_shape=jax.ShapeDtypeStruct((s, d), out_dtype),
        compiler_params=_params(("parallel",), 40),
        name="norm_mod",
    )(x, gain, scale, shift)


def _rope_kernel(pos_ref, c_ref, s1_ref, s2_ref):
    half = ROPE_DIM // 2
    pos = pos_ref[...].astype(F32)
    lane = lax.broadcasted_iota(I32, (1, LANES), 1)
    fidx = (lane & (half - 1)).astype(F32)
    inv_freq = jnp.exp(fidx * (-math.log(ROPE_THETA) / half))
    ang = pos * inv_freq
    cs = jnp.cos(ang)
    sn = jnp.sin(ang)
    c_ref[...] = jnp.where(lane < ROPE_DIM, cs, 1.0)
    s1_ref[...] = jnp.where(lane < half, -sn, 0.0)
    s2_ref[...] = jnp.where((lane >= half) & (lane < ROPE_DIM), sn, 0.0)


def _rope_tables(positions):
    s = positions.shape[0]
    tm = 512
    spec = pl.BlockSpec((tm, LANES), lambda i: (i, 0))
    shp = jax.ShapeDtypeStruct((s, LANES), F32)
    return pl.pallas_call(
        _rope_kernel,
        grid=(s // tm,),
        in_specs=[pl.BlockSpec((tm, 1), lambda i: (i, 0))],
        out_specs=[spec, spec, spec],
        out_shape=[shp, shp, shp],
        compiler_params=_params(("parallel",), 32),
        name="rope_tables",
    )(positions.reshape(s, 1))


def _inproj_kernel(a_ref, w_ref, c_ref, s1_ref, s2_ref, o_ref, *, tn):
    j = pl.program_id(1)
    acc = jnp.dot(a_ref[...], w_ref[...], preferred_element_type=F32)
    rep = tn // HEAD_DIM
    n_qk = 2 * ATTN_WIDTH // tn
    n_q = ATTN_WIDTH // tn

    @pl.when(j < n_qk)
    def _():
        c = jnp.concatenate([c_ref[...]] * rep, axis=1)
        s1 = jnp.concatenate([s1_ref[...]] * rep, axis=1)
        s2 = jnp.concatenate([s2_ref[...]] * rep, axis=1)
        half = ROPE_DIM // 2
        rot = acc * c + pltpu.roll(acc, tn - half, 1) * s1 + pltpu.roll(acc, half, 1) * s2
        scale = jnp.where(j < n_q, HEAD_DIM ** -0.5, 1.0).astype(F32)
        o_ref[...] = (rot * scale).astype(o_ref.dtype)

    @pl.when(j >= n_qk)
    def _():
        o_ref[...] = acc.astype(o_ref.dtype)


def _in_proj(xm, w_in, cos_t, sin1_t, sin2_t):
    s, d = xm.shape
    n = w_in.shape[1]
    tm, tn = 1024, 512
    tab = pl.BlockSpec((tm, LANES), lambda i, j: (i, 0))
    return pl.pallas_call(
        functools.partial(_inproj_kernel, tn=tn),
        grid=(s // tm, n // tn),
        in_specs=[pl.BlockSpec((tm, d), lambda i, j: (i, 0)),
                  pl.BlockSpec((d, tn), lambda i, j: (0, j)),
                  tab, tab, tab],
        out_specs=pl.BlockSpec((tm, tn), lambda i, j: (i, j)),
        out_shape=jax.ShapeDtypeStruct((s, n), BF16),
        compiler_params=_params(("parallel", "arbitrary"), 48),
        name="in_proj",
    )(xm, w_in, cos_t, sin1_t, sin2_t)


def _attn_kernel(q_ref, k_ref, v_ref, o_ref, l_ref, *, n):
    tq = LANES
    tk = 2 * LANES
    row_id = lax.broadcasted_iota(I32, (tq, tk), 0)
    col_id = lax.broadcasted_iota(I32, (tq, tk), 1)
    eye = lax.broadcasted_iota(I32, (tq, tq), 0) == lax.broadcasted_iota(I32, (tq, tq), 1)

    def body(sb, carry):
        q0 = pl.multiple_of(sb * tq, tq)
        ks = pl.multiple_of(jnp.clip(q0 - ATTN_HALF, 0, n - tk), ATTN_HALF)
        q = q_ref[pl.ds(q0, tq), :]
        k = k_ref[pl.ds(ks, tk), :]
        v = v_ref[pl.ds(ks, tk), :]
        s = lax.dot_general(q, k, (((1,), (1,)), ((), ())), preferred_element_type=F32)
        rel = (ks + col_id) - (q0 + row_id)
        s = jnp.where(jnp.abs(rel) <= ATTN_HALF, s, NEG_INF)
        m = jnp.max(s, axis=-1, keepdims=True)
        p = jnp.exp(s - m)
        l = jnp.sum(p, axis=-1, keepdims=True)
        o = jnp.dot(p.astype(BF16), v, preferred_element_type=F32) / l
        o_ref[pl.ds(q0, tq), :] = o.astype(o_ref.dtype)
        lse = m + jnp.log(l)
        l_ref[0, pl.ds(sb, 1), :] = jnp.sum(jnp.where(eye, lse, 0.0), axis=0, keepdims=True)
        return carry

    lax.fori_loop(0, n // tq, body, 0)


def _dilated_attention(proj, dil):
    s, width = proj.shape
    n = s // dil
    view = proj.reshape(n, dil * width)
    cols = width // HEAD_DIM
    nsb = n // LANES

    def spec(off):
        return pl.BlockSpec((n, HEAD_DIM), lambda r, h: (0, r * cols + off + h))

    o, lse = pl.pallas_call(
        functools.partial(_attn_kernel, n=n),
        grid=(dil, N_HEADS),
        in_specs=[spec(0), spec(N_HEADS), spec(2 * N_HEADS)],
        out_specs=[pl.BlockSpec((n, HEAD_DIM), lambda r, h: (0, r * N_HEADS + h)),
                   pl.BlockSpec((1, nsb, LANES), lambda r, h: (r * N_HEADS + h, 0, 0))],
        out_shape=[jax.ShapeDtypeStruct((n, dil * ATTN_WIDTH), BF16),
                   jax.ShapeDtypeStruct((dil * N_HEADS, nsb, LANES), F32)],
        compiler_params=_params(("parallel", "parallel"), 48),
        name=f"dilated_attn_d{dil}",
    )(view, view, view)
    o = o.reshape(s, ATTN_WIDTH)
    lse = lse.reshape(dil, N_HEADS, n).transpose(2, 0, 1).reshape(s, N_HEADS)
    return o, lse


def _merge_kernel(o1_ref, o2_ref, o3_ref, l1_ref, l2_ref, l3_ref, g_ref, out_ref, acc_ref):
    la, lb, lc = l1_ref[...], l2_ref[...], l3_ref[...]
    m = jnp.maximum(jnp.maximum(la, lb), lc)
    ea, eb, ec = jnp.exp(la - m), jnp.exp(lb - m), jnp.exp(lc - m)
    den = ea + eb + ec
    wa, wb, wc = ea / den, eb / den, ec / den
    sq = jnp.zeros((la.shape[0], 1), F32)
    for h in range(N_HEADS):
        cs = slice(h * HEAD_DIM, (h + 1) * HEAD_DIM)
        slab = (wa[:, h:h + 1] * o1_ref[:, cs].astype(F32)
                + wb[:, h:h + 1] * o2_ref[:, cs].astype(F32)
                + wc[:, h:h + 1] * o3_ref[:, cs].astype(F32))
        acc_ref[:, cs] = slab
        sq = sq + jnp.sum(slab * slab, axis=-1, keepdims=True)
    inv = lax.rsqrt(sq * (1.0 / ATTN_WIDTH) + EPS)
    out_ref[...] = (acc_ref[...] * inv * g_ref[...]).astype(out_ref.dtype)


def _merge_patterns(outs, lses, gain):
    s = outs[0].shape[0]
    tm = 256
    ospec = pl.BlockSpec((tm, ATTN_WIDTH), lambda i: (i, 0))
    lspec = pl.BlockSpec((tm, N_HEADS), lambda i: (i, 0))
    return pl.pallas_call(
        _merge_kernel,
        grid=(s // tm,),
        in_specs=[ospec, ospec, ospec, lspec, lspec, lspec,
                  pl.BlockSpec((1, ATTN_WIDTH), lambda i: (0, 0))],
        out_specs=ospec,
        out_shape=jax.ShapeDtypeStruct((s, ATTN_WIDTH), BF16),
        scratch_shapes=[pltpu.VMEM((tm, ATTN_WIDTH), F32)],
        compiler_params=_params(("parallel",), 32),
        name="attn_merge_norm",
    )(*outs, *lses, gain)


def _ssm_direction_terms(lam_re, lam_im, log_dt, b_re, b_im, c_re, c_im):
    dt = jnp.exp(log_dt)[:, None]
    mag = jnp.exp(lam_re * dt)
    ang = lam_im * dt
    lb_re = mag * jnp.cos(ang)
    lb_im = mag * jnp.sin(ang)
    den = lam_re * lam_re + lam_im * lam_im
    ar = lb_re - 1.0
    ai = lb_im
    coef_re = (ar * lam_re + ai * lam_im) / den
    coef_im = (ai * lam_re - ar * lam_im) / den
    bb_re = coef_re[..., None] * b_re - coef_im[..., None] * b_im
    bb_im = coef_re[..., None] * b_im + coef_im[..., None] * b_re
    tau = jnp.arange(SSM_CHUNK + 1, dtype=F32)[:, None, None]
    pw_mag = jnp.exp(tau * (lam_re * dt)[None])
    pw_re = pw_mag * jnp.cos(tau * ang[None])
    pw_im = pw_mag * jnp.sin(tau * ang[None])
    e_re = pw_re[..., None] * bb_re[None] - pw_im[..., None] * bb_im[None]
    e_im = pw_re[..., None] * bb_im[None] + pw_im[..., None] * bb_re[None]
    kern = (jnp.einsum('ghp,tgpi->tghi', c_re, e_re[:SSM_CHUNK], precision=HIGHEST)
            - jnp.einsum('ghp,tgpi->tghi', c_im, e_im[:SSM_CHUNK], precision=HIGHEST))
    ca_re = c_re[None] * pw_re[:, :, None, :] - c_im[None] * pw_im[:, :, None, :]
    ca_im = c_re[None] * pw_im[:, :, None, :] + c_im[None] * pw_re[:, :, None, :]
    return kern, e_re, e_im, ca_re, ca_im, pw_re[SSM_CHUNK], pw_im[SSM_CHUNK]


def _ssm_operators(lam_re_f, lam_im_f, log_dt_f, lam_re_b, lam_im_b, log_dt_b, b_re, b_im, c_re, c_im):
    L, G, P = SSM_CHUNK, SSM_GROUPS, SSM_STATE
    kf, ef_re, ef_im, caf_re, caf_im, af_re, af_im = _ssm_direction_terms(
        lam_re_f, lam_im_f, log_dt_f, b_re, b_im, c_re, c_im)
    kb, eb_re, eb_im, cab_re, cab_im, ab_re, ab_im = _ssm_direction_terms(
        lam_re_b, lam_im_b, log_dt_b, b_re, b_im, c_re, c_im)
    i_idx = jnp.arange(L)[:, None]
    j_idx = jnp.arange(L)[None, :]
    lag = j_idx - i_idx
    wf = jnp.where((lag >= 0)[:, :, None, None, None], kf[jnp.clip(lag, 0, L - 1)], 0.0)
    wb = jnp.where((lag <= 0)[:, :, None, None, None], kb[jnp.clip(-lag, 0, L - 1)], 0.0)
    w = jnp.transpose(wf + wb, (2, 0, 4, 1, 3)).reshape(G, L * SSM_GROUP, L * SSM_GROUP)

    def p_mat(e, order):
        return jnp.transpose(e[order], (1, 0, 3, 2)).reshape(G, L * SSM_GROUP, P)

    def q_mat(ca, order):
        return jnp.transpose(ca[order], (1, 3, 0, 2)).reshape(G, P, L * SSM_GROUP)

    f_ord = (L - 1) - jnp.arange(L)
    b_ord = jnp.arange(L)
    parts_p = [p_mat(ef_re, f_ord), p_mat(ef_im, f_ord), p_mat(eb_re, b_ord), p_mat(eb_im, b_ord)]
    parts_q = [q_mat(caf_re, jnp.arange(L) + 1), -q_mat(caf_im, jnp.arange(L) + 1),
               q_mat(cab_re, L - jnp.arange(L)), -q_mat(cab_im, L - jnp.arange(L))]
    even = (jnp.arange(G) % 2 == 0)[:, None, None]

    def pad_p(x):
        z = jnp.zeros_like(x)
        return jnp.where(even, jnp.concatenate([x, z], -1), jnp.concatenate([z, x], -1))

    def pad_q(x):
        z = jnp.zeros_like(x)
        return jnp.where(even, jnp.concatenate([x, z], 1), jnp.concatenate([z, x], 1))

    p_all = jnp.stack([pad_p(x) for x in parts_p], axis=1).astype(BF16)
    q_all = jnp.stack([pad_q(x) for x in parts_q], axis=1).astype(BF16)

    def pair(x):
        return x.reshape(G // 2, 1, 2 * P)

    a_all = jnp.concatenate([pair(af_re), pair(af_im), pair(ab_re), pair(ab_im)], axis=1)
    return w.astype(BF16), p_all, q_all, a_all


def _gelu_tanh(x):
    return 0.5 * x * (1.0 + jnp.tanh(math.sqrt(2.0 / math.pi) * (x + 0.044715 * (x * x * x))))


def _ssm_kernel(u_ref, w_ref, p_ref, q_ref, a_ref, d_ref, y_ref, sfr, sfi, sbr, sbi, *, nc, gb):
    npair = gb // 2
    for g in range(gb):
        cs = slice((g // 2) * LANES, (g // 2 + 1) * LANES)
        u = u_ref[g]
        for k, scr in enumerate((sfr, sfi, sbr, sbi)):
            contrib = jnp.dot(u, p_ref[g, k], preferred_element_type=F32)
            if g % 2 == 0:
                scr[:, cs] = contrib
            else:
                scr[:, cs] += contrib

    a = a_ref[...]
    af_re = jnp.concatenate([a[p, 0:1] for p in range(npair)], axis=1)
    af_im = jnp.concatenate([a[p, 1:2] for p in range(npair)], axis=1)
    ab_re = jnp.concatenate([a[p, 2:3] for p in range(npair)], axis=1)
    ab_im = jnp.concatenate([a[p, 3:4] for p in range(npair)], axis=1)

    def step(c, carry):
        hfr, hfi, hbr, hbi = carry
        cb = nc - 1 - c
        rfr = sfr[pl.ds(c, 1), :]
        rfi = sfi[pl.ds(c, 1), :]
        rbr = sbr[pl.ds(cb, 1), :]
        rbi = sbi[pl.ds(cb, 1), :]
        sfr[pl.ds(c, 1), :] = hfr
        sfi[pl.ds(c, 1), :] = hfi
        sbr[pl.ds(cb, 1), :] = hbr
        sbi[pl.ds(cb, 1), :] = hbi
        return (af_re * hfr - af_im * hfi + rfr, af_re * hfi + af_im * hfr + rfi,
                ab_re * hbr - ab_im * hbi + rbr, ab_re * hbi + ab_im * hbr + rbi)

    z = jnp.zeros((1, npair * LANES), F32)
    lax.fori_loop(0, nc, step, (z, z, z, z))

    for g in range(gb):
        cs = slice((g // 2) * LANES, (g // 2 + 1) * LANES)
        u = u_ref[g]
        y = jnp.dot(u, w_ref[g], preferred_element_type=F32)
        for k, scr in enumerate((sfr, sfi, sbr, sbi)):
            y = y + jnp.dot(scr[:, cs].astype(BF16), q_ref[g, k], preferred_element_type=F32)
        y = y + d_ref[g] * u.astype(F32)
        y_ref[g] = _gelu_tanh(y).astype(y_ref.dtype)


def _ssm_mixer(u_t, w, p_all, q_all, a_all, d_t):
    g, nc, k = u_t.shape
    gb = 8
    return pl.pallas_call(
        functools.partial(_ssm_kernel, nc=nc, gb=gb),
        grid=(g // gb,),
        in_specs=[pl.BlockSpec((gb, nc, k), lambda i: (i, 0, 0)),
                  pl.BlockSpec((gb, k, k), lambda i: (i, 0, 0)),
                  pl.BlockSpec((gb, 4, k, LANES), lambda i: (i, 0, 0, 0)),
                  pl.BlockSpec((gb, 4, LANES, k), lambda i: (i, 0, 0, 0)),
                  pl.BlockSpec((gb // 2, 4, LANES), lambda i: (i, 0, 0)),
                  pl.BlockSpec((gb, 1, k), lambda i: (i, 0, 0))],
        out_specs=pl.BlockSpec((gb, nc, k), lambda i: (i, 0, 0)),
        out_shape=jax.ShapeDtypeStruct((g, nc, k), BF16),
        scratch_shapes=[pltpu.VMEM((nc, (gb // 2) * LANES), F32)] * 4,
        compiler_params=_params(("parallel",), 48),
        name="ssm_scan",
    )(u_t, w, p_all, q_all, a_all, d_t)


def _glu_kernel(g_ref, w_ref, b_ref, n_ref, o_ref):
    g = g_ref[...]
    z = jnp.dot(g, w_ref[...], preferred_element_type=F32) + b_ref[...]
    out = g.astype(F32) * jax.nn.sigmoid(z)
    ms = jnp.mean(out * out, axis=-1, keepdims=True)
    o_ref[...] = (out * lax.rsqrt(ms + EPS) * n_ref[...]).astype(o_ref.dtype)


def _glu_norm(g, w_glu, b_glu, gain):
    s, d = g.shape
    tm = 512
    row = pl.BlockSpec((1, d), lambda i: (0, 0))
    return pl.pallas_call(
        _glu_kernel,
        grid=(s // tm,),
        in_specs=[pl.BlockSpec((tm, d), lambda i: (i, 0)),
                  pl.BlockSpec((d, d), lambda i: (0, 0)), row, row],
        out_specs=pl.BlockSpec((tm, d), lambda i: (i, 0)),
        out_shape=jax.ShapeDtypeStruct((s, d), BF16),
        compiler_params=_params(("parallel",), 48),
        name="ssm_glu_norm",
    )(g, w_glu, b_glu, gain)


def _outproj_kernel(a1_ref, a2_ref, w1_ref, w2_ref, x_ref, gm_ref, o_ref):
    mix = (jnp.dot(a1_ref[...], w1_ref[...], preferred_element_type=F32)
           + jnp.dot(a2_ref[...], w2_ref[...], preferred_element_type=F32))
    o_ref[...] = x_ref[...] + gm_ref[...] * mix


def _out_proj(attn_n, ssm_n, w_out, x, g_m):
    s, k = attn_n.shape
    n = w_out.shape[1]
    tm, tn = 1024, 512
    return pl.pallas_call(
        _outproj_kernel,
        grid=(s // tm, n // tn),
        in_specs=[pl.BlockSpec((tm, k), lambda i, j: (i, 0)),
                  pl.BlockSpec((tm, k), lambda i, j: (i, 0)),
                  pl.BlockSpec((k, tn), lambda i, j: (0, j)),
                  pl.BlockSpec((k, tn), lambda i, j: (1, j)),
                  pl.BlockSpec((tm, tn), lambda i, j: (i, j)),
                  pl.BlockSpec((1, tn), lambda i, j: (0, j))],
        out_specs=pl.BlockSpec((tm, tn), lambda i, j: (i, j)),
        out_shape=jax.ShapeDtypeStruct((s, n), F32),
        compiler_params=_params(("parallel", "arbitrary"), 48),
        name="out_proj_residual",
    )(attn_n, ssm_n, w_out, w_out, x, g_m)


def _ffn_prep_kernel(h_ref, g_ref, sc_ref, sh_ref, wr_ref, xf_ref, lg_ref):
    xf = _norm_mod(h_ref[...], g_ref[...], sc_ref[...], sh_ref[...])
    xf_ref[...] = xf
    lg_ref[...] = lax.dot_general(wr_ref[...], xf, (((1,), (1,)), ((), ())),
                                  precision=HIGHEST, preferred_element_type=F32)


def _ffn_prep(h1, gain, scale, shift, w_router_t):
    s, d = h1.shape
    tm = 256
    row = pl.BlockSpec((1, d), lambda i: (0, 0))
    return pl.pallas_call(
        _ffn_prep_kernel,
        grid=(s // tm,),
        in_specs=[pl.BlockSpec((tm, d), lambda i: (i, 0)), row, row, row,
                  pl.BlockSpec((N_EXPERTS, d), lambda i: (0, 0))],
        out_specs=[pl.BlockSpec((tm, d), lambda i: (i, 0)),
                   pl.BlockSpec((N_EXPERTS, tm), lambda i: (0, i))],
        out_shape=[jax.ShapeDtypeStruct((s, d), F32),
                   jax.ShapeDtypeStruct((N_EXPERTS, s), F32)],
        compiler_params=_params(("parallel",), 40),
        name="ffn_norm_router",
    )(h1, gain, scale, shift, w_router_t)


def _route_kernel(lg_ref, sel_ref, pos_ref, aff_ref, *, cap, nk):
    e_n = N_EXPERTS
    lg = lg_ref[...]
    mx = jnp.max(lg, axis=0, keepdims=True)
    ex = jnp.exp(lg - mx)
    aff = ex / jnp.sum(ex, axis=0, keepdims=True)
    aff_ref[...] = aff
    bits = pltpu.bitcast(aff, I32)

    def count(mask):
        return jnp.sum(jnp.sum(mask.astype(F32), axis=2, keepdims=True), axis=1, keepdims=True)

    def radix(i, thr):
        cand = thr | jnp.left_shift(jnp.int32(1), 30 - i)
        return jnp.where(count(bits >= cand) >= cap, cand, thr)

    thr = lax.fori_loop(0, 31, radix, jnp.zeros((e_n, 1, 1), I32))
    gt = bits > thr
    eq = bits == thr
    need = cap - count(gt)

    li = lax.broadcasted_iota(I32, (LANES, LANES), 0)
    lj = lax.broadcasted_iota(I32, (LANES, LANES), 1)
    tri_lane = (li < lj).astype(BF16)
    ki = lax.broadcasted_iota(I32, (nk, nk), 0)
    kj = lax.broadcasted_iota(I32, (nk, nk), 1)
    tri_row = (kj < ki).astype(BF16)

    def excl_prefix(mask):
        mf = mask.astype(F32)
        within = jnp.dot(mf.reshape(e_n * nk, LANES).astype(BF16), tri_lane,
                         preferred_element_type=F32).reshape(e_n, nk, LANES)
        rowtot = jnp.broadcast_to(jnp.sum(mf, axis=2, keepdims=True), (e_n, nk, LANES)).astype(BF16)
        across = jnp.stack([jnp.dot(tri_row, rowtot[e], preferred_element_type=F32) for e in range(e_n)])
        return within + across

    sel = gt | (eq & (excl_prefix(eq) < need))
    sel_ref[...] = sel.astype(I32)
    pos_ref[...] = excl_prefix(sel).astype(I32)


def _route(logits_t, cap):
    e_n, s = logits_t.shape
    nk = s // LANES
    spec = pl.BlockSpec((e_n, nk, LANES), lambda i: (0, 0, 0))
    sel, pos, aff = pl.pallas_call(
        functools.partial(_route_kernel, cap=cap, nk=nk),
        grid=(1,),
        in_specs=[spec],
        out_specs=[spec, spec, spec],
        out_shape=[jax.ShapeDtypeStruct((e_n, nk, LANES), I32),
                   jax.ShapeDtypeStruct((e_n, nk, LANES), I32),
                   jax.ShapeDtypeStruct((e_n, nk, LANES), F32)],
        compiler_params=_params(("arbitrary",), 48),
        name="expert_choice_route",
    )(logits_t.reshape(e_n, nk, LANES))
    return sel.reshape(e_n, s), pos.reshape(e_n, s), aff.reshape(e_n, s)


def _compact_kernel(sel_hbm, aff_hbm, idx_ref, gate_ref, sel_s, aff_s, sem, *, s, cap):
    e = pl.program_id(0)
    c_sel = pltpu.make_async_copy(sel_hbm.at[e], sel_s, sem.at[0])
    c_aff = pltpu.make_async_copy(aff_hbm.at[e], aff_s, sem.at[1])
    c_sel.start()
    c_aff.start()
    c_sel.wait()
    c_aff.wait()
    idx_ref[0, 0, cap] = 0
    gate_ref[0, 0, cap] = 0.0

    def body(t, cnt):
        slot = jnp.minimum(cnt, cap)
        idx_ref[0, 0, slot] = t
        gate_ref[0, 0, slot] = aff_s[t]
        return cnt + sel_s[t]

    lax.fori_loop(0, s, body, jnp.int32(0))
    idx_ref[0, 0, cap] = 0
    gate_ref[0, 0, cap] = 0.0


def _compact(sel, aff, cap):
    e_n, s = sel.shape
    capp = cap + 1
    spec = pl.BlockSpec((1, 1, capp), lambda e: (e, 0, 0), memory_space=pltpu.SMEM)
    idx, gate = pl.pallas_call(
        functools.partial(_compact_kernel, s=s, cap=cap),
        grid=(e_n,),
        in_specs=[pl.BlockSpec(memory_space=pl.ANY), pl.BlockSpec(memory_space=pl.ANY)],
        out_specs=[spec, spec],
        out_shape=[jax.ShapeDtypeStruct((e_n, 1, capp), I32),
                   jax.ShapeDtypeStruct((e_n, 1, capp), F32)],
        scratch_shapes=[pltpu.SMEM((s,), I32), pltpu.SMEM((s,), F32), pltpu.SemaphoreType.DMA((2,))],
        compiler_params=_params(("arbitrary",), 32),
        name="expert_compact",
    )(sel, aff)
    return idx[:, 0, :cap], gate[:, 0, :cap]


def _moe_up_kernel(idx_hbm, xf_hbm, wg_ref, wu_ref, h_ref, x_scr, stage, idx_s, sem_g, sem_i, *, cap, rows):
    e = pl.program_id(0)
    f = pl.program_id(1)

    @pl.when(f == 0)
    def _gather():
        c_idx = pltpu.make_async_copy(idx_hbm.at[e], idx_s, sem_i.at[0])
        c_idx.start()
        c_idx.wait()

        def row_copy(token, slot, r):
            return pltpu.make_async_copy(xf_hbm.at[pl.ds(token, 1), :],
                                         stage.at[slot, pl.ds(r, 1), :], sem_g.at[slot])

        def issue(j, slot):
            def one(r, c):
                row_copy(idx_s[j * rows + r], slot, r).start()
                return c
            lax.fori_loop(0, rows, one, 0)

        def drain(slot):
            def one(r, c):
                row_copy(0, slot, r).wait()
                return c
            lax.fori_loop(0, rows, one, 0)

        nchunk = cap // rows
        issue(0, 0)
        for j in range(nchunk):
            slot = j % 2
            if j + 1 < nchunk:
                issue(j + 1, 1 - slot)
            drain(slot)
            x_scr[j * rows:(j + 1) * rows, :] = stage[slot].astype(BF16)

    xb = x_scr[...]
    g = jnp.dot(xb, wg_ref[0].astype(BF16), preferred_element_type=F32)
    u = jnp.dot(xb, wu_ref[0].astype(BF16), preferred_element_type=F32)
    h_ref[0] = (g * jax.nn.sigmoid(g) * u).astype(h_ref.dtype)


def _moe_up(idx, xf, w_gate, w_up, cap):
    e_n, d, ff = w_gate.shape
    tf = 256
    rows = 256
    return pl.pallas_call(
        functools.partial(_moe_up_kernel, cap=cap, rows=rows),
        grid=(e_n, ff // tf),
        in_specs=[pl.BlockSpec(memory_space=pl.ANY),
                  pl.BlockSpec(memory_space=pl.ANY),
                  pl.BlockSpec((1, d, tf), lambda e, f: (e, 0, f)),
                  pl.BlockSpec((1, d, tf), lambda e, f: (e, 0, f))],
        out_specs=pl.BlockSpec((1, cap, tf), lambda e, f: (e, 0, f)),
        out_shape=jax.ShapeDtypeStruct((e_n, cap, ff), BF16),
        scratch_shapes=[pltpu.VMEM((cap, d), BF16),
                        pltpu.VMEM((2, rows, d), F32),
                        pltpu.SMEM((cap,), I32),
                        pltpu.SemaphoreType.DMA((2,)),
                        pltpu.SemaphoreType.DMA((1,))],
        compiler_params=_params(("arbitrary", "arbitrary"), 56),
        name="moe_gather_up",
    )(idx, xf, w_gate, w_up)


def _moe_down_kernel(h_ref, wd_ref, gate_ref, y_ref):
    y = jnp.dot(h_ref[0], wd_ref[0].astype(BF16), preferred_element_type=F32)
    y_ref[0] = (y * gate_ref[0]).astype(y_ref.dtype)


def _moe_down(hdn, w_down, gates):
    e_n, cap, ff = hdn.shape
    d = w_down.shape[2]
    tn = 512
    return pl.pallas_call(
        _moe_down_kernel,
        grid=(e_n, d // tn),
        in_specs=[pl.BlockSpec((1, cap, ff), lambda e, n: (e, 0, 0)),
                  pl.BlockSpec((1, ff, tn), lambda e, n: (e, 0, n)),
                  pl.BlockSpec((1, cap, 1), lambda e, n: (e, 0, 0))],
        out_specs=pl.BlockSpec((1, cap, tn), lambda e, n: (e, 0, n)),
        out_shape=jax.ShapeDtypeStruct((e_n, cap, d), BF16),
        compiler_params=_params(("parallel", "arbitrary"), 48),
        name="moe_down",
    )(hdn, w_down, gates.reshape(e_n, cap, 1))


COMBINE_WINDOW = 64
SLOT_ALIGN = 16


def _combine_kernel(lo_ref, h_ref, sel_ref, pos_ref, gf_ref, gn_ref, y_hbm, o_ref,
                    ybuf, yextra, acc_ref, sem, sem_x, *, cap, tm):
    i = pl.program_id(0)
    win = COMBINE_WINDOW
    e_n = N_EXPERTS

    def window(e, j):
        nominal = (lo_ref[e, i] // SLOT_ALIGN) * SLOT_ALIGN + j * win
        start = pl.multiple_of(jnp.minimum(nominal, cap - win), SLOT_ALIGN)
        return nominal, start

    def first_copy(e):
        _, start = window(e, 0)
        return pltpu.make_async_copy(y_hbm.at[e, pl.ds(start, win), :],
                                     ybuf.at[pl.ds(e * win, win), :], sem.at[e])

    for e in range(e_n):
        first_copy(e).start()

    sel = sel_ref[...]
    pos = pos_ref[...]
    lane = lax.broadcasted_iota(I32, (tm, LANES), 1)
    left = lane < win
    pieces = []
    for e2 in range(e_n // 2):
        ea, eb = 2 * e2, 2 * e2 + 1
        na, sa = window(ea, 0)
        nb, sb = window(eb, 0)
        slot = jnp.where(left, sa + lane, sb + lane - win)
        nominal = jnp.where(left, na, nb)
        pe = jnp.where(left, pos[:, ea:ea + 1], pos[:, eb:eb + 1])
        se = jnp.where(left, sel[:, ea:ea + 1], sel[:, eb:eb + 1])
        pieces.append(((pe == slot) & (se > 0) & (slot >= nominal)).astype(BF16))
    onehot = jnp.concatenate(pieces, axis=1)
    for e in range(e_n):
        first_copy(e).wait()
    acc_ref[...] = jnp.dot(onehot, ybuf[...], preferred_element_type=F32)

    lane_w = lax.broadcasted_iota(I32, (tm, win), 1)
    for e in range(e_n):
        base = (lo_ref[e, i] // SLOT_ALIGN) * SLOT_ALIGN
        nwin = (lo_ref[e, i + 1] - base + win - 1) // win

        def extra(j, c, e=e):
            nominal, start = window(e, j)
            cp = pltpu.make_async_copy(y_hbm.at[e, pl.ds(start, win), :], yextra, sem_x.at[0])
            cp.start()
            cp.wait()
            slot = start + lane_w
            oh = ((pos[:, e:e + 1] == slot) & (sel[:, e:e + 1] > 0) & (slot >= nominal)).astype(BF16)
            acc_ref[...] += jnp.dot(oh, yextra[...], preferred_element_type=F32)
            return c

        lax.fori_loop(1, nwin, extra, 0)

    h2 = h_ref[...] + gf_ref[...] * acc_ref[...]
    ms = jnp.mean(h2 * h2, axis=-1, keepdims=True)
    o_ref[...] = h2 * lax.rsqrt(ms + EPS) * gn_ref[...]


def _combine(h1, y, sel_tm, pos_tm, lo, g_f, gain, cap):
    s, d = h1.shape
    tm = 256
    win = COMBINE_WINDOW
    row = pl.BlockSpec((1, d), lambda i, lo_r: (0, 0))
    tok = pl.BlockSpec((tm, N_EXPERTS), lambda i, lo_r: (i, 0))
    grid_spec = pltpu.PrefetchScalarGridSpec(
        num_scalar_prefetch=1,
        grid=(s // tm,),
        in_specs=[pl.BlockSpec((tm, d), lambda i, lo_r: (i, 0)), tok, tok, row, row,
                  pl.BlockSpec(memory_space=pl.ANY)],
        out_specs=pl.BlockSpec((tm, d), lambda i, lo_r: (i, 0)),
        scratch_shapes=[pltpu.VMEM((N_EXPERTS * win, d), BF16),
                        pltpu.VMEM((win, d), BF16),
                        pltpu.VMEM((tm, d), F32),
                        pltpu.SemaphoreType.DMA((N_EXPERTS,)),
                        pltpu.SemaphoreType.DMA((1,))],
    )
    return pl.pallas_call(
        functools.partial(_combine_kernel, cap=cap, tm=tm),
        grid_spec=grid_spec,
        out_shape=jax.ShapeDtypeStruct((s, d), F32),
        compiler_params=_params(("arbitrary",), 48),
        name="moe_combine_norm",
    )(lo, h1, sel_tm, pos_tm, g_f, gain, y)


def kernel(x, c, positions, w_ada, b_ada, norm_mix_gain, w_in, lam_re_fwd, lam_im_fwd, log_dt_fwd, lam_re_bwd, lam_im_bwd, log_dt_bwd, ssm_b_re, ssm_b_im, ssm_c_re, ssm_c_im, ssm_d, w_glu, b_glu, norm_attn_out_gain, norm_ssm_out_gain, w_out, norm_ffn_gain, w_router, w_exp_gate, w_exp_up, w_exp_down, norm_final_gain):
    batch, s, d = x.shape
    depth = w_ada.shape[0]
    assert batch == 1 and d == D_MODEL and depth == 1
    cap =max(1, CAPACITY_FACTOR * s // N_EXPERTS)
    h = x[0]
    pos = positions[0]
    cos_t, sin1_t, sin2_t = _rope_tables(pos)
    for layer in range(depth):
        mod = _adaln(c, w_ada[layer], b_ada[layer])
        sh_m, sc_m, g_m, sh_f, sc_f, g_f = jnp.split(mod, 6, axis=-1)

        xm = _norm_mod_call(h, norm_mix_gain[layer][None], sc_m, sh_m, BF16)
        proj = _in_proj(xm, w_in[layer].astype(BF16), cos_t, sin1_t, sin2_t)
        outs, lses = [], []
        for dil in DILATIONS:
            o, l = _dilated_attention(proj, dil)
            outs.append(o)
            lses.append(l)
        attn_n = _merge_patterns(outs, lses, norm_attn_out_gain[layer][None])

        nc = s // SSM_CHUNK
        u_t = proj[:, 3 * ATTN_WIDTH:].reshape(nc, SSM_CHUNK, SSM_GROUPS, SSM_GROUP)
        u_t = jnp.transpose(u_t, (2, 0, 1, 3)).reshape(SSM_GROUPS, nc, SSM_CHUNK * SSM_GROUP)
        w_t, p_all, q_all, a_all = _ssm_operators(
            lam_re_fwd[layer], lam_im_fwd[layer], log_dt_fwd[layer],
            lam_re_bwd[layer], lam_im_bwd[layer], log_dt_bwd[layer],
            ssm_b_re[layer], ssm_b_im[layer], ssm_c_re[layer], ssm_c_im[layer])
        d_t = jnp.tile(ssm_d[layer].reshape(SSM_GROUPS, 1, SSM_GROUP), (1, 1, SSM_CHUNK))
        g_t = _ssm_mixer(u_t, w_t, p_all, q_all, a_all, d_t)
        g_nat = jnp.transpose(g_t.reshape(SSM_GROUPS, nc, SSM_CHUNK, SSM_GROUP), (1, 2, 0, 3)).reshape(s, SSM_WIDTH)
        ssm_n = _glu_norm(g_nat, w_glu[layer].astype(BF16), b_glu[layer][None], norm_ssm_out_gain[layer][None])

        h = _out_proj(attn_n, ssm_n, w_out[layer].astype(BF16), h, g_m)

        xf, logits_t = _ffn_prep(h, norm_ffn_gain[layer][None], sc_f, sh_f, jnp.transpose(w_router[layer]))
        sel, slot, aff = _route(logits_t, cap)
        idx, gates = _compact(sel, aff, cap)
        hdn = _moe_up(idx, xf, w_exp_gate[layer], w_exp_up[layer], cap)
        y = _moe_down(hdn, w_exp_down[layer], gates)
        tm = 256
        lo = jnp.concatenate([slot[:, ::tm], jnp.full((N_EXPERTS, 1), cap, I32)], axis=1)
        h = _combine(h, y, jnp.transpose(sel), jnp.transpose(slot), lo, g_f, norm_final_gain[None], cap)
    return h[None]
```

```python
import functools
import math

import jax
import jax.numpy as jnp
from jax import lax
from jax.experimental import pallas as pl
from jax.experimental.pallas import tpu as pltpu

F32 = jnp.float32
BF16 = jnp.bfloat16
I32 = jnp.int32
HIGHEST = lax.Precision.HIGHEST

D_MODEL = 4096
ATTN_WIDTH = 2048
SSM_WIDTH = 2048
HEAD_DIM = 128
N_HEADS = 16
IN_PROJ_WIDTH = 3 * ATTN_WIDTH + SSM_WIDTH
ROPE_DIM = 32
ROPE_THETA = 500000.0
DILATIONS = (1, 4, 16)
ATTN_HALF = 64
NEG_INF = -1e30
SSM_GROUP = 16
SSM_GROUPS = 128
SSM_STATE = 64
SSM_CHUNK = 16
N_EXPERTS = 16
EXPERT_FF = 2048
CAPACITY_FACTOR = 2
EPS = 1e-6

LANES = 128
MIB = 1024 * 1024


def _params(semantics, vmem_mib):
    return pltpu.CompilerParams(dimension_semantics=semantics, vmem_limit_bytes=vmem_mib * MIB)


def _ada_kernel(c_ref, w_ref, b_ref, o_ref):
    c = c_ref[...]
    ca = c * jax.nn.sigmoid(c)
    o_ref[...] = jnp.dot(ca, w_ref[...], precision=HIGHEST, preferred_element_type=F32) + b_ref[...]


def _adaln(c, w_ada, b_ada):
    d, n = w_ada.shape
    tn = 512
    c8 = jnp.pad(c, ((0, 7), (0, 0)))
    out = pl.pallas_call(
        _ada_kernel,
        grid=(n // tn,),
        in_specs=[pl.BlockSpec((8, d), lambda j: (0, 0)),
                  pl.BlockSpec((d, tn), lambda j: (0, j)),
                  pl.BlockSpec((1, tn), lambda j: (0, j))],
        out_specs=pl.BlockSpec((8, tn), lambda j: (0, j)),
        out_shape=jax.ShapeDtypeStruct((8, n), F32),
        compiler_params=_params(("parallel",), 40),
        name="adaln",
    )(c8, w_ada, b_ada.reshape(1, n))
    return out[0:1]


def _norm_mod(x, gain, scale, shift):
    ms = jnp.mean(x * x, axis=-1, keepdims=True)
    y = x * lax.rsqrt(ms + EPS) * gain
    return y * (1.0 + scale) + shift


def _norm_mod_kernel(x_ref, g_ref, sc_ref, sh_ref, o_ref):
    o_ref[...] = _norm_mod(x_ref[...], g_ref[...], sc_ref[...], sh_ref[...]).astype(o_ref.dtype)


def _norm_mod_call(x, gain, scale, shift, out_dtype):
    s, d = x.shape
    tm = 256
    row = pl.BlockSpec((1, d), lambda i: (0, 0))
    return pl.pallas_call(
        _norm_mod_kernel,
        grid=(s // tm,),
        in_specs=[pl.BlockSpec((tm, d), lambda i: (i, 0)), row, row, row],
        out_specs=pl.BlockSpec((tm, d), lambda i: (i, 0)),
        out_shape=jax.ShapeDtypeStruct((s, d), out_dtype),
        compiler_params=_params(("parallel",), 40),
        name="norm_mod",
    )(x, gain, scale, shift)


def _rope_kernel(pos_ref, c_ref, s1_ref, s2_ref):
    half = ROPE_DIM // 2
    pos = pos_ref[...].astype(F32)
    lane = lax.broadcasted_iota(I32, (1, LANES), 1)
    fidx = (lane & (half - 1)).astype(F32)
    inv_freq = jnp.exp(fidx * (-math.log(ROPE_THETA) / half))
    ang = pos * inv_freq
    cs = jnp.cos(ang)
    sn = jnp.sin(ang)
    c_ref[...] = jnp.where(lane < ROPE_DIM, cs, 1.0)
    s1_ref[...] = jnp.where(lane < half, -sn, 0.0)
    s2_ref[...] = jnp.where((lane >= half) & (lane < ROPE_DIM), sn, 0.0)


def _rope_tables(positions):
    s = positions.shape[0]
    tm = 512
    spec = pl.BlockSpec((tm, LANES), lambda i: (i, 0))
    shp = jax.ShapeDtypeStruct((s, LANES), F32)
    return pl.pallas_call(
        _rope_kernel,
        grid=(s // tm,),
        in_specs=[pl.BlockSpec((tm, 1), lambda i: (i, 0))],
        out_specs=[spec, spec, spec],
        out_shape=[shp, shp, shp],
        compiler_params=_params(("parallel",), 32),
        name="rope_tables",
    )(positions.reshape(s, 1))


PROJ_TM = 1024
PROJ_TN = 512


def _piece_transpose8(xs):
    xs = list(xs)
    lane = lax.broadcasted_iota(I32, xs[0].shape, 1)
    for h in (4, 2, 1):
        low = (lane & (16 * h)) == 0
        for a in range(8):
            if a & h:
                continue
            lo, hi = xs[a], xs[a + h]
            xs[a] = jnp.where(low, lo, pltpu.roll(hi, 16 * h, 1))
            xs[a + h] = jnp.where(low, pltpu.roll(lo, LANES - 16 * h, 1), hi)
    return xs


def _qkv_kernel(a_ref, w_ref, c_ref, s1_ref, s2_ref, nat_ref, p4_ref, p16_ref, slab, *, tm, tn):
    j = pl.program_id(1)
    acc = jnp.dot(a_ref[...], w_ref[...], preferred_element_type=F32)
    rep = tn // HEAD_DIM
    n_qk = 2 * ATTN_WIDTH // tn
    n_q = ATTN_WIDTH // tn

    def finish(val):
        nat_ref[...] = val.astype(nat_ref.dtype)
        for c in range(rep):
            slab[c] = val[:, c * LANES:(c + 1) * LANES]
        for d, ref in ((4, p4_ref), (16, p16_ref)):
            for r in range(d):
                for c in range(rep):
                    ref[r, :, c * LANES:(c + 1) * LANES] = slab[c, pl.ds(r, tm // d, stride=d), :].astype(ref.dtype)

    @pl.when(j < n_qk)
    def _():
        c = jnp.concatenate([c_ref[...]] * rep, axis=1)
        s1 = jnp.concatenate([s1_ref[...]] * rep, axis=1)
        s2 = jnp.concatenate([s2_ref[...]] * rep, axis=1)
        half = ROPE_DIM // 2
        rot = acc * c + pltpu.roll(acc, tn - half, 1) * s1 + pltpu.roll(acc, half, 1) * s2
        scale = jnp.where(j < n_q, HEAD_DIM ** -0.5 * math.log2(math.e), 1.0).astype(F32)
        finish(rot * scale)

    @pl.when(j >= n_qk)
    def _():
        finish(acc)


def _qkv_proj(xm, w_qkv, cos_t, sin1_t, sin2_t):
    s, d = xm.shape
    n = w_qkv.shape[1]
    tm, tn = PROJ_TM, PROJ_TN
    tab = pl.BlockSpec((tm, LANES), lambda i, j: (i, 0))
    return pl.pallas_call(
        functools.partial(_qkv_kernel, tm=tm, tn=tn),
        grid=(s // tm, n // tn),
        in_specs=[pl.BlockSpec((tm, d), lambda i, j: (i, 0)),
                  pl.BlockSpec((d, tn), lambda i, j: (0, j)),
                  tab, tab, tab],
        out_specs=[pl.BlockSpec((tm, tn), lambda i, j: (i, j)),
                   pl.BlockSpec((4, tm // 4, tn), lambda i, j: (0, i, j)),
                   pl.BlockSpec((16, tm // 16, tn), lambda i, j: (0, i, j))],
        out_shape=[jax.ShapeDtypeStruct((s, n), BF16),
                   jax.ShapeDtypeStruct((4, s // 4, n), BF16),
                   jax.ShapeDtypeStruct((16, s // 16, n), BF16)],
        scratch_shapes=[pltpu.VMEM((tn // LANES, tm, LANES), F32)],
        compiler_params=_params(("parallel", "arbitrary"), 52),
        name="qkv_proj",
    )(xm, w_qkv, cos_t, sin1_t, sin2_t)


def _uproj_kernel(a_ref, w_ref, u_ref, slab, *, tm, tn):
    acc = jnp.dot(a_ref[...], w_ref[...], preferred_element_type=F32)
    nchunk = tm // SSM_CHUNK
    for c in range(tn // LANES):
        slab[c] = acc[:, c * LANES:(c + 1) * LANES]
    for c in range(tn // LANES):
        rows = [slab[c, pl.ds(i, nchunk, stride=SSM_CHUNK), :] for i in range(SSM_CHUNK)]
        first = _piece_transpose8(rows[:8])
        second = _piece_transpose8(rows[8:])
        for gp in range(8):
            u_ref[c * 8 + gp, :, 0:LANES] = first[gp].astype(u_ref.dtype)
            u_ref[c * 8 + gp, :, LANES:2 * LANES] = second[gp].astype(u_ref.dtype)


def _u_proj(xm, w_u):
    s, d = xm.shape
    n = w_u.shape[1]
    tm, tn = PROJ_TM, PROJ_TN
    gpt = tn // SSM_GROUP
    return pl.pallas_call(
        functools.partial(_uproj_kernel, tm=tm, tn=tn),
        grid=(s // tm, n // tn),
        in_specs=[pl.BlockSpec((tm, d), lambda i, j: (i, 0)),
                  pl.BlockSpec((d, tn), lambda i, j: (0, j))],
        out_specs=pl.BlockSpec((gpt, tm // SSM_CHUNK, SSM_CHUNK * SSM_GROUP), lambda i, j: (j, i, 0)),
        out_shape=jax.ShapeDtypeStruct((n // SSM_GROUP, s // SSM_CHUNK, SSM_CHUNK * SSM_GROUP), BF16),
        scratch_shapes=[pltpu.VMEM((tn // LANES, tm, LANES), F32)],
        compiler_params=_params(("parallel", "arbitrary"), 48),
        name="u_proj",
    )(xm, w_u)


def _attn_kernel(q_ref, k_ref, v_ref, o_ref, l_ref, *, n):
    tq = LANES
    tk = 2 * LANES
    nsb = n // tq
    col_minus_row = lax.broadcasted_iota(I32, (tq, tk), 1) - lax.broadcasted_iota(I32, (tq, tk), 0)
    eye = lax.broadcasted_iota(I32, (tq, tq), 0) == lax.broadcasted_iota(I32, (tq, tq), 1)

    def band_bias(key_start_minus_q0):
        return jnp.where(jnp.abs(col_minus_row + key_start_minus_q0) <= ATTN_HALF, 0.0, NEG_INF).astype(F32)

    def block(sb, q0, ks, bias):
        q = q_ref[pl.ds(q0, tq), :]
        k = k_ref[pl.ds(ks, tk), :]
        v = v_ref[pl.ds(ks, tk), :]
        s = lax.dot_general(q, k, (((1,), (1,)), ((), ())), preferred_element_type=F32) + bias
        m = jnp.max(s, axis=-1, keepdims=True)
        p = jnp.exp2(s - m)
        l = jnp.sum(p, axis=-1, keepdims=True)
        o = jnp.dot(p.astype(BF16), v, preferred_element_type=F32) / l
        o_ref[pl.ds(q0, tq), :] = o.astype(o_ref.dtype)
        lse = (m + jnp.log2(l)) * math.log(2.0)
        l_ref[0, pl.ds(sb, 1), :] = jnp.sum(jnp.where(eye, lse, 0.0), axis=0, keepdims=True)

    block(0, 0, 0, band_bias(0))
    block(nsb - 1, n - tq, n - tk, band_bias(-tq))
    interior = nsb - 2
    if interior > 0:
        unroll = max(u for u in (6, 5, 4, 3, 2, 1) if interior % u == 0)
        bias = band_bias(-ATTN_HALF)

        def body(it, carry):
            for u in range(unroll):
                sb = 1 + it * unroll + u
                q0 = pl.multiple_of(sb * tq, tq)
                block(sb, q0, pl.multiple_of(q0 - ATTN_HALF, ATTN_HALF), bias)
            return carry

        lax.fori_loop(0, interior // unroll, body, 0)


def _dilated_attention(qkv, dil):
    _, n, width = qkv.shape
    s = n * dil
    nsb = n // LANES

    def spec(off):
        return pl.BlockSpec((None, n, HEAD_DIM), lambda r, h: (r, 0, off + h))

    o, lse = pl.pallas_call(
        functools.partial(_attn_kernel, n=n),
        grid=(dil, N_HEADS),
        in_specs=[spec(0), spec(N_HEADS), spec(2 * N_HEADS)],
        out_specs=[pl.BlockSpec((None, n, HEAD_DIM), lambda r, h: (r, 0, h)),
                   pl.BlockSpec((1, nsb, LANES), lambda r, h: (r * N_HEADS + h, 0, 0))],
        out_shape=[jax.ShapeDtypeStruct((dil, n, ATTN_WIDTH), BF16),
                   jax.ShapeDtypeStruct((dil * N_HEADS, nsb, LANES), F32)],
        compiler_params=_params(("parallel", "parallel"), 48),
        name=f"dilated_attn_d{dil}",
    )(qkv, qkv, qkv)
    lse = lse.reshape(dil, N_HEADS, n).transpose(2, 0, 1).reshape(s, N_HEADS)
    return o, lse


def _merge_kernel(o1_ref, o4_ref, o16_ref, l1_ref, l2_ref, l3_ref, g_ref, out_ref, acc_ref, s4_ref, s16_ref, *, tm):
    for d, src, dst in ((4, o4_ref, s4_ref), (16, o16_ref, s16_ref)):
        for r in range(d):
            for h in range(N_HEADS):
                dst[h, pl.ds(r, tm // d, stride=d), :] = src[r, :, h * HEAD_DIM:(h + 1) * HEAD_DIM].astype(F32)
    la, lb, lc = l1_ref[...], l2_ref[...], l3_ref[...]
    m = jnp.maximum(jnp.maximum(la, lb), lc)
    ea, eb, ec = jnp.exp(la - m), jnp.exp(lb - m), jnp.exp(lc - m)
    den = ea + eb + ec
    wa, wb, wc = ea / den, eb / den, ec / den
    sq = jnp.zeros((tm, 1), F32)
    for h in range(N_HEADS):
        cs = slice(h * HEAD_DIM, (h + 1) * HEAD_DIM)
        slab = (wa[:, h:h + 1] * o1_ref[:, cs].astype(F32)
                + wb[:, h:h + 1] * s4_ref[h]
                + wc[:, h:h + 1] * s16_ref[h])
        acc_ref[:, cs] = slab
        sq = sq + jnp.sum(slab * slab, axis=-1, keepdims=True)
    inv = lax.rsqrt(sq * (1.0 / ATTN_WIDTH) + EPS)
    out_ref[...] = (acc_ref[...] * inv * g_ref[...]).astype(out_ref.dtype)


def _merge_patterns(outs, lses, gain):
    o1, o4, o16 = outs
    s = o1.shape[1]
    tm = 256
    ospec = pl.BlockSpec((tm, ATTN_WIDTH), lambda i: (i, 0))
    lspec = pl.BlockSpec((tm, N_HEADS), lambda i: (i, 0))
    return pl.pallas_call(
        functools.partial(_merge_kernel, tm=tm),
        grid=(s // tm,),
        in_specs=[pl.BlockSpec((None, tm, ATTN_WIDTH), lambda i: (0, i, 0)),
                  pl.BlockSpec((4, tm // 4, ATTN_WIDTH), lambda i: (0, i, 0)),
                  pl.BlockSpec((16, tm // 16, ATTN_WIDTH), lambda i: (0, i, 0)),
                  lspec, lspec, lspec,
                  pl.BlockSpec((1, ATTN_WIDTH), lambda i: (0, 0))],
        out_specs=ospec,
        out_shape=jax.ShapeDtypeStruct((s, ATTN_WIDTH), BF16),
        scratch_shapes=[pltpu.VMEM((tm, ATTN_WIDTH), F32),
                        pltpu.VMEM((N_HEADS, tm, HEAD_DIM), F32),
                        pltpu.VMEM((N_HEADS, tm, HEAD_DIM), F32)],
        compiler_params=_params(("parallel",), 32),
        name="attn_merge_norm",
    )(o1, o4, o16, *lses, gain)


def _ssm_direction_terms(lam_re, lam_im, log_dt, b_re, b_im, c_re, c_im):
    dt = jnp.exp(log_dt)[:, None]
    mag = jnp.exp(lam_re * dt)
    ang = lam_im * dt
    lb_re = mag * jnp.cos(ang)
    lb_im = mag * jnp.sin(ang)
    den = lam_re * lam_re + lam_im * lam_im
    ar = lb_re - 1.0
    ai = lb_im
    coef_re = (ar * lam_re + ai * lam_im) / den
    coef_im = (ai * lam_re - ar * lam_im) / den
    bb_re = coef_re[..., None] * b_re - coef_im[..., None] * b_im
    bb_im = coef_re[..., None] * b_im + coef_im[..., None] * b_re
    tau = jnp.arange(SSM_CHUNK + 1, dtype=F32)[:, None, None]
    pw_mag = jnp.exp(tau * (lam_re * dt)[None])
    pw_re = pw_mag * jnp.cos(tau * ang[None])
    pw_im = pw_mag * jnp.sin(tau * ang[None])
    e_re = pw_re[..., None] * bb_re[None] - pw_im[..., None] * bb_im[None]
    e_im = pw_re[..., None] * bb_im[None] + pw_im[..., None] * bb_re[None]
    kern = (jnp.einsum('ghp,tgpi->tghi', c_re, e_re[:SSM_CHUNK], precision=HIGHEST)
            - jnp.einsum('ghp,tgpi->tghi', c_im, e_im[:SSM_CHUNK], precision=HIGHEST))
    ca_re = c_re[None] * pw_re[:, :, None, :] - c_im[None] * pw_im[:, :, None, :]
    ca_im = c_re[None] * pw_im[:, :, None, :] + c_im[None] * pw_re[:, :, None, :]
    return kern, e_re, e_im, ca_re, ca_im, pw_re[SSM_CHUNK], pw_im[SSM_CHUNK]


def _ssm_operators(lam_re_f, lam_im_f, log_dt_f, lam_re_b, lam_im_b, log_dt_b, b_re, b_im, c_re, c_im):
    L, G, P = SSM_CHUNK, SSM_GROUPS, SSM_STATE
    kf, ef_re, ef_im, caf_re, caf_im, af_re, af_im = _ssm_direction_terms(
        lam_re_f, lam_im_f, log_dt_f, b_re, b_im, c_re, c_im)
    kb, eb_re, eb_im, cab_re, cab_im, ab_re, ab_im = _ssm_direction_terms(
        lam_re_b, lam_im_b, log_dt_b, b_re, b_im, c_re, c_im)
    i_idx = jnp.arange(L)[:, None]
    j_idx = jnp.arange(L)[None, :]
    lag = j_idx - i_idx
    wf = jnp.where((lag >= 0)[:, :, None, None, None], kf[jnp.clip(lag, 0, L - 1)], 0.0)
    wb = jnp.where((lag <= 0)[:, :, None, None, None], kb[jnp.clip(-lag, 0, L - 1)], 0.0)
    w = jnp.transpose(wf + wb, (2, 0, 4, 1, 3)).reshape(G, L * SSM_GROUP, L * SSM_GROUP)

    def p_mat(e, order):
        return jnp.transpose(e[order], (1, 0, 3, 2)).reshape(G, L * SSM_GROUP, P)

    def q_mat(ca, order):
        return jnp.transpose(ca[order], (1, 3, 0, 2)).reshape(G, P, L * SSM_GROUP)

    f_ord = (L - 1) - jnp.arange(L)
    b_ord = jnp.arange(L)
    parts_p = [p_mat(ef_re, f_ord), p_mat(ef_im, f_ord), p_mat(eb_re, b_ord), p_mat(eb_im, b_ord)]
    parts_q = [q_mat(caf_re, jnp.arange(L) + 1), -q_mat(caf_im, jnp.arange(L) + 1),
               q_mat(cab_re, L - jnp.arange(L)), -q_mat(cab_im, L - jnp.arange(L))]
    even = (jnp.arange(G) % 2 == 0)[:, None, None]

    def pad_p(x):
        z = jnp.zeros_like(x)
        return jnp.where(even, jnp.concatenate([x, z], -1), jnp.concatenate([z, x], -1))

    def pad_q(x):
        z = jnp.zeros_like(x)
        return jnp.where(even, jnp.concatenate([x, z], 1), jnp.concatenate([z, x], 1))

    p_all = jnp.stack([pad_p(x) for x in parts_p], axis=1).astype(BF16)
    q_all = jnp.stack([pad_q(x) for x in parts_q], axis=1).astype(BF16)

    def pair(x):
        return x.reshape(G // 2, 1, 2 * P)

    a_all = jnp.concatenate([pair(af_re), pair(af_im), pair(ab_re), pair(ab_im)], axis=1)
    return w.astype(BF16), p_all, q_all, a_all


def _gelu_tanh(x):
    return 0.5 * x * (1.0 + jnp.tanh(math.sqrt(2.0 / math.pi) * (x + 0.044715 * (x * x * x))))


def _ssm_kernel(u_ref, w_ref, p_ref, q_ref, a_ref, d_ref, y_ref, sfr, sfi, sbr, sbi, *, nc, gb):
    npair = gb // 2
    for g in range(gb):
        cs = slice((g // 2) * LANES, (g // 2 + 1) * LANES)
        u = u_ref[g]
        for k, scr in enumerate((sfr, sfi, sbr, sbi)):
            contrib = jnp.dot(u, p_ref[g, k], preferred_element_type=F32)
            if g % 2 == 0:
                scr[:, cs] = contrib
            else:
                scr[:, cs] += contrib

    a = a_ref[...]
    af_re = jnp.concatenate([a[p, 0:1] for p in range(npair)], axis=1)
    af_im = jnp.concatenate([a[p, 1:2] for p in range(npair)], axis=1)
    ab_re = jnp.concatenate([a[p, 2:3] for p in range(npair)], axis=1)
    ab_im = jnp.concatenate([a[p, 3:4] for p in range(npair)], axis=1)

    def step(c, carry):
        hfr, hfi, hbr, hbi = carry
        cb = nc - 1 - c
        rfr = sfr[pl.ds(c, 1), :]
        rfi = sfi[pl.ds(c, 1), :]
        rbr = sbr[pl.ds(cb, 1), :]
        rbi = sbi[pl.ds(cb, 1), :]
        sfr[pl.ds(c, 1), :] = hfr
        sfi[pl.ds(c, 1), :] = hfi
        sbr[pl.ds(cb, 1), :] = hbr
        sbi[pl.ds(cb, 1), :] = hbi
        return (af_re * hfr - af_im * hfi + rfr, af_re * hfi + af_im * hfr + rfi,
                ab_re * hbr - ab_im * hbi + rbr, ab_re * hbi + ab_im * hbr + rbi)

    z = jnp.zeros((1, npair * LANES), F32)
    lax.fori_loop(0, nc, step, (z, z, z, z))

    for g in range(gb):
        cs = slice((g // 2) * LANES, (g // 2 + 1) * LANES)
        u = u_ref[g]
        y = jnp.dot(u, w_ref[g], preferred_element_type=F32)
        for k, scr in enumerate((sfr, sfi, sbr, sbi)):
            y = y + jnp.dot(scr[:, cs].astype(BF16), q_ref[g, k], preferred_element_type=F32)
        y = y + d_ref[g] * u.astype(F32)
        y_ref[g] = _gelu_tanh(y).astype(y_ref.dtype)


def _ssm_mixer(u_t, w, p_all, q_all, a_all, d_t):
    g, nc, k = u_t.shape
    gb = 8
    return pl.pallas_call(
        functools.partial(_ssm_kernel, nc=nc, gb=gb),
        grid=(g // gb,),
        in_specs=[pl.BlockSpec((gb, nc, k), lambda i: (i, 0, 0)),
                  pl.BlockSpec((gb, k, k), lambda i: (i, 0, 0)),
                  pl.BlockSpec((gb, 4, k, LANES), lambda i: (i, 0, 0, 0)),
                  pl.BlockSpec((gb, 4, LANES, k), lambda i: (i, 0, 0, 0)),
                  pl.BlockSpec((gb // 2, 4, LANES), lambda i: (i, 0, 0)),
                  pl.BlockSpec((gb, 1, k), lambda i: (i, 0, 0))],
        out_specs=pl.BlockSpec((gb, nc, k), lambda i: (i, 0, 0)),
        out_shape=jax.ShapeDtypeStruct((g, nc, k), BF16),
        scratch_shapes=[pltpu.VMEM((nc, (gb // 2) * LANES), F32)] * 4,
        compiler_params=_params(("parallel",), 48),
        name="ssm_scan",
    )(u_t, w, p_all, q_all, a_all, d_t)


def _glu_kernel(g_ref, w_ref, b_ref, n_ref, o_ref, nat, gb, *, tm):
    nchunk = tm // SSM_CHUNK
    for sl in range(SSM_WIDTH // LANES):
        for half in range(2):
            xs = [g_ref[8 * sl + gp, :, half * LANES:(half + 1) * LANES].astype(F32) for gp in range(8)]
            ys = _piece_transpose8(xs)
            for jj in range(8):
                nat[sl, pl.ds(half * 8 + jj, nchunk, stride=SSM_CHUNK), :] = ys[jj]
        gb[:, sl * LANES:(sl + 1) * LANES] = nat[sl].astype(BF16)
    z = jnp.dot(gb[...], w_ref[...], preferred_element_type=F32) + b_ref[...]
    sq = jnp.zeros((tm, 1), F32)
    for sl in range(SSM_WIDTH // LANES):
        cs = slice(sl * LANES, (sl + 1) * LANES)
        out = nat[sl] * jax.nn.sigmoid(z[:, cs])
        nat[sl] = out
        sq = sq + jnp.sum(out * out, axis=-1, keepdims=True)
    inv = lax.rsqrt(sq * (1.0 / SSM_WIDTH) + EPS)
    for sl in range(SSM_WIDTH // LANES):
        cs = slice(sl * LANES, (sl + 1) * LANES)
        o_ref[:, cs] = (nat[sl] * inv * n_ref[:, cs]).astype(o_ref.dtype)


def _glu_norm(g_t, w_glu, b_glu, gain):
    ng, nc, k = g_t.shape
    s, d = nc * SSM_CHUNK, ng * SSM_GROUP
    tm = 512
    row = pl.BlockSpec((1, d), lambda i: (0, 0))
    return pl.pallas_call(
        functools.partial(_glu_kernel, tm=tm),
        grid=(s // tm,),
        in_specs=[pl.BlockSpec((ng, tm // SSM_CHUNK, k), lambda i: (0, i, 0)),
                  pl.BlockSpec((d, d), lambda i: (0, 0)), row, row],
        out_specs=pl.BlockSpec((tm, d), lambda i: (i, 0)),
        out_shape=jax.ShapeDtypeStruct((s, d), BF16),
        scratch_shapes=[pltpu.VMEM((d // LANES, tm, LANES), F32),
                        pltpu.VMEM((tm, d), BF16)],
        compiler_params=_params(("parallel",), 48),
        name="ssm_glu_norm",
    )(g_t, w_glu, b_glu, gain)


def _outproj_kernel(a1_ref, a2_ref, w1_ref, w2_ref, x_ref, gm_ref, o_ref):
    mix = (jnp.dot(a1_ref[...], w1_ref[...], preferred_element_type=F32)
           + jnp.dot(a2_ref[...], w2_ref[...], preferred_element_type=F32))
    o_ref[...] = x_ref[...] + gm_ref[...] * mix


def _out_proj(attn_n, ssm_n, w_out, x, g_m):
    s, k = attn_n.shape
    n = w_out.shape[1]
    tm, tn = 1024, 512
    return pl.pallas_call(
        _outproj_kernel,
        grid=(s // tm, n // tn),
        in_specs=[pl.BlockSpec((tm, k), lambda i, j: (i, 0)),
                  pl.BlockSpec((tm, k), lambda i, j: (i, 0)),
                  pl.BlockSpec((k, tn), lambda i, j: (0, j)),
                  pl.BlockSpec((k, tn), lambda i, j: (1, j)),
                  pl.BlockSpec((tm, tn), lambda i, j: (i, j)),
                  pl.BlockSpec((1, tn), lambda i, j: (0, j))],
        out_specs=pl.BlockSpec((tm, tn), lambda i, j: (i, j)),
        out_shape=jax.ShapeDtypeStruct((s, n), F32),
        compiler_params=_params(("parallel", "arbitrary"), 48),
        name="out_proj_residual",
    )(attn_n, ssm_n, w_out, w_out, x, g_m)


def _ffn_prep_kernel(h_ref, g_ref, sc_ref, sh_ref, wr_ref, xf_ref, lg_ref):
    xf = _norm_mod(h_ref[...], g_ref[...], sc_ref[...], sh_ref[...])
    xf_ref[...] = xf
    lg_ref[...] = lax.dot_general(wr_ref[...], xf, (((1,), (1,)), ((), ())),
                                  precision=HIGHEST, preferred_element_type=F32)


def _ffn_prep(h1, gain, scale, shift, w_router_t):
    s, d = h1.shape
    tm = 256
    row = pl.BlockSpec((1, d), lambda i: (0, 0))
    return pl.pallas_call(
        _ffn_prep_kernel,
        grid=(s // tm,),
        in_specs=[pl.BlockSpec((tm, d), lambda i: (i, 0)), row, row, row,
                  pl.BlockSpec((N_EXPERTS, d), lambda i: (0, 0))],
        out_specs=[pl.BlockSpec((tm, d), lambda i: (i, 0)),
                   pl.BlockSpec((N_EXPERTS, tm), lambda i: (0, i))],
        out_shape=[jax.ShapeDtypeStruct((s, d), F32),
                   jax.ShapeDtypeStruct((N_EXPERTS, s), F32)],
        compiler_params=_params(("parallel",), 40),
        name="ffn_norm_router",
    )(h1, gain, scale, shift, w_router_t)


def _route_kernel(lg_ref, sel_ref, pos_ref, aff_ref, *, cap, nk):
    e_n = N_EXPERTS
    lg = lg_ref[...]
    mx = jnp.max(lg, axis=0, keepdims=True)
    ex = jnp.exp(lg - mx)
    aff = ex / jnp.sum(ex, axis=0, keepdims=True)
    aff_ref[...] = aff
    bits = pltpu.bitcast(aff, I32)

    def count(mask):
        return jnp.sum(jnp.sum(mask.astype(F32), axis=2, keepdims=True), axis=1, keepdims=True)

    def radix(i, thr):
        cand = thr | jnp.left_shift(jnp.int32(1), 30 - i)
        return jnp.where(count(bits >= cand) >= cap, cand, thr)

    thr = lax.fori_loop(0, 31, radix, jnp.zeros((e_n, 1, 1), I32))
    gt = bits > thr
    eq = bits == thr
    need = cap - count(gt)

    li = lax.broadcasted_iota(I32, (LANES, LANES), 0)
    lj = lax.broadcasted_iota(I32, (LANES, LANES), 1)
    tri_lane = (li < lj).astype(BF16)
    ki = lax.broadcasted_iota(I32, (nk, nk), 0)
    kj = lax.broadcasted_iota(I32, (nk, nk), 1)
    tri_row = (kj < ki).astype(BF16)

    def excl_prefix(mask):
        mf = mask.astype(F32)
        within = jnp.dot(mf.reshape(e_n * nk, LANES).astype(BF16), tri_lane,
                         preferred_element_type=F32).reshape(e_n, nk, LANES)
        rowtot = jnp.broadcast_to(jnp.sum(mf, axis=2, keepdims=True), (e_n, nk, LANES)).astype(BF16)
        across = jnp.stack([jnp.dot(tri_row, rowtot[e], preferred_element_type=F32) for e in range(e_n)])
        return within + across

    sel = gt | (eq & (excl_prefix(eq) < need))
    sel_ref[...] = sel.astype(I32)
    pos_ref[...] = excl_prefix(sel).astype(I32)


def _route(logits_t, cap):
    e_n, s = logits_t.shape
    nk = s // LANES
    spec = pl.BlockSpec((e_n, nk, LANES), lambda i: (0, 0, 0))
    sel, pos, aff = pl.pallas_call(
        functools.partial(_route_kernel, cap=cap, nk=nk),
        grid=(1,),
        in_specs=[spec],
        out_specs=[spec, spec, spec],
        out_shape=[jax.ShapeDtypeStruct((e_n, nk, LANES), I32),
                   jax.ShapeDtypeStruct((e_n, nk, LANES), I32),
                   jax.ShapeDtypeStruct((e_n, nk, LANES), F32)],
        compiler_params=_params(("arbitrary",), 48),
        name="expert_choice_route",
    )(logits_t.reshape(e_n, nk, LANES))
    return sel.reshape(e_n, s), pos.reshape(e_n, s), aff.reshape(e_n, s)


def _compact_kernel(off_ref, sel_ref, pos_ref, aff_ref, idx_ref, gate_ref, *, nk):
    e = pl.program_id(0)
    idx_ref[...] = jnp.zeros(idx_ref.shape, I32)
    gate_ref[...] = jnp.zeros(gate_ref.shape, F32)
    local = lax.broadcasted_iota(I32, (LANES, LANES), 0)
    lane = lax.broadcasted_iota(I32, (LANES, LANES), 1)

    def body(k, carry):
        off = off_ref[e, k]
        sel = sel_ref[0, pl.ds(k, 1), :]
        pos = pos_ref[0, pl.ds(k, 1), :]
        aff = aff_ref[0, pl.ds(k, 1), :]
        hit = (sel > 0) & ((pos - off) == local)
        tok = jnp.sum(jnp.where(hit, (lane + k * LANES).astype(F32), 0.0), axis=1, keepdims=True)
        gat = jnp.sum(jnp.where(hit, aff, 0.0), axis=1, keepdims=True)
        idx_ref[0, pl.ds(off, LANES), :] = tok.astype(I32)
        gate_ref[0, pl.ds(off, LANES), :] = gat
        return carry

    lax.fori_loop(0, nk, body, 0)


def _compact(sel, pos, aff, cap):
    e_n, s = sel.shape
    nk = s // LANES
    capp = cap + LANES
    offs = pos[:, ::LANES]
    tok = pl.BlockSpec((1, nk, LANES), lambda e, off: (e, 0, 0))
    col = pl.BlockSpec((1, capp, 1), lambda e, off: (e, 0, 0))
    grid_spec = pltpu.PrefetchScalarGridSpec(
        num_scalar_prefetch=1, grid=(e_n,), in_specs=[tok, tok, tok], out_specs=[col, col])
    idx, gate = pl.pallas_call(
        functools.partial(_compact_kernel, nk=nk),
        grid_spec=grid_spec,
        out_shape=[jax.ShapeDtypeStruct((e_n, capp, 1), I32),
                   jax.ShapeDtypeStruct((e_n, capp, 1), F32)],
        compiler_params=_params(("arbitrary",), 32),
        name="expert_compact",
    )(offs, sel.reshape(e_n, nk, LANES), pos.reshape(e_n, nk, LANES), aff.reshape(e_n, nk, LANES))
    return idx[:, :cap, 0], gate[:, :cap, :]


def _moe_up_kernel(idx_hbm, xf_hbm, wg_ref, wu_ref, h_ref, x_scr, stage, idx_s, sem_g, sem_i, *, cap, rows):
    e = pl.program_id(0)
    f = pl.program_id(1)

    @pl.when(f == 0)
    def _gather():
        c_idx = pltpu.make_async_copy(idx_hbm.at[e], idx_s, sem_i.at[0])
        c_idx.start()
        c_idx.wait()

        def row_copy(token, slot, r):
            return pltpu.make_async_copy(xf_hbm.at[pl.ds(token, 1), :],
                                         stage.at[slot, pl.ds(r, 1), :], sem_g.at[slot])

        def issue(j, slot):
            def one(r, c):
                row_copy(idx_s[j * rows + r], slot, r).start()
                return c
            lax.fori_loop(0, rows, one, 0)

        def drain(slot):
            def one(r, c):
                row_copy(0, slot, r).wait()
                return c
            lax.fori_loop(0, rows, one, 0)

        nchunk = cap // rows
        issue(0, 0)
        for j in range(nchunk):
            slot = j % 2
            if j + 1 < nchunk:
                issue(j + 1, 1 - slot)
            drain(slot)
            x_scr[j * rows:(j + 1) * rows, :] = stage[slot].astype(BF16)

    xb = x_scr[...]
    g = jnp.dot(xb, wg_ref[0].astype(BF16), preferred_element_type=F32)
    u = jnp.dot(xb, wu_ref[0].astype(BF16), preferred_element_type=F32)
    h_ref[0] = (g * jax.nn.sigmoid(g) * u).astype(h_ref.dtype)


def _moe_up(idx, xf, w_gate, w_up, cap):
    e_n, d, ff = w_gate.shape
    tf = 256
    rows = 256
    return pl.pallas_call(
        functools.partial(_moe_up_kernel, cap=cap, rows=rows),
        grid=(e_n, ff // tf),
        in_specs=[pl.BlockSpec(memory_space=pl.ANY),
                  pl.BlockSpec(memory_space=pl.ANY),
                  pl.BlockSpec((1, d, tf), lambda e, f: (e, 0, f)),
                  pl.BlockSpec((1, d, tf), lambda e, f: (e, 0, f))],
        out_specs=pl.BlockSpec((1, cap, tf), lambda e, f: (e, 0, f)),
        out_shape=jax.ShapeDtypeStruct((e_n, cap, ff), BF16),
        scratch_shapes=[pltpu.VMEM((cap, d), BF16),
                        pltpu.VMEM((2, rows, d), F32),
                        pltpu.SMEM((cap,), I32),
                        pltpu.SemaphoreType.DMA((2,)),
                        pltpu.SemaphoreType.DMA((1,))],
        compiler_params=_params(("arbitrary", "arbitrary"), 56),
        name="moe_gather_up",
    )(idx, xf, w_gate, w_up)


def _moe_down_kernel(h_ref, wd_ref, gate_ref, y_ref):
    y = jnp.dot(h_ref[0], wd_ref[0].astype(BF16), preferred_element_type=F32)
    y_ref[0] = (y * gate_ref[0]).astype(y_ref.dtype)


def _moe_down(hdn, w_down, gates):
    e_n, cap, ff = hdn.shape
    d = w_down.shape[2]
    tn = 512
    return pl.pallas_call(
        _moe_down_kernel,
        grid=(e_n, d // tn),
        in_specs=[pl.BlockSpec((1, cap, ff), lambda e, n: (e, 0, 0)),
                  pl.BlockSpec((1, ff, tn), lambda e, n: (e, 0, n)),
                  pl.BlockSpec((1, cap, 1), lambda e, n: (e, 0, 0))],
        out_specs=pl.BlockSpec((1, cap, tn), lambda e, n: (e, 0, n)),
        out_shape=jax.ShapeDtypeStruct((e_n, cap, d), BF16),
        compiler_params=_params(("parallel", "arbitrary"), 48),
        name="moe_down",
    )(hdn, w_down, gates)


COMBINE_WINDOW = 64
SLOT_ALIGN = 16


def _combine_kernel(lo_ref, h_ref, sel_ref, pos_ref, gf_ref, gn_ref, y_hbm, o_ref,
                    ybuf, yextra, acc_ref, sem, sem_x, *, cap, tm):
    i = pl.program_id(0)
    win = COMBINE_WINDOW
    e_n = N_EXPERTS

    def window(e, j):
        nominal = (lo_ref[e, i] // SLOT_ALIGN) * SLOT_ALIGN + j * win
        start = pl.multiple_of(jnp.minimum(nominal, cap - win), SLOT_ALIGN)
        return nominal, start

    def first_copy(e):
        _, start = window(e, 0)
        return pltpu.make_async_copy(y_hbm.at[e, pl.ds(start, win), :],
                                     ybuf.at[pl.ds(e * win, win), :], sem.at[e])

    for e in range(e_n):
        first_copy(e).start()

    sel = sel_ref[...]
    pos = pos_ref[...]
    lane = lax.broadcasted_iota(I32, (tm, LANES), 1)
    left = lane < win
    pieces = []
    for e2 in range(e_n // 2):
        ea, eb = 2 * e2, 2 * e2 + 1
        na, sa = window(ea, 0)
        nb, sb = window(eb, 0)
        slot = jnp.where(left, sa + lane, sb + lane - win)
        nominal = jnp.where(left, na, nb)
        pe = jnp.where(left, pos[:, ea:ea + 1], pos[:, eb:eb + 1])
        se = jnp.where(left, sel[:, ea:ea + 1], sel[:, eb:eb + 1])
        pieces.append(((pe == slot) & (se > 0) & (slot >= nominal)).astype(BF16))
    onehot = jnp.concatenate(pieces, axis=1)
    for e in range(e_n):
        first_copy(e).wait()
    acc_ref[...] = jnp.dot(onehot, ybuf[...], preferred_element_type=F32)

    lane_w = lax.broadcasted_iota(I32, (tm, win), 1)
    for e in range(e_n):
        base = (lo_ref[e, i] // SLOT_ALIGN) * SLOT_ALIGN
        nwin = (lo_ref[e, i + 1] - base + win - 1) // win

        def extra(j, c, e=e):
            nominal, start = window(e, j)
            cp = pltpu.make_async_copy(y_hbm.at[e, pl.ds(start, win), :], yextra, sem_x.at[0])
            cp.start()
            cp.wait()
            slot = start + lane_w
            oh = ((pos[:, e:e + 1] == slot) & (sel[:, e:e + 1] > 0) & (slot >= nominal)).astype(BF16)
            acc_ref[...] += jnp.dot(oh, yextra[...], preferred_element_type=F32)
            return c

        lax.fori_loop(1, nwin, extra, 0)

    h2 = h_ref[...] + gf_ref[...] * acc_ref[...]
    ms = jnp.mean(h2 * h2, axis=-1, keepdims=True)
    o_ref[...] = h2 * lax.rsqrt(ms + EPS) * gn_ref[...]


def _combine(h1, y, sel_tm, pos_tm, lo, g_f, gain, cap):
    s, d = h1.shape
    tm = 256
    win = COMBINE_WINDOW
    row = pl.BlockSpec((1, d), lambda i, lo_r: (0, 0))
    tok = pl.BlockSpec((tm, N_EXPERTS), lambda i, lo_r: (i, 0))
    grid_spec = pltpu.PrefetchScalarGridSpec(
        num_scalar_prefetch=1,
        grid=(s // tm,),
        in_specs=[pl.BlockSpec((tm, d), lambda i, lo_r: (i, 0)), tok, tok, row, row,
                  pl.BlockSpec(memory_space=pl.ANY)],
        out_specs=pl.BlockSpec((tm, d), lambda i, lo_r: (i, 0)),
        scratch_shapes=[pltpu.VMEM((N_EXPERTS * win, d), BF16),
                        pltpu.VMEM((win, d), BF16),
                        pltpu.VMEM((tm, d), F32),
                        pltpu.SemaphoreType.DMA((N_EXPERTS,)),
                        pltpu.SemaphoreType.DMA((1,))],
    )
    return pl.pallas_call(
        functools.partial(_combine_kernel, cap=cap, tm=tm),
        grid_spec=grid_spec,
        out_shape=jax.ShapeDtypeStruct((s, d), F32),
        compiler_params=_params(("arbitrary",), 48),
        name="moe_combine_norm",
    )(lo, h1, sel_tm, pos_tm, g_f, gain, y)


def kernel(x, c, positions, w_ada, b_ada, norm_mix_gain, w_in, lam_re_fwd, lam_im_fwd, log_dt_fwd, lam_re_bwd, lam_im_bwd, log_dt_bwd, ssm_b_re, ssm_b_im, ssm_c_re, ssm_c_im, ssm_d, w_glu, b_glu, norm_attn_out_gain, norm_ssm_out_gain, w_out, norm_ffn_gain, w_router, w_exp_gate, w_exp_up, w_exp_down, norm_final_gain):
    batch, s, d = x.shape
    depth = w_ada.shape[0]
    assert batch == 1 and d == D_MODEL and depth == 1
    cap = max(1, CAPACITY_FACTOR * s // N_EXPERTS)
    h = x[0]
    pos = positions[0]
    cos_t, sin1_t, sin2_t = _rope_tables(pos)
    for layer in range(depth):
        mod = _adaln(c, w_ada[layer], b_ada[layer])
        sh_m, sc_m, g_m, sh_f, sc_f, g_f = jnp.split(mod, 6, axis=-1)

        xm = _norm_mod_call(h, norm_mix_gain[layer][None], sc_m, sh_m, BF16)
        w_in_b = w_in[layer].astype(BF16)
        qkv, qkv4, qkv16 = _qkv_proj(xm, w_in_b[:, :3 * ATTN_WIDTH], cos_t, sin1_t, sin2_t)
        outs, lses = [], []
        for dil, arr in zip(DILATIONS, (qkv[None], qkv4, qkv16)):
            o, l = _dilated_attention(arr, dil)
            outs.append(o)
            lses.append(l)
        attn_n = _merge_patterns(outs, lses, norm_attn_out_gain[layer][None])

        u_t = _u_proj(xm, w_in_b[:, 3 * ATTN_WIDTH:])
        w_t, p_all, q_all, a_all = _ssm_operators(
            lam_re_fwd[layer], lam_im_fwd[layer], log_dt_fwd[layer],
            lam_re_bwd[layer], lam_im_bwd[layer], log_dt_bwd[layer],
            ssm_b_re[layer], ssm_b_im[layer], ssm_c_re[layer], ssm_c_im[layer])
        d_t = jnp.tile(ssm_d[layer].reshape(SSM_GROUPS, 1, SSM_GROUP), (1, 1, SSM_CHUNK))
        g_t = _ssm_mixer(u_t, w_t, p_all, q_all, a_all, d_t)
        ssm_n = _glu_norm(g_t, w_glu[layer].astype(BF16), b_glu[layer][None], norm_ssm_out_gain[layer][None])

        h = _out_proj(attn_n, ssm_n, w_out[layer].astype(BF16), h, g_m)

        xf, logits_t = _ffn_prep(h, norm_ffn_gain[layer][None], sc_f, sh_f, jnp.transpose(w_router[layer]))
        sel, slot, aff = _route(logits_t, cap)
        idx, gates = _compact(sel, slot, aff, cap)
        hdn = _moe_up(idx, xf, w_exp_gate[layer], w_exp_up[layer], cap)
        y = _moe_down(hdn, w_exp_down[layer], gates)
        tm = 256
        lo = jnp.concatenate([slot[:, ::tm], jnp.full((N_EXPERTS, 1), cap, I32)], axis=1)
        h = _combine(h, y, jnp.transpose(sel), jnp.transpose(slot), lo, g_f, norm_final_gain[None], cap)
    return h[None]
```

```python
import functools
import math

import jax
import jax.numpy as jnp
from jax import lax
from jax.experimental import pallas as pl
from jax.experimental.pallas import tpu as pltpu

F32 = jnp.float32
BF16 = jnp.bfloat16
I32 = jnp.int32
HIGHEST = lax.Precision.HIGHEST

D_MODEL = 4096
ATTN_WIDTH = 2048
SSM_WIDTH = 2048
HEAD_DIM = 128
N_HEADS = 16
IN_PROJ_WIDTH = 3 * ATTN_WIDTH + SSM_WIDTH
ROPE_DIM = 32
ROPE_THETA = 500000.0
DILATIONS = (1, 4, 16)
ATTN_HALF = 64
NEG_INF = -1e30
SSM_GROUP = 16
SSM_GROUPS = 128
SSM_STATE = 64
SSM_CHUNK = 16
SSM_GB = 8
N_EXPERTS = 16
EXPERT_FF = 2048
CAPACITY_FACTOR = 2
EPS = 1e-6

LANES = 128
MIB = 1024 * 1024


def _params(semantics, vmem_mib):
    return pltpu.CompilerParams(dimension_semantics=semantics, vmem_limit_bytes=vmem_mib * MIB)


def _ada_kernel(c_ref, w_ref, b_ref, o_ref):
    c = c_ref[...]
    ca = c * jax.nn.sigmoid(c)
    o_ref[...] = jnp.dot(ca, w_ref[...], precision=HIGHEST, preferred_element_type=F32) + b_ref[...]


def _adaln(c, w_ada, b_ada):
    d, n = w_ada.shape
    tn = 512
    c8 = jnp.pad(c, ((0, 7), (0, 0)))
    out = pl.pallas_call(
        _ada_kernel,
        grid=(n // tn,),
        in_specs=[pl.BlockSpec((8, d), lambda j: (0, 0)),
                  pl.BlockSpec((d, tn), lambda j: (0, j)),
                  pl.BlockSpec((1, tn), lambda j: (0, j))],
        out_specs=pl.BlockSpec((8, tn), lambda j: (0, j)),
        out_shape=jax.ShapeDtypeStruct((8, n), F32),
        compiler_params=_params(("parallel",), 40),
        name="adaln",
    )(c8, w_ada, b_ada.reshape(1, n))
    return out[0:1]


def _norm_mod(x, gain, scale, shift):
    ms = jnp.mean(x * x, axis=-1, keepdims=True)
    y = x * lax.rsqrt(ms + EPS) * gain
    return y * (1.0 + scale) + shift


def _norm_mod_kernel(x_ref, g_ref, sc_ref, sh_ref, o_ref):
    o_ref[...] = _norm_mod(x_ref[...], g_ref[...], sc_ref[...], sh_ref[...]).astype(o_ref.dtype)


def _norm_mod_call(x, gain, scale, shift, out_dtype):
    s, d = x.shape
    tm = 256
    row = pl.BlockSpec((1, d), lambda i: (0, 0))
    return pl.pallas_call(
        _norm_mod_kernel,
        grid=(s // tm,),
        in_specs=[pl.BlockSpec((tm, d), lambda i: (i, 0)), row, row, row],
        out_specs=pl.BlockSpec((tm, d), lambda i: (i, 0)),
        out_shape=jax.ShapeDtypeStruct((s, d), out_dtype),
        compiler_params=_params(("parallel",), 40),
        name="norm_mod",
    )(x, gain, scale, shift)


def _rope_kernel(pos_ref, c_ref, s1_ref, s2_ref):
    half = ROPE_DIM // 2
    pos = pos_ref[...].astype(F32)
    lane = lax.broadcasted_iota(I32, (1, LANES), 1)
    fidx = (lane & (half - 1)).astype(F32)
    inv_freq = jnp.exp(fidx * (-math.log(ROPE_THETA) / half))
    ang = pos * inv_freq
    cs = jnp.cos(ang)
    sn = jnp.sin(ang)
    c_ref[...] = jnp.where(lane < ROPE_DIM, cs, 1.0)
    s1_ref[...] = jnp.where(lane < half, -sn, 0.0)
    s2_ref[...] = jnp.where((lane >= half) & (lane < ROPE_DIM), sn, 0.0)


def _rope_tables(positions):
    s = positions.shape[0]
    tm = 512
    spec = pl.BlockSpec((tm, LANES), lambda i: (i, 0))
    shp = jax.ShapeDtypeStruct((s, LANES), F32)
    return pl.pallas_call(
        _rope_kernel,
        grid=(s // tm,),
        in_specs=[pl.BlockSpec((tm, 1), lambda i: (i, 0))],
        out_specs=[spec, spec, spec],
        out_shape=[shp, shp, shp],
        compiler_params=_params(("parallel",), 32),
        name="rope_tables",
    )(positions.reshape(s, 1))


PROJ_TM = 1024
PROJ_TN = 512


def _piece_transpose8(xs):
    xs = list(xs)
    lane = lax.broadcasted_iota(I32, xs[0].shape, 1)
    for h in (4, 2, 1):
        low = (lane & (16 * h)) == 0
        for a in range(8):
            if a & h:
                continue
            lo, hi = xs[a], xs[a + h]
            xs[a] = jnp.where(low, lo, pltpu.roll(hi, 16 * h, 1))
            xs[a + h] = jnp.where(low, pltpu.roll(lo, LANES - 16 * h, 1), hi)
    return xs


PROJ_RC = 256


def _qkv_kernel(a_ref, w_ref, c_ref, s1_ref, s2_ref, nat_ref, p4_ref, p16_ref, slab, *, tm, tn):
    j = pl.program_id(1)
    rep = tn // HEAD_DIM
    half = ROPE_DIM // 2
    is_qk = j < 2 * ATTN_WIDTH // tn
    scale = jnp.where(j < ATTN_WIDTH // tn, HEAD_DIM ** -0.5 * math.log2(math.e), 1.0).astype(F32)
    rc = PROJ_RC
    for ch in range(tm // rc):
        rows = slice(ch * rc, (ch + 1) * rc)
        acc = jnp.dot(a_ref[rows, :], w_ref[...], preferred_element_type=F32)
        c = jnp.concatenate([jnp.where(is_qk, c_ref[rows, :], 1.0) * scale] * rep, axis=1)
        s1 = jnp.concatenate([jnp.where(is_qk, s1_ref[rows, :], 0.0) * scale] * rep, axis=1)
        s2 = jnp.concatenate([jnp.where(is_qk, s2_ref[rows, :], 0.0) * scale] * rep, axis=1)
        val = acc * c + pltpu.roll(acc, tn - half, 1) * s1 + pltpu.roll(acc, half, 1) * s2
        nat_ref[rows, :] = val.astype(nat_ref.dtype)
        for cc in range(rep):
            slab[cc, rows, :] = val[:, cc * LANES:(cc + 1) * LANES]
        for d, ref in ((4, p4_ref), (16, p16_ref)):
            nr = rc // d
            for r in range(d):
                for cc in range(rep):
                    ref[r, ch * nr:(ch + 1) * nr, cc * LANES:(cc + 1) * LANES] = (
                        slab[cc, pl.ds(ch * rc + r, nr, stride=d), :].astype(ref.dtype))


def _qkv_proj(xm, w_qkv, cos_t, sin1_t, sin2_t):
    s, d = xm.shape
    n = w_qkv.shape[1]
    tm, tn = PROJ_TM, PROJ_TN
    tab = pl.BlockSpec((tm, LANES), lambda i, j: (i, 0))
    return pl.pallas_call(
        functools.partial(_qkv_kernel, tm=tm, tn=tn),
        grid=(s // tm, n // tn),
        in_specs=[pl.BlockSpec((tm, d), lambda i, j: (i, 0)),
                  pl.BlockSpec((d, tn), lambda i, j: (0, j)),
                  tab, tab, tab],
        out_specs=[pl.BlockSpec((tm, tn), lambda i, j: (i, j)),
                   pl.BlockSpec((4, tm // 4, tn), lambda i, j: (0, i, j)),
                   pl.BlockSpec((16, tm // 16, tn), lambda i, j: (0, i, j))],
        out_shape=[jax.ShapeDtypeStruct((s, n), BF16),
                   jax.ShapeDtypeStruct((4, s // 4, n), BF16),
                   jax.ShapeDtypeStruct((16, s // 16, n), BF16)],
        scratch_shapes=[pltpu.VMEM((tn // LANES, tm, LANES), F32)],
        compiler_params=_params(("parallel", "arbitrary"), 52),
        name="qkv_proj",
    )(xm, w_qkv, cos_t, sin1_t, sin2_t)


def _uproj_kernel(a_ref, w_ref, u_ref, slab, *, tm, tn):
    rc = PROJ_RC
    nchunk = rc // SSM_CHUNK
    for ch in range(tm // rc):
        acc = jnp.dot(a_ref[ch * rc:(ch + 1) * rc, :], w_ref[...], preferred_element_type=F32)
        out_rows = slice(ch * nchunk, (ch + 1) * nchunk)
        for c in range(tn // LANES):
            slab[c, ch * rc:(ch + 1) * rc, :] = acc[:, c * LANES:(c + 1) * LANES]
            rows = [slab[c, pl.ds(ch * rc + i, nchunk, stride=SSM_CHUNK), :]
                    for i in range(SSM_CHUNK)]
            first = _piece_transpose8(rows[:8])
            second = _piece_transpose8(rows[8:])
            for gp in range(8):
                u_ref[c * 8 + gp, out_rows, 0:LANES] = first[gp].astype(u_ref.dtype)
                u_ref[c * 8 + gp, out_rows, LANES:2 * LANES] = second[gp].astype(u_ref.dtype)


def _u_proj(xm, w_u):
    s, d = xm.shape
    n = w_u.shape[1]
    tm, tn = PROJ_TM, PROJ_TN
    gpt = tn // SSM_GROUP
    return pl.pallas_call(
        functools.partial(_uproj_kernel, tm=tm, tn=tn),
        grid=(s // tm, n // tn),
        in_specs=[pl.BlockSpec((tm, d), lambda i, j: (i, 0)),
                  pl.BlockSpec((d, tn), lambda i, j: (0, j))],
        out_specs=pl.BlockSpec((gpt, tm // SSM_CHUNK, SSM_CHUNK * SSM_GROUP), lambda i, j: (j, i, 0)),
        out_shape=jax.ShapeDtypeStruct((n // SSM_GROUP, s // SSM_CHUNK, SSM_CHUNK * SSM_GROUP), BF16),
        scratch_shapes=[pltpu.VMEM((tn // LANES, tm, LANES), F32)],
        compiler_params=_params(("parallel", "arbitrary"), 48),
        name="u_proj",
    )(xm, w_u)


def _attn_kernel(q_ref, k_ref, v_ref, o_ref, l_ref, *, n):
    tq = LANES
    tk = 2 * LANES
    nsb = n // tq
    col_minus_row = lax.broadcasted_iota(I32, (tq, tk), 1) - lax.broadcasted_iota(I32, (tq, tk), 0)
    eye = lax.broadcasted_iota(I32, (tq, tq), 0) == lax.broadcasted_iota(I32, (tq, tq), 1)

    def band_bias(key_start_minus_q0):
        return jnp.where(jnp.abs(col_minus_row + key_start_minus_q0) <= ATTN_HALF, 0.0, NEG_INF).astype(F32)

    def block(sb, q0, ks, bias):
        q = q_ref[pl.ds(q0, tq), :]
        k = k_ref[pl.ds(ks, tk), :]
        v = v_ref[pl.ds(ks, tk), :]
        s = lax.dot_general(q, k, (((1,), (1,)), ((), ())), preferred_element_type=F32) + bias
        m = jnp.max(s, axis=-1, keepdims=True)
        p = jnp.exp2(s - m)
        l = jnp.sum(p, axis=-1, keepdims=True)
        o = jnp.dot(p.astype(BF16), v, preferred_element_type=F32) / l
        o_ref[pl.ds(q0, tq), :] = o.astype(o_ref.dtype)
        lse = (m + jnp.log2(l)) * math.log(2.0)
        l_ref[0, pl.ds(sb, 1), :] = jnp.sum(jnp.where(eye, lse, 0.0), axis=0, keepdims=True)

    block(0, 0, 0, band_bias(0))
    block(nsb - 1, n - tq, n - tk, band_bias(-tq))
    interior = nsb - 2
    if 0 < interior <= 6:
        bias = band_bias(-ATTN_HALF)
        for sb in range(1, nsb - 1):
            block(sb, sb * tq, sb * tq - ATTN_HALF, bias)
    elif interior > 0:
        unroll = max(u for u in (6, 5, 4, 3, 2, 1) if interior % u == 0)
        bias = band_bias(-ATTN_HALF)

        def body(it, carry):
            for u in range(unroll):
                sb = 1 + it * unroll + u
                q0 = pl.multiple_of(sb * tq, tq)
                block(sb, q0, pl.multiple_of(q0 - ATTN_HALF, ATTN_HALF), bias)
            return carry

        lax.fori_loop(0, interior // unroll, body, 0)


def _dilated_attention(qkv, dil):
    _, n, width = qkv.shape
    s = n * dil
    nsb = n // LANES

    def spec(off):
        return pl.BlockSpec((None, n, HEAD_DIM), lambda r, h: (r, 0, off + h))

    o, lse = pl.pallas_call(
        functools.partial(_attn_kernel, n=n),
        grid=(dil, N_HEADS),
        in_specs=[spec(0), spec(N_HEADS), spec(2 * N_HEADS)],
        out_specs=[pl.BlockSpec((None, n, HEAD_DIM), lambda r, h: (r, 0, h)),
                   pl.BlockSpec((1, nsb, LANES), lambda r, h: (r * N_HEADS + h, 0, 0))],
        out_shape=[jax.ShapeDtypeStruct((dil, n, ATTN_WIDTH), BF16),
                   jax.ShapeDtypeStruct((dil * N_HEADS, nsb, LANES), F32)],
        compiler_params=_params(("parallel", "parallel"), 48),
        name=f"dilated_attn_d{dil}",
    )(qkv, qkv, qkv)
    lse = lse.reshape(dil, N_HEADS, n).transpose(2, 0, 1).reshape(s, N_HEADS)
    return o, lse


def _merge_kernel(o1_ref, o4_ref, o16_ref, l1_ref, l2_ref, l3_ref, g_ref, out_ref, acc_ref, s4_ref, s16_ref, *, tm):
    for d, src, dst in ((4, o4_ref, s4_ref), (16, o16_ref, s16_ref)):
        for r in range(d):
            for h in range(N_HEADS):
                dst[h, pl.ds(r, tm // d, stride=d), :] = src[r, :, h * HEAD_DIM:(h + 1) * HEAD_DIM].astype(F32)
    la, lb, lc = l1_ref[...], l2_ref[...], l3_ref[...]
    m = jnp.maximum(jnp.maximum(la, lb), lc)
    ea, eb, ec = jnp.exp(la - m), jnp.exp(lb - m), jnp.exp(lc - m)
    den = ea + eb + ec
    wa, wb, wc = ea / den, eb / den, ec / den
    sq = jnp.zeros((tm, 1), F32)
    for h in range(N_HEADS):
        cs = slice(h * HEAD_DIM, (h + 1) * HEAD_DIM)
        slab = (wa[:, h:h + 1] * o1_ref[:, cs].astype(F32)
                + wb[:, h:h + 1] * s4_ref[h]
                + wc[:, h:h + 1] * s16_ref[h])
        acc_ref[:, cs] = slab
        sq = sq + jnp.sum(slab * slab, axis=-1, keepdims=True)
    inv = lax.rsqrt(sq * (1.0 / ATTN_WIDTH) + EPS)
    out_ref[...] = (acc_ref[...] * inv * g_ref[...]).astype(out_ref.dtype)


def _merge_patterns(outs, lses, gain):
    o1, o4, o16 = outs
    s = o1.shape[1]
    tm = 256
    ospec = pl.BlockSpec((tm, ATTN_WIDTH), lambda i: (i, 0))
    lspec = pl.BlockSpec((tm, N_HEADS), lambda i: (i, 0))
    return pl.pallas_call(
        functools.partial(_merge_kernel, tm=tm),
        grid=(s // tm,),
        in_specs=[pl.BlockSpec((None, tm, ATTN_WIDTH), lambda i: (0, i, 0)),
                  pl.BlockSpec((4, tm // 4, ATTN_WIDTH), lambda i: (0, i, 0)),
                  pl.BlockSpec((16, tm // 16, ATTN_WIDTH), lambda i: (0, i, 0)),
                  lspec, lspec, lspec,
                  pl.BlockSpec((1, ATTN_WIDTH), lambda i: (0, 0))],
        out_specs=ospec,
        out_shape=jax.ShapeDtypeStruct((s, ATTN_WIDTH), BF16),
        scratch_shapes=[pltpu.VMEM((tm, ATTN_WIDTH), F32),
                        pltpu.VMEM((N_HEADS, tm, HEAD_DIM), F32),
                        pltpu.VMEM((N_HEADS, tm, HEAD_DIM), F32)],
        compiler_params=_params(("parallel",), 32),
        name="attn_merge_norm",
    )(o1, o4, o16, *lses, gain)


def _ssm_direction_terms(lam_re, lam_im, log_dt, b_re, b_im, c_re, c_im):
    dt = jnp.exp(log_dt)[:, None]
    mag = jnp.exp(lam_re * dt)
    ang = lam_im * dt
    lb_re = mag * jnp.cos(ang)
    lb_im = mag * jnp.sin(ang)
    den = lam_re * lam_re + lam_im * lam_im
    ar = lb_re - 1.0
    ai = lb_im
    coef_re = (ar * lam_re + ai * lam_im) / den
    coef_im = (ai * lam_re - ar * lam_im) / den
    bb_re = coef_re[..., None] * b_re - coef_im[..., None] * b_im
    bb_im = coef_re[..., None] * b_im + coef_im[..., None] * b_re
    tau = jnp.arange(SSM_CHUNK + 1, dtype=F32)[:, None, None]
    pw_mag = jnp.exp(tau * (lam_re * dt)[None])
    pw_re = pw_mag * jnp.cos(tau * ang[None])
    pw_im = pw_mag * jnp.sin(tau * ang[None])
    e_re = pw_re[..., None] * bb_re[None] - pw_im[..., None] * bb_im[None]
    e_im = pw_re[..., None] * bb_im[None] + pw_im[..., None] * bb_re[None]
    kern = (jnp.einsum('ghp,tgpi->tghi', c_re, e_re[:SSM_CHUNK], precision=HIGHEST)
            - jnp.einsum('ghp,tgpi->tghi', c_im, e_im[:SSM_CHUNK], precision=HIGHEST))
    ca_re = c_re[None] * pw_re[:, :, None, :] - c_im[None] * pw_im[:, :, None, :]
    ca_im = c_re[None] * pw_im[:, :, None, :] + c_im[None] * pw_re[:, :, None, :]
    return kern, e_re, e_im, ca_re, ca_im, pw_re[SSM_CHUNK], pw_im[SSM_CHUNK]


def _ssm_operators(lam_re_f, lam_im_f, log_dt_f, lam_re_b, lam_im_b, log_dt_b, b_re, b_im, c_re, c_im):
    L, G, P = SSM_CHUNK, SSM_GROUPS, SSM_STATE
    kf, ef_re, ef_im, caf_re, caf_im, af_re, af_im = _ssm_direction_terms(
        lam_re_f, lam_im_f, log_dt_f, b_re, b_im, c_re, c_im)
    kb, eb_re, eb_im, cab_re, cab_im, ab_re, ab_im = _ssm_direction_terms(
        lam_re_b, lam_im_b, log_dt_b, b_re, b_im, c_re, c_im)
    i_idx = jnp.arange(L)[:, None]
    j_idx = jnp.arange(L)[None, :]
    lag = j_idx - i_idx
    wf = jnp.where((lag >= 0)[:, :, None, None, None], kf[jnp.clip(lag, 0, L - 1)], 0.0)
    wb = jnp.where((lag <= 0)[:, :, None, None, None], kb[jnp.clip(-lag, 0, L - 1)], 0.0)
    w = jnp.transpose(wf + wb, (2, 0, 4, 1, 3)).reshape(G, L * SSM_GROUP, L * SSM_GROUP)

    def p_mat(e, order):
        return jnp.transpose(e[order], (1, 0, 3, 2)).reshape(G, L * SSM_GROUP, P)

    def q_mat(ca, order):
        return jnp.transpose(ca[order], (1, 3, 0, 2)).reshape(G, P, L * SSM_GROUP)

    f_ord = (L - 1) - jnp.arange(L)
    b_ord = jnp.arange(L)
    parts_p = [p_mat(ef_re, f_ord), p_mat(ef_im, f_ord), p_mat(eb_re, b_ord), p_mat(eb_im, b_ord)]
    parts_q = [q_mat(caf_re, jnp.arange(L) + 1), -q_mat(caf_im, jnp.arange(L) + 1),
               q_mat(cab_re, L - jnp.arange(L)), -q_mat(cab_im, L - jnp.arange(L))]
    even = (jnp.arange(G) % 2 == 0)[:, None, None]

    def pad_p(x):
        z = jnp.zeros_like(x)
        return jnp.where(even, jnp.concatenate([x, z], -1), jnp.concatenate([z, x], -1))

    def pad_q(x):
        z = jnp.zeros_like(x)
        return jnp.where(even, jnp.concatenate([x, z], 1), jnp.concatenate([z, x], 1))

    p_all = jnp.stack([pad_p(x) for x in parts_p], axis=1).astype(BF16)
    q_all = jnp.stack([pad_q(x) for x in parts_q], axis=1).astype(BF16)

    def tiles(re, im):
        re = re.reshape(G // SSM_GB, SSM_GB // 2, 2 * P)
        im = im.reshape(G // SSM_GB, SSM_GB // 2, 2 * P)
        return jnp.concatenate([re, re], axis=1), jnp.concatenate([-im, im], axis=1)

    a_all = jnp.stack([*tiles(af_re, af_im), *tiles(ab_re, ab_im)], axis=1)
    return w.astype(BF16), p_all, q_all, a_all


def _gelu_tanh(x):
    return 0.5 * x * (1.0 + jnp.tanh(math.sqrt(2.0 / math.pi) * (x + 0.044715 * (x * x * x))))


SUBLANES = 8


def _ssm_kernel(u_ref, w_ref, p_ref, q_ref, a_ref, d_ref, y_ref, sf, sb, *, nc, gb):
    npair = gb // 2
    assert 2 * npair == SUBLANES
    for pr in range(npair):
        g0, g1 = 2 * pr, 2 * pr + 1
        for k, (scr, row) in enumerate(((sf, pr), (sf, npair + pr), (sb, pr), (sb, npair + pr))):
            contrib = (jnp.dot(u_ref[g0], p_ref[g0, k], preferred_element_type=F32)
                       + jnp.dot(u_ref[g1], p_ref[g1, k], preferred_element_type=F32))
            scr[pl.ds(row, nc, stride=SUBLANES), :] = contrib

    a1f, a2f, a1b, a2b = a_ref[0, 0], a_ref[0, 1], a_ref[0, 2], a_ref[0, 3]

    def step(c, carry):
        hf, hb = carry
        cf = pl.multiple_of(c * SUBLANES, SUBLANES)
        cb = pl.multiple_of((nc - 1 - c) * SUBLANES, SUBLANES)
        rf = sf[pl.ds(cf, SUBLANES), :]
        rb = sb[pl.ds(cb, SUBLANES), :]
        sf[pl.ds(cf, SUBLANES), :] = hf
        sb[pl.ds(cb, SUBLANES), :] = hb
        return (a1f * hf + a2f * pltpu.roll(hf, npair, 0) + rf,
                a1b * hb + a2b * pltpu.roll(hb, npair, 0) + rb)

    z = jnp.zeros((SUBLANES, LANES), F32)
    lax.fori_loop(0, nc, step, (z, z))

    for pr in range(npair):
        states = [scr[pl.ds(row, nc, stride=SUBLANES), :].astype(BF16)
                  for scr, row in ((sf, pr), (sf, npair + pr), (sb, pr), (sb, npair + pr))]
        for g in (2 * pr, 2 * pr + 1):
            u = u_ref[g]
            y = jnp.dot(u, w_ref[g], preferred_element_type=F32)
            for k in range(4):
                y = y + jnp.dot(states[k], q_ref[g, k], preferred_element_type=F32)
            y = y + d_ref[g] * u.astype(F32)
            y_ref[g] = _gelu_tanh(y).astype(y_ref.dtype)


def _ssm_mixer(u_t, w, p_all, q_all, a_all, d_t):
    g, nc, k = u_t.shape
    gb = SSM_GB
    return pl.pallas_call(
        functools.partial(_ssm_kernel, nc=nc, gb=gb),
        grid=(g // gb,),
        in_specs=[pl.BlockSpec((gb, nc, k), lambda i: (i, 0, 0)),
                  pl.BlockSpec((gb, k, k), lambda i: (i, 0, 0)),
                  pl.BlockSpec((gb, 4, k, LANES), lambda i: (i, 0, 0, 0)),
                  pl.BlockSpec((gb, 4, LANES, k), lambda i: (i, 0, 0, 0)),
                  pl.BlockSpec((1, 4, SUBLANES, LANES), lambda i: (i, 0, 0, 0)),
                  pl.BlockSpec((gb, 1, k), lambda i: (i, 0, 0))],
        out_specs=pl.BlockSpec((gb, nc, k), lambda i: (i, 0, 0)),
        out_shape=jax.ShapeDtypeStruct((g, nc, k), BF16),
        scratch_shapes=[pltpu.VMEM((nc * SUBLANES, LANES), F32)] * 2,
        compiler_params=_params(("parallel",), 48),
        name="ssm_scan",
    )(u_t, w, p_all, q_all, a_all, d_t)


def _glu_kernel(g_ref, w_ref, b_ref, n_ref, o_ref, nat, gb, *, tm):
    nchunk = tm // SSM_CHUNK
    for sl in range(SSM_WIDTH // LANES):
        for half in range(2):
            xs = [g_ref[8 * sl + gp, :, half * LANES:(half + 1) * LANES].astype(F32) for gp in range(8)]
            ys = _piece_transpose8(xs)
            for jj in range(8):
                nat[sl, pl.ds(half * 8 + jj, nchunk, stride=SSM_CHUNK), :] = ys[jj]
        gb[:, sl * LANES:(sl + 1) * LANES] = nat[sl].astype(BF16)
    z = jnp.dot(gb[...], w_ref[...], preferred_element_type=F32) + b_ref[...]
    sq = jnp.zeros((tm, 1), F32)
    for sl in range(SSM_WIDTH // LANES):
        cs = slice(sl * LANES, (sl + 1) * LANES)
        out = nat[sl] * jax.nn.sigmoid(z[:, cs])
        nat[sl] = out
        sq = sq + jnp.sum(out * out, axis=-1, keepdims=True)
    inv = lax.rsqrt(sq * (1.0 / SSM_WIDTH) + EPS)
    for sl in range(SSM_WIDTH // LANES):
        cs = slice(sl * LANES, (sl + 1) * LANES)
        o_ref[:, cs] = (nat[sl] * inv * n_ref[:, cs]).astype(o_ref.dtype)


def _glu_norm(g_t, w_glu, b_glu, gain):
    ng, nc, k = g_t.shape
    s, d = nc * SSM_CHUNK, ng * SSM_GROUP
    tm = 512
    row = pl.BlockSpec((1, d), lambda i: (0, 0))
    return pl.pallas_call(
        functools.partial(_glu_kernel, tm=tm),
        grid=(s // tm,),
        in_specs=[pl.BlockSpec((ng, tm // SSM_CHUNK, k), lambda i: (0, i, 0)),
                  pl.BlockSpec((d, d), lambda i: (0, 0)), row, row],
        out_specs=pl.BlockSpec((tm, d), lambda i: (i, 0)),
        out_shape=jax.ShapeDtypeStruct((s, d), BF16),
        scratch_shapes=[pltpu.VMEM((d // LANES, tm, LANES), F32),
                        pltpu.VMEM((tm, d), BF16)],
        compiler_params=_params(("parallel",), 48),
        name="ssm_glu_norm",
    )(g_t, w_glu, b_glu, gain)


def _outproj_kernel(a1_ref, a2_ref, w1_ref, w2_ref, x_ref, gm_ref, o_ref):
    mix = (jnp.dot(a1_ref[...], w1_ref[...], preferred_element_type=F32)
           + jnp.dot(a2_ref[...], w2_ref[...], preferred_element_type=F32))
    o_ref[...] = x_ref[...] + gm_ref[...] * mix


def _out_proj(attn_n, ssm_n, w_out, x, g_m):
    s, k = attn_n.shape
    n = w_out.shape[1]
    tm, tn = 1024, 512
    return pl.pallas_call(
        _outproj_kernel,
        grid=(s // tm, n // tn),
        in_specs=[pl.BlockSpec((tm, k), lambda i, j: (i, 0)),
                  pl.BlockSpec((tm, k), lambda i, j: (i, 0)),
                  pl.BlockSpec((k, tn), lambda i, j: (0, j)),
                  pl.BlockSpec((k, tn), lambda i, j: (1, j)),
                  pl.BlockSpec((tm, tn), lambda i, j: (i, j)),
                  pl.BlockSpec((1, tn), lambda i, j: (0, j))],
        out_specs=pl.BlockSpec((tm, tn), lambda i, j: (i, j)),
        out_shape=jax.ShapeDtypeStruct((s, n), F32),
        compiler_params=_params(("parallel", "arbitrary"), 48),
        name="out_proj_residual",
    )(attn_n, ssm_n, w_out, w_out, x, g_m)


def _ffn_prep_kernel(h_ref, g_ref, sc_ref, sh_ref, wr_ref, xf_ref, lg_ref):
    xf = _norm_mod(h_ref[...], g_ref[...], sc_ref[...], sh_ref[...])
    xf_ref[...] = xf
    lg_ref[...] = lax.dot_general(wr_ref[...], xf, (((1,), (1,)), ((), ())),
                                  precision=HIGHEST, preferred_element_type=F32)


def _ffn_prep(h1, gain, scale, shift, w_router_t):
    s, d = h1.shape
    tm = 256
    row = pl.BlockSpec((1, d), lambda i: (0, 0))
    return pl.pallas_call(
        _ffn_prep_kernel,
        grid=(s // tm,),
        in_specs=[pl.BlockSpec((tm, d), lambda i: (i, 0)), row, row, row,
                  pl.BlockSpec((N_EXPERTS, d), lambda i: (0, 0))],
        out_specs=[pl.BlockSpec((tm, d), lambda i: (i, 0)),
                   pl.BlockSpec((N_EXPERTS, tm), lambda i: (0, i))],
        out_shape=[jax.ShapeDtypeStruct((s, d), F32),
                   jax.ShapeDtypeStruct((N_EXPERTS, s), F32)],
        compiler_params=_params(("parallel",), 40),
        name="ffn_norm_router",
    )(h1, gain, scale, shift, w_router_t)


def _route_kernel(lg_ref, sel_ref, pos_ref, aff_ref, *, cap, nk):
    e_n = N_EXPERTS
    lg = lg_ref[...]
    mx = jnp.max(lg, axis=0, keepdims=True)
    ex = jnp.exp(lg - mx)
    aff = ex / jnp.sum(ex, axis=0, keepdims=True)
    aff_ref[...] = aff
    bits = pltpu.bitcast(aff, I32)

    def count(mask):
        return jnp.sum(jnp.sum(mask.astype(F32), axis=2, keepdims=True), axis=1, keepdims=True)

    def radix(i, thr):
        cand = thr | jnp.left_shift(jnp.int32(1), 30 - i)
        return jnp.where(count(bits >= cand) >= cap, cand, thr)

    thr = lax.fori_loop(0, 31, radix, jnp.zeros((e_n, 1, 1), I32))
    gt = bits > thr
    eq = bits == thr
    need = cap - count(gt)

    li = lax.broadcasted_iota(I32, (LANES, LANES), 0)
    lj = lax.broadcasted_iota(I32, (LANES, LANES), 1)
    tri_lane = (li < lj).astype(BF16)
    ki = lax.broadcasted_iota(I32, (nk, nk), 0)
    kj = lax.broadcasted_iota(I32, (nk, nk), 1)
    tri_row = (kj < ki).astype(BF16)

    def excl_prefix(mask):
        mf = mask.astype(F32)
        within = jnp.dot(mf.reshape(e_n * nk, LANES).astype(BF16), tri_lane,
                         preferred_element_type=F32).reshape(e_n, nk, LANES)
        rowtot = jnp.broadcast_to(jnp.sum(mf, axis=2, keepdims=True), (e_n, nk, LANES)).astype(BF16)
        across = jnp.stack([jnp.dot(tri_row, rowtot[e], preferred_element_type=F32) for e in range(e_n)])
        return within + across

    sel = gt | (eq & (excl_prefix(eq) < need))
    sel_ref[...] = sel.astype(I32)
    pos_ref[...] = excl_prefix(sel).astype(I32)


def _route(logits_t, cap):
    e_n, s = logits_t.shape
    nk = s // LANES
    spec = pl.BlockSpec((e_n, nk, LANES), lambda i: (0, 0, 0))
    sel, pos, aff = pl.pallas_call(
        functools.partial(_route_kernel, cap=cap, nk=nk),
        grid=(1,),
        in_specs=[spec],
        out_specs=[spec, spec, spec],
        out_shape=[jax.ShapeDtypeStruct((e_n, nk, LANES), I32),
                   jax.ShapeDtypeStruct((e_n, nk, LANES), I32),
                   jax.ShapeDtypeStruct((e_n, nk, LANES), F32)],
        compiler_params=_params(("arbitrary",), 48),
        name="expert_choice_route",
    )(logits_t.reshape(e_n, nk, LANES))
    return sel.reshape(e_n, s), pos.reshape(e_n, s), aff.reshape(e_n, s)


def _compact_kernel(off_ref, sel_ref, pos_ref, aff_ref, idx_ref, gate_ref, *, nk):
    e = pl.program_id(0)
    idx_ref[...] = jnp.zeros(idx_ref.shape, I32)
    gate_ref[...] = jnp.zeros(gate_ref.shape, F32)
    local = lax.broadcasted_iota(I32, (LANES, LANES), 0)
    lane = lax.broadcasted_iota(I32, (LANES, LANES), 1)

    unroll = 4 if nk % 4 == 0 else 1

    def body(it, carry):
        for u in range(unroll):
            k = it * unroll + u
            off = off_ref[e, k]
            sel = sel_ref[0, pl.ds(k, 1), :]
            pos = pos_ref[0, pl.ds(k, 1), :]
            aff = aff_ref[0, pl.ds(k, 1), :]
            hit = (sel > 0) & ((pos - off) == local)
            tok = jnp.sum(jnp.where(hit, (lane + k * LANES).astype(F32), 0.0), axis=1, keepdims=True)
            gat = jnp.sum(jnp.where(hit, aff, 0.0), axis=1, keepdims=True)
            idx_ref[0, pl.ds(off, LANES), :] = tok.astype(I32)
            gate_ref[0, pl.ds(off, LANES), :] = gat
        return carry

    lax.fori_loop(0, nk // unroll, body, 0)


def _compact(sel, pos, aff, cap):
    e_n, s = sel.shape
    nk = s // LANES
    capp = cap + LANES
    offs = pos[:, ::LANES]
    tok = pl.BlockSpec((1, nk, LANES), lambda e, off: (e, 0, 0))
    col = pl.BlockSpec((1, capp, 1), lambda e, off: (e, 0, 0))
    grid_spec = pltpu.PrefetchScalarGridSpec(
        num_scalar_prefetch=1, grid=(e_n,), in_specs=[tok, tok, tok], out_specs=[col, col])
    idx, gate = pl.pallas_call(
        functools.partial(_compact_kernel, nk=nk),
        grid_spec=grid_spec,
        out_shape=[jax.ShapeDtypeStruct((e_n, capp, 1), I32),
                   jax.ShapeDtypeStruct((e_n, capp, 1), F32)],
        compiler_params=_params(("arbitrary",), 32),
        name="expert_compact",
    )(offs, sel.reshape(e_n, nk, LANES), pos.reshape(e_n, nk, LANES), aff.reshape(e_n, nk, LANES))
    return idx[:, :cap, 0], gate[:, :cap, :]


def _moe_up_kernel(idx_hbm, xf_hbm, wg_ref, wu_ref, h_ref, x_scr, stage, idx_s, sem_g, sem_i, *, cap, rows):
    e = pl.program_id(0)
    f = pl.program_id(1)

    @pl.when(f == 0)
    def _gather():
        c_idx = pltpu.make_async_copy(idx_hbm.at[e], idx_s, sem_i.at[0])
        c_idx.start()
        c_idx.wait()

        def row_copy(token, slot, r):
            return pltpu.make_async_copy(xf_hbm.at[pl.ds(token, 1), :],
                                         stage.at[slot, pl.ds(r, 1), :], sem_g.at[slot])

        def issue(j, slot):
            def one(r, c):
                row_copy(idx_s[j * rows + r], slot, r).start()
                return c
            lax.fori_loop(0, rows, one, 0)

        def drain(slot):
            def one(r, c):
                row_copy(0, slot, r).wait()
                return c
            lax.fori_loop(0, rows, one, 0)

        nchunk = cap // rows
        issue(0, 0)
        for j in range(nchunk):
            slot = j % 2
            if j + 1 < nchunk:
                issue(j + 1, 1 - slot)
            drain(slot)
            x_scr[j * rows:(j + 1) * rows, :] = stage[slot].astype(BF16)

    xb = x_scr[...]
    g = jnp.dot(xb, wg_ref[0].astype(BF16), preferred_element_type=F32)
    u = jnp.dot(xb, wu_ref[0].astype(BF16), preferred_element_type=F32)
    h_ref[0] = (g * jax.nn.sigmoid(g) * u).astype(h_ref.dtype)


def _moe_up(idx, xf, w_gate, w_up, cap):
    e_n, d, ff = w_gate.shape
    tf = 256
    rows = 256
    return pl.pallas_call(
        functools.partial(_moe_up_kernel, cap=cap, rows=rows),
        grid=(e_n, ff // tf),
        in_specs=[pl.BlockSpec(memory_space=pl.ANY),
                  pl.BlockSpec(memory_space=pl.ANY),
                  pl.BlockSpec((1, d, tf), lambda e, f: (e, 0, f)),
                  pl.BlockSpec((1, d, tf), lambda e, f: (e, 0, f))],
        out_specs=pl.BlockSpec((1, cap, tf), lambda e, f: (e, 0, f)),
        out_shape=jax.ShapeDtypeStruct((e_n, cap, ff), BF16),
        scratch_shapes=[pltpu.VMEM((cap, d), BF16),
                        pltpu.VMEM((2, rows, d), F32),
                        pltpu.SMEM((cap,), I32),
                        pltpu.SemaphoreType.DMA((2,)),
                        pltpu.SemaphoreType.DMA((1,))],
        compiler_params=_params(("arbitrary", "arbitrary"), 56),
        name="moe_gather_up",
    )(idx, xf, w_gate, w_up)


def _moe_down_kernel(h_ref, wd_ref, gate_ref, y_ref):
    y = jnp.dot(h_ref[0], wd_ref[0].astype(BF16), preferred_element_type=F32)
    y_ref[0] = (y * gate_ref[0]).astype(y_ref.dtype)


def _moe_down(hdn, w_down, gates):
    e_n, cap, ff = hdn.shape
    d = w_down.shape[2]
    tn = 512
    return pl.pallas_call(
        _moe_down_kernel,
        grid=(e_n, d // tn),
        in_specs=[pl.BlockSpec((1, cap, ff), lambda e, n: (e, 0, 0)),
                  pl.BlockSpec((1, ff, tn), lambda e, n: (e, 0, n)),
                  pl.BlockSpec((1, cap, 1), lambda e, n: (e, 0, 0))],
        out_specs=pl.BlockSpec((1, cap, tn), lambda e, n: (e, 0, n)),
        out_shape=jax.ShapeDtypeStruct((e_n, cap, d), BF16),
        compiler_params=_params(("parallel", "arbitrary"), 48),
        name="moe_down",
    )(hdn, w_down, gates)


COMBINE_WINDOW = 64
SLOT_ALIGN = 16


def _combine_kernel(lo_ref, h_ref, sel_ref, pos_ref, gf_ref, gn_ref, y_hbm, o_ref,
                    ybuf, yextra, acc_ref, sem, sem_x, *, cap, tm):
    i = pl.program_id(0)
    win = COMBINE_WINDOW
    e_n = N_EXPERTS

    n_steps = pl.num_programs(0)
    cur = i % 2

    def window(e, j, step=None):
        step = i if step is None else step
        nominal = (lo_ref[e, step] // SLOT_ALIGN) * SLOT_ALIGN + j * win
        start = pl.multiple_of(jnp.minimum(nominal, cap - win), SLOT_ALIGN)
        return nominal, start

    def first_copy(e, step, buf):
        _, start = window(e, 0, step)
        return pltpu.make_async_copy(y_hbm.at[e, pl.ds(start, win), :],
                                     ybuf.at[buf, pl.ds(e * win, win), :], sem.at[buf, e])

    @pl.when(i == 0)
    def _():
        for e in range(e_n):
            first_copy(e, 0, 0).start()

    @pl.when(i + 1 < n_steps)
    def _():
        for e in range(e_n):
            first_copy(e, i + 1, 1 - cur).start()

    sel = sel_ref[...]
    pos = pos_ref[...]
    lane = lax.broadcasted_iota(I32, (tm, LANES), 1)
    left = lane < win
    pieces = []
    for e2 in range(e_n // 2):
        ea, eb = 2 * e2, 2 * e2 + 1
        na, sa = window(ea, 0)
        nb, sb = window(eb, 0)
        slot = jnp.where(left, sa + lane, sb + lane - win)
        nominal = jnp.where(left, na, nb)
        pe = jnp.where(left, pos[:, ea:ea + 1], pos[:, eb:eb + 1])
        se = jnp.where(left, sel[:, ea:ea + 1], sel[:, eb:eb + 1])
        pieces.append(((pe == slot) & (se > 0) & (slot >= nominal)).astype(BF16))
    onehot = jnp.concatenate(pieces, axis=1)
    for e in range(e_n):
        first_copy(e, i, cur).wait()
    acc_ref[...] = jnp.dot(onehot, ybuf[cur], preferred_element_type=F32)

    lane_w = lax.broadcasted_iota(I32, (tm, win), 1)
    for e in range(e_n):
        base = (lo_ref[e, i] // SLOT_ALIGN) * SLOT_ALIGN
        nwin = (lo_ref[e, i + 1] - base + win - 1) // win

        def extra(j, c, e=e):
            nominal, start = window(e, j)
            cp = pltpu.make_async_copy(y_hbm.at[e, pl.ds(start, win), :], yextra, sem_x.at[0])
            cp.start()
            cp.wait()
            slot = start + lane_w
            oh = ((pos[:, e:e + 1] == slot) & (sel[:, e:e + 1] > 0) & (slot >= nominal)).astype(BF16)
            acc_ref[...] += jnp.dot(oh, yextra[...], preferred_element_type=F32)
            return c

        lax.fori_loop(1, nwin, extra, 0)

    h2 = h_ref[...] + gf_ref[...] * acc_ref[...]
    ms = jnp.mean(h2 * h2, axis=-1, keepdims=True)
    o_ref[...] = h2 * lax.rsqrt(ms + EPS) * gn_ref[...]


def _combine(h1, y, sel_tm, pos_tm, lo, g_f, gain, cap):
    s, d = h1.shape
    tm = 256
    win = COMBINE_WINDOW
    row = pl.BlockSpec((1, d), lambda i, lo_r: (0, 0))
    tok = pl.BlockSpec((tm, N_EXPERTS), lambda i, lo_r: (i, 0))
    grid_spec = pltpu.PrefetchScalarGridSpec(
        num_scalar_prefetch=1,
        grid=(s // tm,),
        in_specs=[pl.BlockSpec((tm, d), lambda i, lo_r: (i, 0)), tok, tok, row, row,
                  pl.BlockSpec(memory_space=pl.ANY)],
        out_specs=pl.BlockSpec((tm, d), lambda i, lo_r: (i, 0)),
        scratch_shapes=[pltpu.VMEM((2, N_EXPERTS * win, d), BF16),
                        pltpu.VMEM((win, d), BF16),
                        pltpu.VMEM((tm, d), F32),
                        pltpu.SemaphoreType.DMA((2, N_EXPERTS)),
                        pltpu.SemaphoreType.DMA((1,))],
    )
    return pl.pallas_call(
        functools.partial(_combine_kernel, cap=cap, tm=tm),
        grid_spec=grid_spec,
        out_shape=jax.ShapeDtypeStruct((s, d), F32),
        compiler_params=_params(("arbitrary",), 56),
        name="moe_combine_norm",
    )(lo, h1, sel_tm, pos_tm, g_f, gain, y)


def kernel(x, c, positions, w_ada, b_ada, norm_mix_gain, w_in, lam_re_fwd, lam_im_fwd, log_dt_fwd, lam_re_bwd, lam_im_bwd, log_dt_bwd, ssm_b_re, ssm_b_im, ssm_c_re, ssm_c_im, ssm_d, w_glu, b_glu, norm_attn_out_gain, norm_ssm_out_gain, w_out, norm_ffn_gain, w_router, w_exp_gate, w_exp_up, w_exp_down, norm_final_gain):
    batch, s, d = x.shape
    depth = w_ada.shape[0]
    assert batch == 1 and d == D_MODEL and depth == 1
    cap = max(1, CAPACITY_FACTOR * s // N_EXPERTS)
    h = x[0]
    pos = positions[0]
    cos_t, sin1_t, sin2_t = _rope_tables(pos)
    for layer in range(depth):
        mod = _adaln(c, w_ada[layer], b_ada[layer])
        sh_m, sc_m, g_m, sh_f, sc_f, g_f = jnp.split(mod, 6, axis=-1)

        xm = _norm_mod_call(h, norm_mix_gain[layer][None], sc_m, sh_m, BF16)
        w_in_b = w_in[layer].astype(BF16)
        qkv, qkv4, qkv16 = _qkv_proj(xm, w_in_b[:, :3 * ATTN_WIDTH], cos_t, sin1_t, sin2_t)
        outs, lses = [], []
        for dil, arr in zip(DILATIONS, (qkv[None], qkv4, qkv16)):
            o, l = _dilated_attention(arr, dil)
            outs.append(o)
            lses.append(l)
        attn_n = _merge_patterns(outs, lses, norm_attn_out_gain[layer][None])

        u_t = _u_proj(xm, w_in_b[:, 3 * ATTN_WIDTH:])
        w_t, p_all, q_all, a_all = _ssm_operators(
            lam_re_fwd[layer], lam_im_fwd[layer], log_dt_fwd[layer],
            lam_re_bwd[layer], lam_im_bwd[layer], log_dt_bwd[layer],
            ssm_b_re[layer], ssm_b_im[layer], ssm_c_re[layer], ssm_c_im[layer])
        d_t = jnp.tile(ssm_d[layer].reshape(SSM_GROUPS, 1, SSM_GROUP), (1, 1, SSM_CHUNK))
        g_t = _ssm_mixer(u_t, w_t, p_all, q_all, a_all, d_t)
        ssm_n = _glu_norm(g_t, w_glu[layer].astype(BF16), b_glu[layer][None], norm_ssm_out_gain[layer][None])

        h = _out_proj(attn_n, ssm_n, w_out[layer].astype(BF16), h, g_m)

        xf, logits_t = _ffn_prep(h, norm_ffn_gain[layer][None], sc_f, sh_f, jnp.transpose(w_router[layer]))
        sel, slot, aff = _route(logits_t, cap)
        idx, gates = _compact(sel, slot, aff, cap)
        hdn = _moe_up(idx, xf, w_exp_gate[layer], w_exp_up[layer], cap)
        y = _moe_down(hdn, w_exp_down[layer], gates)
        tm = 256
        lo = jnp.concatenate([slot[:, ::tm], jnp.full((N_EXPERTS, 1), cap, I32)], axis=1)
        h = _combine(h, y, jnp.transpose(sel), jnp.transpose(slot), lo, g_f, norm_final_gain[None], cap)
    return h[None]
```

```python
import functools
import math

import jax
import jax.numpy as jnp
from jax import lax
from jax.experimental import pallas as pl
from jax.experimental.pallas import tpu as pltpu

F32 = jnp.float32
BF16 = jnp.bfloat16
I32 = jnp.int32
HIGHEST = lax.Precision.HIGHEST

D_MODEL = 4096
ATTN_WIDTH = 2048
SSM_WIDTH = 2048
HEAD_DIM = 128
N_HEADS = 16
IN_PROJ_WIDTH = 3 * ATTN_WIDTH + SSM_WIDTH
ROPE_DIM = 32
ROPE_THETA = 500000.0
DILATIONS = (1, 4, 16)
ATTN_HALF = 64
NEG_INF = -1e30
SSM_GROUP = 16
SSM_GROUPS = 128
SSM_STATE = 64
SSM_CHUNK = 16
SSM_GB = 8
N_EXPERTS = 16
EXPERT_FF = 2048
CAPACITY_FACTOR = 2
EPS = 1e-6

LANES = 128
MIB = 1024 * 1024


def _params(semantics, vmem_mib):
    return pltpu.CompilerParams(dimension_semantics=semantics, vmem_limit_bytes=vmem_mib * MIB)


def _ada_kernel(c_ref, w_ref, b_ref, o_ref):
    c = c_ref[...]
    ca = c * jax.nn.sigmoid(c)
    o_ref[...] = jnp.dot(ca, w_ref[...], precision=HIGHEST, preferred_element_type=F32) + b_ref[...]


def _adaln(c, w_ada, b_ada):
    d, n = w_ada.shape
    tn = 512
    c8 = jnp.pad(c, ((0, 7), (0, 0)))
    out = pl.pallas_call(
        _ada_kernel,
        grid=(n // tn,),
        in_specs=[pl.BlockSpec((8, d), lambda j: (0, 0)),
                  pl.BlockSpec((d, tn), lambda j: (0, j)),
                  pl.BlockSpec((1, tn), lambda j: (0, j))],
        out_specs=pl.BlockSpec((8, tn), lambda j: (0, j)),
        out_shape=jax.ShapeDtypeStruct((8, n), F32),
        compiler_params=_params(("parallel",), 40),
        name="adaln",
    )(c8, w_ada, b_ada.reshape(1, n))
    return out[0:1]


def _norm_mod(x, gain, scale, shift):
    ms = jnp.mean(x * x, axis=-1, keepdims=True)
    y = x * lax.rsqrt(ms + EPS) * gain
    return y * (1.0 + scale) + shift


def _norm_mod_kernel(x_ref, g_ref, sc_ref, sh_ref, o_ref):
    o_ref[...] = _norm_mod(x_ref[...], g_ref[...], sc_ref[...], sh_ref[...]).astype(o_ref.dtype)


def _norm_mod_call(x, gain, scale, shift, out_dtype):
    s, d = x.shape
    tm = 256
    row = pl.BlockSpec((1, d), lambda i: (0, 0))
    return pl.pallas_call(
        _norm_mod_kernel,
        grid=(s // tm,),
        in_specs=[pl.BlockSpec((tm, d), lambda i: (i, 0)), row, row, row],
        out_specs=pl.BlockSpec((tm, d), lambda i: (i, 0)),
        out_shape=jax.ShapeDtypeStruct((s, d), out_dtype),
        compiler_params=_params(("parallel",), 40),
        name="norm_mod",
    )(x, gain, scale, shift)


def _rope_kernel(pos_ref, c_ref, s1_ref, s2_ref):
    half = ROPE_DIM // 2
    pos = pos_ref[...].astype(F32)
    lane = lax.broadcasted_iota(I32, (1, LANES), 1)
    fidx = (lane & (half - 1)).astype(F32)
    inv_freq = jnp.exp(fidx * (-math.log(ROPE_THETA) / half))
    ang = pos * inv_freq
    cs = jnp.cos(ang)
    sn = jnp.sin(ang)
    c_ref[...] = jnp.where(lane < ROPE_DIM, cs, 1.0)
    s1_ref[...] = jnp.where(lane < half, -sn, 0.0)
    s2_ref[...] = jnp.where((lane >= half) & (lane < ROPE_DIM), sn, 0.0)


def _rope_tables(positions):
    s = positions.shape[0]
    tm = 512
    spec = pl.BlockSpec((tm, LANES), lambda i: (i, 0))
    shp = jax.ShapeDtypeStruct((s, LANES), F32)
    return pl.pallas_call(
        _rope_kernel,
        grid=(s // tm,),
        in_specs=[pl.BlockSpec((tm, 1), lambda i: (i, 0))],
        out_specs=[spec, spec, spec],
        out_shape=[shp, shp, shp],
        compiler_params=_params(("parallel",), 32),
        name="rope_tables",
    )(positions.reshape(s, 1))


PROJ_TM = 1024
PROJ_TN = 512


def _piece_transpose8(xs):
    xs = list(xs)
    lane = lax.broadcasted_iota(I32, xs[0].shape, 1)
    for h in (4, 2, 1):
        low = (lane & (16 * h)) == 0
        for a in range(8):
            if a & h:
                continue
            lo, hi = xs[a], xs[a + h]
            xs[a] = jnp.where(low, lo, pltpu.roll(hi, 16 * h, 1))
            xs[a + h] = jnp.where(low, pltpu.roll(lo, LANES - 16 * h, 1), hi)
    return xs


PROJ_RC = 256


def _qkv_kernel(a_ref, w_ref, c_ref, s1_ref, s2_ref, nat_ref, p4_ref, p16_ref, slab, *, tm, tn):
    j = pl.program_id(1)
    rep = tn // HEAD_DIM
    half = ROPE_DIM // 2
    is_qk = j < 2 * ATTN_WIDTH // tn
    scale = jnp.where(j < ATTN_WIDTH // tn, HEAD_DIM ** -0.5 * math.log2(math.e), 1.0).astype(F32)
    rc = PROJ_RC
    for ch in range(tm // rc):
        rows = slice(ch * rc, (ch + 1) * rc)
        acc = jnp.dot(a_ref[rows, :], w_ref[...], preferred_element_type=F32)
        c = jnp.concatenate([jnp.where(is_qk, c_ref[rows, :], 1.0) * scale] * rep, axis=1)
        s1 = jnp.concatenate([jnp.where(is_qk, s1_ref[rows, :], 0.0) * scale] * rep, axis=1)
        s2 = jnp.concatenate([jnp.where(is_qk, s2_ref[rows, :], 0.0) * scale] * rep, axis=1)
        val = acc * c + pltpu.roll(acc, tn - half, 1) * s1 + pltpu.roll(acc, half, 1) * s2
        for cc in range(rep):
            piece = val[:, cc * LANES:(cc + 1) * LANES]
            nat_ref[cc, rows, :] = piece.astype(nat_ref.dtype)
            slab[cc, rows, :] = piece
        for d, ref in ((4, p4_ref), (16, p16_ref)):
            nr = rc // d
            for r in range(d):
                for cc in range(rep):
                    ref[r, cc, ch * nr:(ch + 1) * nr, :] = (
                        slab[cc, pl.ds(ch * rc + r, nr, stride=d), :].astype(ref.dtype))


def _qkv_proj(xm, w_qkv, cos_t, sin1_t, sin2_t):
    s, d = xm.shape
    n = w_qkv.shape[1]
    tm, tn = PROJ_TM, PROJ_TN
    nh, hpt = n // HEAD_DIM, tn // HEAD_DIM
    tab = pl.BlockSpec((tm, LANES), lambda i, j: (i, 0))
    return pl.pallas_call(
        functools.partial(_qkv_kernel, tm=tm, tn=tn),
        grid=(s // tm, n // tn),
        in_specs=[pl.BlockSpec((tm, d), lambda i, j: (i, 0)),
                  pl.BlockSpec((d, tn), lambda i, j: (0, j)),
                  tab, tab, tab],
        out_specs=[pl.BlockSpec((hpt, tm, HEAD_DIM), lambda i, j: (j, i, 0)),
                   pl.BlockSpec((4, hpt, tm // 4, HEAD_DIM), lambda i, j: (0, j, i, 0)),
                   pl.BlockSpec((16, hpt, tm // 16, HEAD_DIM), lambda i, j: (0, j, i, 0))],
        out_shape=[jax.ShapeDtypeStruct((nh, s, HEAD_DIM), BF16),
                   jax.ShapeDtypeStruct((4, nh, s // 4, HEAD_DIM), BF16),
                   jax.ShapeDtypeStruct((16, nh, s // 16, HEAD_DIM), BF16)],
        scratch_shapes=[pltpu.VMEM((tn // LANES, tm, LANES), F32)],
        compiler_params=_params(("parallel", "arbitrary"), 52),
        name="qkv_proj",
    )(xm, w_qkv, cos_t, sin1_t, sin2_t)


def _uproj_kernel(a_ref, w_ref, u_ref, slab, *, tm, tn):
    rc = PROJ_RC
    nchunk = rc // SSM_CHUNK
    for ch in range(tm // rc):
        acc = jnp.dot(a_ref[ch * rc:(ch + 1) * rc, :], w_ref[...], preferred_element_type=F32)
        out_rows = slice(ch * nchunk, (ch + 1) * nchunk)
        for c in range(tn // LANES):
            slab[c, ch * rc:(ch + 1) * rc, :] = acc[:, c * LANES:(c + 1) * LANES]
            rows = [slab[c, pl.ds(ch * rc + i, nchunk, stride=SSM_CHUNK), :]
                    for i in range(SSM_CHUNK)]
            first = _piece_transpose8(rows[:8])
            second = _piece_transpose8(rows[8:])
            for gp in range(8):
                u_ref[c * 8 + gp, out_rows, 0:LANES] = first[gp].astype(u_ref.dtype)
                u_ref[c * 8 + gp, out_rows, LANES:2 * LANES] = second[gp].astype(u_ref.dtype)


def _u_proj(xm, w_u):
    s, d = xm.shape
    n = w_u.shape[1]
    tm, tn = PROJ_TM, PROJ_TN
    gpt = tn // SSM_GROUP
    return pl.pallas_call(
        functools.partial(_uproj_kernel, tm=tm, tn=tn),
        grid=(s // tm, n // tn),
        in_specs=[pl.BlockSpec((tm, d), lambda i, j: (i, 0)),
                  pl.BlockSpec((d, tn), lambda i, j: (0, j))],
        out_specs=pl.BlockSpec((gpt, tm // SSM_CHUNK, SSM_CHUNK * SSM_GROUP), lambda i, j: (j, i, 0)),
        out_shape=jax.ShapeDtypeStruct((n // SSM_GROUP, s // SSM_CHUNK, SSM_CHUNK * SSM_GROUP), BF16),
        scratch_shapes=[pltpu.VMEM((tn // LANES, tm, LANES), F32)],
        compiler_params=_params(("parallel", "arbitrary"), 48),
        name="u_proj",
    )(xm, w_u)


def _attn_kernel(q_ref, k_ref, v_ref, o_ref, l_ref, *, n):
    tq = LANES
    tk = 2 * LANES
    nsb = n // tq
    col_minus_row = lax.broadcasted_iota(I32, (tq, tk), 1) - lax.broadcasted_iota(I32, (tq, tk), 0)
    eye = lax.broadcasted_iota(I32, (tq, tq), 0) == lax.broadcasted_iota(I32, (tq, tq), 1)

    def band_bias(key_start_minus_q0):
        return jnp.where(jnp.abs(col_minus_row + key_start_minus_q0) <= ATTN_HALF, 0.0, NEG_INF).astype(F32)

    def block(sb, q0, ks, bias):
        q = q_ref[pl.ds(q0, tq), :]
        k = k_ref[pl.ds(ks, tk), :]
        v = v_ref[pl.ds(ks, tk), :]
        s = lax.dot_general(q, k, (((1,), (1,)), ((), ())), preferred_element_type=F32) + bias
        m = jnp.max(s, axis=-1, keepdims=True)
        p = jnp.exp2(s - m)
        l = jnp.sum(p, axis=-1, keepdims=True)
        o = jnp.dot(p.astype(BF16), v, preferred_element_type=F32) / l
        o_ref[pl.ds(q0, tq), :] = o.astype(o_ref.dtype)
        lse = (m + jnp.log2(l)) * math.log(2.0)
        l_ref[0, pl.ds(sb, 1), :] = jnp.sum(jnp.where(eye, lse, 0.0), axis=0, keepdims=True)

    block(0, 0, 0, band_bias(0))
    block(nsb - 1, n - tq, n - tk, band_bias(-tq))
    interior = nsb - 2
    if 0 < interior <= 6:
        bias = band_bias(-ATTN_HALF)
        for sb in range(1, nsb - 1):
            block(sb, sb * tq, sb * tq - ATTN_HALF, bias)
    elif interior > 0:
        unroll = max(u for u in (6, 5, 4, 3, 2, 1) if interior % u == 0)
        bias = band_bias(-ATTN_HALF)

        def body(it, carry):
            for u in range(unroll):
                sb = 1 + it * unroll + u
                q0 = pl.multiple_of(sb * tq, tq)
                block(sb, q0, pl.multiple_of(q0 - ATTN_HALF, ATTN_HALF), bias)
            return carry

        lax.fori_loop(0, interior // unroll, body, 0)


def _dilated_attention(qkv, dil):
    _, _, n, _ = qkv.shape
    s = n * dil
    nsb = n // LANES

    def spec(off):
        return pl.BlockSpec((None, None, n, HEAD_DIM), lambda r, h: (r, off + h, 0, 0))

    o, lse = pl.pallas_call(
        functools.partial(_attn_kernel, n=n),
        grid=(dil, N_HEADS),
        in_specs=[spec(0), spec(N_HEADS), spec(2 * N_HEADS)],
        out_specs=[pl.BlockSpec((None, None, n, HEAD_DIM), lambda r, h: (r, h, 0, 0)),
                   pl.BlockSpec((1, nsb, LANES), lambda r, h: (r * N_HEADS + h, 0, 0))],
        out_shape=[jax.ShapeDtypeStruct((dil, N_HEADS, n, HEAD_DIM), BF16),
                   jax.ShapeDtypeStruct((dil * N_HEADS, nsb, LANES), F32)],
        compiler_params=_params(("parallel", "parallel"), 48),
        name=f"dilated_attn_d{dil}",
    )(qkv, qkv, qkv)
    lse = lse.reshape(dil, N_HEADS, n).transpose(2, 0, 1).reshape(s, N_HEADS)
    return o, lse


def _merge_kernel(o1_ref, o4_ref, o16_ref, l1_ref, l2_ref, l3_ref, g_ref, out_ref, acc_ref, s4_ref, s16_ref, *, tm):
    for d, src, dst in ((4, o4_ref, s4_ref), (16, o16_ref, s16_ref)):
        for r in range(d):
            for h in range(N_HEADS):
                dst[h, pl.ds(r, tm // d, stride=d), :] = src[r, h].astype(F32)
    la, lb, lc = l1_ref[...], l2_ref[...], l3_ref[...]
    m = jnp.maximum(jnp.maximum(la, lb), lc)
    ea, eb, ec = jnp.exp(la - m), jnp.exp(lb - m), jnp.exp(lc - m)
    den = ea + eb + ec
    wa, wb, wc = ea / den, eb / den, ec / den
    sq = jnp.zeros((tm, 1), F32)
    for h in range(N_HEADS):
        cs = slice(h * HEAD_DIM, (h + 1) * HEAD_DIM)
        slab = (wa[:, h:h + 1] * o1_ref[h].astype(F32)
                + wb[:, h:h + 1] * s4_ref[h]
                + wc[:, h:h + 1] * s16_ref[h])
        acc_ref[:, cs] = slab
        sq = sq + jnp.sum(slab * slab, axis=-1, keepdims=True)
    inv = lax.rsqrt(sq * (1.0 / ATTN_WIDTH) + EPS)
    out_ref[...] = (acc_ref[...] * inv * g_ref[...]).astype(out_ref.dtype)


def _merge_patterns(outs, lses, gain):
    o1, o4, o16 = outs
    s = o1.shape[2]
    tm = 256
    ospec = pl.BlockSpec((tm, ATTN_WIDTH), lambda i: (i, 0))
    lspec = pl.BlockSpec((tm, N_HEADS), lambda i: (i, 0))
    return pl.pallas_call(
        functools.partial(_merge_kernel, tm=tm),
        grid=(s // tm,),
        in_specs=[pl.BlockSpec((None, N_HEADS, tm, HEAD_DIM), lambda i: (0, 0, i, 0)),
                  pl.BlockSpec((4, N_HEADS, tm // 4, HEAD_DIM), lambda i: (0, 0, i, 0)),
                  pl.BlockSpec((16, N_HEADS, tm // 16, HEAD_DIM), lambda i: (0, 0, i, 0)),
                  lspec, lspec, lspec,
                  pl.BlockSpec((1, ATTN_WIDTH), lambda i: (0, 0))],
        out_specs=ospec,
        out_shape=jax.ShapeDtypeStruct((s, ATTN_WIDTH), BF16),
        scratch_shapes=[pltpu.VMEM((tm, ATTN_WIDTH), F32),
                        pltpu.VMEM((N_HEADS, tm, HEAD_DIM), F32),
                        pltpu.VMEM((N_HEADS, tm, HEAD_DIM), F32)],
        compiler_params=_params(("parallel",), 32),
        name="attn_merge_norm",
    )(o1, o4, o16, *lses, gain)


def _ssm_direction_terms(lam_re, lam_im, log_dt, b_re, b_im, c_re, c_im):
    dt = jnp.exp(log_dt)[:, None]
    mag = jnp.exp(lam_re * dt)
    ang = lam_im * dt
    lb_re = mag * jnp.cos(ang)
    lb_im = mag * jnp.sin(ang)
    den = lam_re * lam_re + lam_im * lam_im
    ar = lb_re - 1.0
    ai = lb_im
    coef_re = (ar * lam_re + ai * lam_im) / den
    coef_im = (ai * lam_re - ar * lam_im) / den
    bb_re = coef_re[..., None] * b_re - coef_im[..., None] * b_im
    bb_im = coef_re[..., None] * b_im + coef_im[..., None] * b_re
    tau = jnp.arange(SSM_CHUNK + 1, dtype=F32)[:, None, None]
    pw_mag = jnp.exp(tau * (lam_re * dt)[None])
    pw_re = pw_mag * jnp.cos(tau * ang[None])
    pw_im = pw_mag * jnp.sin(tau * ang[None])
    e_re = pw_re[..., None] * bb_re[None] - pw_im[..., None] * bb_im[None]
    e_im = pw_re[..., None] * bb_im[None] + pw_im[..., None] * bb_re[None]
    kern = (jnp.einsum('ghp,tgpi->tghi', c_re, e_re[:SSM_CHUNK], precision=HIGHEST)
            - jnp.einsum('ghp,tgpi->tghi', c_im, e_im[:SSM_CHUNK], precision=HIGHEST))
    ca_re = c_re[None] * pw_re[:, :, None, :] - c_im[None] * pw_im[:, :, None, :]
    ca_im = c_re[None] * pw_im[:, :, None, :] + c_im[None] * pw_re[:, :, None, :]
    return kern, e_re, e_im, ca_re, ca_im, pw_re[SSM_CHUNK], pw_im[SSM_CHUNK]


def _ssm_operators(lam_re_f, lam_im_f, log_dt_f, lam_re_b, lam_im_b, log_dt_b, b_re, b_im, c_re, c_im):
    L, G, P = SSM_CHUNK, SSM_GROUPS, SSM_STATE
    kf, ef_re, ef_im, caf_re, caf_im, af_re, af_im = _ssm_direction_terms(
        lam_re_f, lam_im_f, log_dt_f, b_re, b_im, c_re, c_im)
    kb, eb_re, eb_im, cab_re, cab_im, ab_re, ab_im = _ssm_direction_terms(
        lam_re_b, lam_im_b, log_dt_b, b_re, b_im, c_re, c_im)
    pad = jnp.zeros((L - 1,) + kf.shape[1:], F32)
    kf_ext = jnp.concatenate([pad, kf], axis=0)
    kb_ext = jnp.concatenate([kb[::-1], pad], axis=0)
    wf = jnp.stack([kf_ext[L - 1 - i:2 * L - 1 - i] for i in range(L)])
    wb = jnp.stack([kb_ext[L - 1 - i:2 * L - 1 - i] for i in range(L)])
    w = jnp.transpose(wf + wb, (2, 0, 4, 1, 3)).reshape(G, L * SSM_GROUP, L * SSM_GROUP)

    def p_mat(e, order):
        return jnp.transpose(e[order], (1, 0, 3, 2)).reshape(G, L * SSM_GROUP, P)

    def q_mat(ca, order):
        return jnp.transpose(ca[order], (1, 3, 0, 2)).reshape(G, P, L * SSM_GROUP)

    f_ord = (L - 1) - jnp.arange(L)
    b_ord = jnp.arange(L)
    parts_p = [p_mat(ef_re, f_ord), p_mat(ef_im, f_ord), p_mat(eb_re, b_ord), p_mat(eb_im, b_ord)]
    parts_q = [q_mat(caf_re, jnp.arange(L) + 1), -q_mat(caf_im, jnp.arange(L) + 1),
               q_mat(cab_re, L - jnp.arange(L)), -q_mat(cab_im, L - jnp.arange(L))]
    even = (jnp.arange(G) % 2 == 0)[:, None, None]

    def pad_p(x):
        z = jnp.zeros_like(x)
        return jnp.where(even, jnp.concatenate([x, z], -1), jnp.concatenate([z, x], -1))

    def pad_q(x):
        z = jnp.zeros_like(x)
        return jnp.where(even, jnp.concatenate([x, z], 1), jnp.concatenate([z, x], 1))

    p_all = jnp.stack([pad_p(x) for x in parts_p], axis=1).astype(BF16)
    q_all = jnp.stack([pad_q(x) for x in parts_q], axis=1).astype(BF16)

    def tiles(re, im):
        re = re.reshape(G // SSM_GB, SSM_GB // 2, 2 * P)
        im = im.reshape(G // SSM_GB, SSM_GB // 2, 2 * P)
        return jnp.concatenate([re, re], axis=1), jnp.concatenate([-im, im], axis=1)

    a_all = jnp.stack([*tiles(af_re, af_im), *tiles(ab_re, ab_im)], axis=1)
    return w.astype(BF16), p_all, q_all, a_all


def _gelu_tanh(x):
    return 0.5 * x * (1.0 + jnp.tanh(math.sqrt(2.0 / math.pi) * (x + 0.044715 * (x * x * x))))


SUBLANES = 8


def _ssm_kernel(u_ref, w_ref, p_ref, q_ref, a_ref, d_ref, y_ref, sf, sb, *, nc, gb):
    npair = gb // 2
    assert 2 * npair == SUBLANES
    for pr in range(npair):
        g0, g1 = 2 * pr, 2 * pr + 1
        for k, (scr, row) in enumerate(((sf, pr), (sf, npair + pr), (sb, pr), (sb, npair + pr))):
            contrib = (jnp.dot(u_ref[g0], p_ref[g0, k], preferred_element_type=F32)
                       + jnp.dot(u_ref[g1], p_ref[g1, k], preferred_element_type=F32))
            scr[pl.ds(row, nc, stride=SUBLANES), :] = contrib

    a1f, a2f, a1b, a2b = a_ref[0, 0], a_ref[0, 1], a_ref[0, 2], a_ref[0, 3]

    def step(c, carry):
        hf, hb = carry
        cf = pl.multiple_of(c * SUBLANES, SUBLANES)
        cb = pl.multiple_of((nc - 1 - c) * SUBLANES, SUBLANES)
        rf = sf[pl.ds(cf, SUBLANES), :]
        rb = sb[pl.ds(cb, SUBLANES), :]
        sf[pl.ds(cf, SUBLANES), :] = hf
        sb[pl.ds(cb, SUBLANES), :] = hb
        return (a1f * hf + a2f * pltpu.roll(hf, npair, 0) + rf,
                a1b * hb + a2b * pltpu.roll(hb, npair, 0) + rb)

    z = jnp.zeros((SUBLANES, LANES), F32)
    lax.fori_loop(0, nc, step, (z, z))

    for pr in range(npair):
        states = [scr[pl.ds(row, nc, stride=SUBLANES), :].astype(BF16)
                  for scr, row in ((sf, pr), (sf, npair + pr), (sb, pr), (sb, npair + pr))]
        for g in (2 * pr, 2 * pr + 1):
            u = u_ref[g]
            y = jnp.dot(u, w_ref[g], preferred_element_type=F32)
            for k in range(4):
                y = y + jnp.dot(states[k], q_ref[g, k], preferred_element_type=F32)
            y = y + d_ref[g] * u.astype(F32)
            y_ref[g] = _gelu_tanh(y).astype(y_ref.dtype)


def _ssm_mixer(u_t, w, p_all, q_all, a_all, d_t):
    g, nc, k = u_t.shape
    gb = SSM_GB
    return pl.pallas_call(
        functools.partial(_ssm_kernel, nc=nc, gb=gb),
        grid=(g // gb,),
        in_specs=[pl.BlockSpec((gb, nc, k), lambda i: (i, 0, 0)),
                  pl.BlockSpec((gb, k, k), lambda i: (i, 0, 0)),
                  pl.BlockSpec((gb, 4, k, LANES), lambda i: (i, 0, 0, 0)),
                  pl.BlockSpec((gb, 4, LANES, k), lambda i: (i, 0, 0, 0)),
                  pl.BlockSpec((1, 4, SUBLANES, LANES), lambda i: (i, 0, 0, 0)),
                  pl.BlockSpec((gb, 1, k), lambda i: (i, 0, 0))],
        out_specs=pl.BlockSpec((gb, nc, k), lambda i: (i, 0, 0)),
        out_shape=jax.ShapeDtypeStruct((g, nc, k), BF16),
        scratch_shapes=[pltpu.VMEM((nc * SUBLANES, LANES), F32)] * 2,
        compiler_params=_params(("parallel",), 48),
        name="ssm_scan",
    )(u_t, w, p_all, q_all, a_all, d_t)


def _glu_kernel(g_ref, w_ref, b_ref, n_ref, o_ref, nat, gb, *, tm):
    nchunk = tm // SSM_CHUNK
    for sl in range(SSM_WIDTH // LANES):
        for half in range(2):
            xs = [g_ref[8 * sl + gp, :, half * LANES:(half + 1) * LANES].astype(F32) for gp in range(8)]
            ys = _piece_transpose8(xs)
            for jj in range(8):
                nat[sl, pl.ds(half * 8 + jj, nchunk, stride=SSM_CHUNK), :] = ys[jj]
        gb[:, sl * LANES:(sl + 1) * LANES] = nat[sl].astype(BF16)
    z = jnp.dot(gb[...], w_ref[...], preferred_element_type=F32) + b_ref[...]
    sq = jnp.zeros((tm, 1), F32)
    for sl in range(SSM_WIDTH // LANES):
        cs = slice(sl * LANES, (sl + 1) * LANES)
        out = nat[sl] * jax.nn.sigmoid(z[:, cs])
        nat[sl] = out
        sq = sq + jnp.sum(out * out, axis=-1, keepdims=True)
    inv = lax.rsqrt(sq * (1.0 / SSM_WIDTH) + EPS)
    for sl in range(SSM_WIDTH // LANES):
        cs = slice(sl * LANES, (sl + 1) * LANES)
        o_ref[:, cs] = (nat[sl] * inv * n_ref[:, cs]).astype(o_ref.dtype)


def _glu_norm(g_t, w_glu, b_glu, gain):
    ng, nc, k = g_t.shape
    s, d = nc * SSM_CHUNK, ng * SSM_GROUP
    tm = 512
    row = pl.BlockSpec((1, d), lambda i: (0, 0))
    return pl.pallas_call(
        functools.partial(_glu_kernel, tm=tm),
        grid=(s // tm,),
        in_specs=[pl.BlockSpec((ng, tm // SSM_CHUNK, k), lambda i: (0, i, 0)),
                  pl.BlockSpec((d, d), lambda i: (0, 0)), row, row],
        out_specs=pl.BlockSpec((tm, d), lambda i: (i, 0)),
        out_shape=jax.ShapeDtypeStruct((s, d), BF16),
        scratch_shapes=[pltpu.VMEM((d // LANES, tm, LANES), F32),
                        pltpu.VMEM((tm, d), BF16)],
        compiler_params=_params(("parallel",), 48),
        name="ssm_glu_norm",
    )(g_t, w_glu, b_glu, gain)


def _outproj_kernel(a1_ref, a2_ref, w1_ref, w2_ref, x_ref, gm_ref, o_ref):
    mix = (jnp.dot(a1_ref[...], w1_ref[...], preferred_element_type=F32)
           + jnp.dot(a2_ref[...], w2_ref[...], preferred_element_type=F32))
    o_ref[...] = x_ref[...] + gm_ref[...] * mix


def _out_proj(attn_n, ssm_n, w_out, x, g_m):
    s, k = attn_n.shape
    n = w_out.shape[1]
    tm, tn = 1024, 512
    return pl.pallas_call(
        _outproj_kernel,
        grid=(s // tm, n // tn),
        in_specs=[pl.BlockSpec((tm, k), lambda i, j: (i, 0)),
                  pl.BlockSpec((tm, k), lambda i, j: (i, 0)),
                  pl.BlockSpec((k, tn), lambda i, j: (0, j)),
                  pl.BlockSpec((k, tn), lambda i, j: (1, j)),
                  pl.BlockSpec((tm, tn), lambda i, j: (i, j)),
                  pl.BlockSpec((1, tn), lambda i, j: (0, j))],
        out_specs=pl.BlockSpec((tm, tn), lambda i, j: (i, j)),
        out_shape=jax.ShapeDtypeStruct((s, n), F32),
        compiler_params=_params(("parallel", "arbitrary"), 48),
        name="out_proj_residual",
    )(attn_n, ssm_n, w_out, w_out, x, g_m)


def _ffn_prep_kernel(h_ref, g_ref, sc_ref, sh_ref, wr_ref, xf_ref, lg_ref):
    xf = _norm_mod(h_ref[...], g_ref[...], sc_ref[...], sh_ref[...])
    half = xf.shape[1] // 2
    lo = pltpu.bitcast(xf[:, :half].astype(BF16).astype(F32), jnp.uint32)
    hi = pltpu.bitcast(xf[:, half:].astype(BF16).astype(F32), jnp.uint32)
    xf_ref[...] = (lo >> 16) | hi
    lg_ref[...] = lax.dot_general(wr_ref[...], xf, (((1,), (1,)), ((), ())),
                                  precision=HIGHEST, preferred_element_type=F32)


def _ffn_prep(h1, gain, scale, shift, w_router_t):
    s, d = h1.shape
    tm = 256
    row = pl.BlockSpec((1, d), lambda i: (0, 0))
    return pl.pallas_call(
        _ffn_prep_kernel,
        grid=(s // tm,),
        in_specs=[pl.BlockSpec((tm, d), lambda i: (i, 0)), row, row, row,
                  pl.BlockSpec((N_EXPERTS, d), lambda i: (0, 0))],
        out_specs=[pl.BlockSpec((tm, d // 2), lambda i: (i, 0)),
                   pl.BlockSpec((N_EXPERTS, tm), lambda i: (0, i))],
        out_shape=[jax.ShapeDtypeStruct((s, d // 2), jnp.uint32),
                   jax.ShapeDtypeStruct((N_EXPERTS, s), F32)],
        compiler_params=_params(("parallel",), 40),
        name="ffn_norm_router",
    )(h1, gain, scale, shift, w_router_t)


def _route_kernel(lg_ref, sel_ref, pos_ref, aff_ref, *, cap, nk):
    e_n = N_EXPERTS
    lg = lg_ref[...]
    mx = jnp.max(lg, axis=0, keepdims=True)
    ex = jnp.exp(lg - mx)
    aff = ex / jnp.sum(ex, axis=0, keepdims=True)
    aff_ref[...] = aff
    bits = pltpu.bitcast(aff, I32)

    def count(mask):
        return jnp.sum(jnp.sum(mask.astype(F32), axis=2, keepdims=True), axis=1, keepdims=True)

    def radix(i, thr):
        cand = thr | jnp.left_shift(jnp.int32(1), 30 - i)
        return jnp.where(count(bits >= cand) >= cap, cand, thr)

    thr = lax.fori_loop(0, 31, radix, jnp.zeros((e_n, 1, 1), I32))
    gt = bits > thr
    eq = bits == thr
    need = cap - count(gt)

    li = lax.broadcasted_iota(I32, (LANES, LANES), 0)
    lj = lax.broadcasted_iota(I32, (LANES, LANES), 1)
    tri_lane = (li < lj).astype(BF16)
    ki = lax.broadcasted_iota(I32, (nk, nk), 0)
    kj = lax.broadcasted_iota(I32, (nk, nk), 1)
    tri_row = (kj < ki).astype(BF16)

    def excl_prefix(mask):
        mf = mask.astype(F32)
        within = jnp.dot(mf.reshape(e_n * nk, LANES).astype(BF16), tri_lane,
                         preferred_element_type=F32).reshape(e_n, nk, LANES)
        rowtot = jnp.broadcast_to(jnp.sum(mf, axis=2, keepdims=True), (e_n, nk, LANES)).astype(BF16)
        across = jnp.stack([jnp.dot(tri_row, rowtot[e], preferred_element_type=F32) for e in range(e_n)])
        return within + across

    sel = gt | (eq & (excl_prefix(eq) < need))
    sel_ref[...] = sel.astype(I32)
    pos_ref[...] = excl_prefix(sel).astype(I32)


def _route(logits_t, cap):
    e_n, s = logits_t.shape
    nk = s // LANES
    spec = pl.BlockSpec((e_n, nk, LANES), lambda i: (0, 0, 0))
    sel, pos, aff = pl.pallas_call(
        functools.partial(_route_kernel, cap=cap, nk=nk),
        grid=(1,),
        in_specs=[spec],
        out_specs=[spec, spec, spec],
        out_shape=[jax.ShapeDtypeStruct((e_n, nk, LANES), I32),
                   jax.ShapeDtypeStruct((e_n, nk, LANES), I32),
                   jax.ShapeDtypeStruct((e_n, nk, LANES), F32)],
        compiler_params=_params(("arbitrary",), 48),
        name="expert_choice_route",
    )(logits_t.reshape(e_n, nk, LANES))
    return sel.reshape(e_n, s), pos.reshape(e_n, s), aff.reshape(e_n, s)


def _compact_kernel(off_ref, sel_ref, pos_ref, aff_ref, idx_ref, gate_ref, *, nk):
    e = pl.program_id(0)
    idx_ref[...] = jnp.zeros(idx_ref.shape, I32)
    gate_ref[...] = jnp.zeros(gate_ref.shape, F32)
    local = lax.broadcasted_iota(I32, (LANES, LANES), 0)
    lane = lax.broadcasted_iota(I32, (LANES, LANES), 1)

    unroll = 4 if nk % 4 == 0 else 1

    def body(it, carry):
        for u in range(unroll):
            k = it * unroll + u
            off = off_ref[e, k]
            sel = sel_ref[0, pl.ds(k, 1), :]
            pos = pos_ref[0, pl.ds(k, 1), :]
            aff = aff_ref[0, pl.ds(k, 1), :]
            hit = (sel > 0) & ((pos - off) == local)
            tok = jnp.sum(jnp.where(hit, (lane + k * LANES).astype(F32), 0.0), axis=1, keepdims=True)
            gat = jnp.sum(jnp.where(hit, aff, 0.0), axis=1, keepdims=True)
            idx_ref[0, pl.ds(off, LANES), :] = tok.astype(I32)
            gate_ref[0, pl.ds(off, LANES), :] = gat
        return carry

    lax.fori_loop(0, nk // unroll, body, 0)


def _compact(sel, pos, aff, cap):
    e_n, s = sel.shape
    nk = s // LANES
    capp = cap + LANES
    offs = pos[:, ::LANES]
    tok = pl.BlockSpec((1, nk, LANES), lambda e, off: (e, 0, 0))
    col = pl.BlockSpec((1, capp, 1), lambda e, off: (e, 0, 0))
    grid_spec = pltpu.PrefetchScalarGridSpec(
        num_scalar_prefetch=1, grid=(e_n,), in_specs=[tok, tok, tok], out_specs=[col, col])
    idx, gate = pl.pallas_call(
        functools.partial(_compact_kernel, nk=nk),
        grid_spec=grid_spec,
        out_shape=[jax.ShapeDtypeStruct((e_n, capp, 1), I32),
                   jax.ShapeDtypeStruct((e_n, capp, 1), F32)],
        compiler_params=_params(("arbitrary",), 32),
        name="expert_compact",
    )(offs, sel.reshape(e_n, nk, LANES), pos.reshape(e_n, nk, LANES), aff.reshape(e_n, nk, LANES))
    return idx[:, :cap, 0], gate[:, :cap, :]


def _moe_up_kernel(idx_hbm, xf_hbm, wg_ref, wu_ref, h_ref, xbuf, idx_s, sem_x, sem_i, *, tm, rps):
    nh, nf = pl.num_programs(1), pl.num_programs(2)
    f = pl.program_id(2)
    phase = pl.program_id(0) * nh + pl.program_id(1)
    n_phase = pl.num_programs(0) * nh
    step = phase * nf + f
    b = phase % 2

    def row_copy(token, buf, r):
        return pltpu.make_async_copy(xf_hbm.at[pl.ds(token, 1), :], xbuf.at[buf, pl.ds(r, 1), :], sem_x.at[buf])

    def wait_rows(buf, n):
        def one(r, c):
            row_copy(0, buf, r).wait()
            return c
        lax.fori_loop(0, n, one, 0)

    @pl.when(step == 0)
    def _prologue():
        c_idx = pltpu.make_async_copy(idx_hbm, idx_s, sem_i.at[0])
        c_idx.start()
        c_idx.wait()

        def one(r, c):
            row_copy(idx_s[r], 0, r).start()
            return c
        lax.fori_loop(0, tm, one, 0)
        wait_rows(0, tm)

    @pl.when(step > 0)
    def _():
        wait_rows(jnp.where(f == 0, b, 1 - b), rps)

    nxt = jnp.minimum(phase + 1, n_phase - 1)
    base = nxt * tm + f * rps
    for r in range(rps):
        row_copy(idx_s[base + r], 1 - b, f * rps + r).start()

    x = xbuf[b]
    x_lo = pltpu.bitcast(x << 16, F32).astype(BF16)
    x_hi = pltpu.bitcast(x & jnp.uint32(0xFFFF0000), F32).astype(BF16)
    half = x.shape[1]
    wg = wg_ref[0].astype(BF16)
    wu = wu_ref[0].astype(BF16)
    g = (jnp.dot(x_lo, wg[:half], preferred_element_type=F32)
         + jnp.dot(x_hi, wg[half:], preferred_element_type=F32))
    u = (jnp.dot(x_lo, wu[:half], preferred_element_type=F32)
         + jnp.dot(x_hi, wu[half:], preferred_element_type=F32))
    h_ref[0] = (g * jax.nn.sigmoid(g) * u).astype(h_ref.dtype)

    @pl.when(step == n_phase * nf - 1)
    def _drain():
        wait_rows(1 - b, rps)


def _moe_up(idx, xf_packed, w_gate, w_up, cap):
    e_n, d, ff = w_gate.shape
    tf = 256
    tm = min(1024, cap)
    nf = ff // tf
    rps = tm // nf
    assert rps * nf == tm and cap % tm == 0
    return pl.pallas_call(
        functools.partial(_moe_up_kernel, tm=tm, rps=rps),
        grid=(e_n, cap // tm, nf),
        in_specs=[pl.BlockSpec(memory_space=pl.ANY),
                  pl.BlockSpec(memory_space=pl.ANY),
                  pl.BlockSpec((1, d, tf), lambda e, m, f: (e, 0, f)),
                  pl.BlockSpec((1, d, tf), lambda e, m, f: (e, 0, f))],
        out_specs=pl.BlockSpec((1, tm, tf), lambda e, m, f: (e, m, f)),
        out_shape=jax.ShapeDtypeStruct((e_n, cap, ff), BF16),
        scratch_shapes=[pltpu.VMEM((2, tm, d // 2), jnp.uint32),
                        pltpu.SMEM((e_n * cap,), I32),
                        pltpu.SemaphoreType.DMA((2,)),
                        pltpu.SemaphoreType.DMA((1,))],
        compiler_params=_params(("arbitrary", "arbitrary", "arbitrary"), 56),
        name="moe_gather_up",
    )(idx.reshape(e_n * cap), xf_packed, w_gate, w_up)


def _moe_down_kernel(h_ref, wd_ref, gate_ref, y_ref):
    y = jnp.dot(h_ref[0], wd_ref[0].astype(BF16), preferred_element_type=F32)
    y_ref[0] = (y * gate_ref[0]).astype(y_ref.dtype)


def _moe_down(hdn, w_down, gates):
    e_n, cap, ff = hdn.shape
    d = w_down.shape[2]
    tn = 512
    return pl.pallas_call(
        _moe_down_kernel,
        grid=(e_n, d // tn),
        in_specs=[pl.BlockSpec((1, cap, ff), lambda e, n: (e, 0, 0)),
                  pl.BlockSpec((1, ff, tn), lambda e, n: (e, 0, n)),
                  pl.BlockSpec((1, cap, 1), lambda e, n: (e, 0, 0))],
        out_specs=pl.BlockSpec((1, cap, tn), lambda e, n: (e, 0, n)),
        out_shape=jax.ShapeDtypeStruct((e_n, cap, d), BF16),
        compiler_params=_params(("parallel", "arbitrary"), 48),
        name="moe_down",
    )(hdn, w_down, gates)


COMBINE_WINDOW = 64
SLOT_ALIGN = 16


def _combine_kernel(lo_ref, h_ref, sel_ref, pos_ref, gf_ref, gn_ref, y_hbm, o_ref,
                    ybuf, yextra, acc_ref, sem, sem_x, *, cap, tm):
    i = pl.program_id(0)
    win = COMBINE_WINDOW
    e_n = N_EXPERTS

    n_steps = pl.num_programs(0)
    cur = i % 2

    def window(e, j, step=None):
        step = i if step is None else step
        nominal = (lo_ref[e, step] // SLOT_ALIGN) * SLOT_ALIGN + j * win
        start = pl.multiple_of(jnp.minimum(nominal, cap - win), SLOT_ALIGN)
        return nominal, start

    def first_copy(e, step, buf):
        _, start = window(e, 0, step)
        return pltpu.make_async_copy(y_hbm.at[e, pl.ds(start, win), :],
                                     ybuf.at[buf, pl.ds(e * win, win), :], sem.at[buf, e])

    @pl.when(i == 0)
    def _():
        for e in range(e_n):
            first_copy(e, 0, 0).start()

    @pl.when(i + 1 < n_steps)
    def _():
        for e in range(e_n):
            first_copy(e, i + 1, 1 - cur).start()

    sel = sel_ref[...]
    pos = pos_ref[...]
    lane = lax.broadcasted_iota(I32, (tm, LANES), 1)
    left = lane < win
    pieces = []
    for e2 in range(e_n // 2):
        ea, eb = 2 * e2, 2 * e2 + 1
        na, sa = window(ea, 0)
        nb, sb = window(eb, 0)
        slot = jnp.where(left, sa + lane, sb + lane - win)
        nominal = jnp.where(left, na, nb)
        pe = jnp.where(left, pos[:, ea:ea + 1], pos[:, eb:eb + 1])
        se = jnp.where(left, sel[:, ea:ea + 1], sel[:, eb:eb + 1])
        pieces.append(((pe == slot) & (se > 0) & (slot >= nominal)).astype(BF16))
    onehot = jnp.concatenate(pieces, axis=1)
    for e in range(e_n):
        first_copy(e, i, cur).wait()
    acc_ref[...] = jnp.dot(onehot, ybuf[cur], preferred_element_type=F32)

    lane_w = lax.broadcasted_iota(I32, (tm, win), 1)
    for e in range(e_n):
        base = (lo_ref[e, i] // SLOT_ALIGN) * SLOT_ALIGN
        nwin = (lo_ref[e, i + 1] - base + win - 1) // win

        def extra(j, c, e=e):
            nominal, start = window(e, j)
            cp = pltpu.make_async_copy(y_hbm.at[e, pl.ds(start, win), :], yextra, sem_x.at[0])
            cp.start()
            cp.wait()
            slot = start + lane_w
            oh = ((pos[:, e:e + 1] == slot) & (sel[:, e:e + 1] > 0) & (slot >= nominal)).astype(BF16)
            acc_ref[...] += jnp.dot(oh, yextra[...], preferred_element_type=F32)
            return c

        lax.fori_loop(1, nwin, extra, 0)

    h2 = h_ref[...] + gf_ref[...] * acc_ref[...]
    ms = jnp.mean(h2 * h2, axis=-1, keepdims=True)
    o_ref[...] = h2 * lax.rsqrt(ms + EPS) * gn_ref[...]


def _combine(h1, y, sel_tm, pos_tm, lo, g_f, gain, cap):
    s, d = h1.shape
    tm = 256
    win = COMBINE_WINDOW
    row = pl.BlockSpec((1, d), lambda i, lo_r: (0, 0))
    tok = pl.BlockSpec((tm, N_EXPERTS), lambda i, lo_r: (i, 0))
    grid_spec = pltpu.PrefetchScalarGridSpec(
        num_scalar_prefetch=1,
        grid=(s // tm,),
        in_specs=[pl.BlockSpec((tm, d), lambda i, lo_r: (i, 0)), tok, tok, row, row,
                  pl.BlockSpec(memory_space=pl.ANY)],
        out_specs=pl.BlockSpec((tm, d), lambda i, lo_r: (i, 0)),
        scratch_shapes=[pltpu.VMEM((2, N_EXPERTS * win, d), BF16),
                        pltpu.VMEM((win, d), BF16),
                        pltpu.VMEM((tm, d), F32),
                        pltpu.SemaphoreType.DMA((2, N_EXPERTS)),
                        pltpu.SemaphoreType.DMA((1,))],
    )
    return pl.pallas_call(
        functools.partial(_combine_kernel, cap=cap, tm=tm),
        grid_spec=grid_spec,
        out_shape=jax.ShapeDtypeStruct((s, d), F32),
        compiler_params=_params(("arbitrary",), 56),
        name="moe_combine_norm",
    )(lo, h1, sel_tm, pos_tm, g_f, gain, y)


def kernel(x, c, positions, w_ada, b_ada, norm_mix_gain, w_in, lam_re_fwd, lam_im_fwd, log_dt_fwd, lam_re_bwd, lam_im_bwd, log_dt_bwd, ssm_b_re, ssm_b_im, ssm_c_re, ssm_c_im, ssm_d, w_glu, b_glu, norm_attn_out_gain, norm_ssm_out_gain, w_out, norm_ffn_gain, w_router, w_exp_gate, w_exp_up, w_exp_down, norm_final_gain):
    batch, s, d = x.shape
    depth = w_ada.shape[0]
    assert batch == 1 and d == D_MODEL and depth == 1
    cap = max(1, CAPACITY_FACTOR * s // N_EXPERTS)
    h = x[0]
    pos = positions[0]
    cos_t, sin1_t, sin2_t = _rope_tables(pos)
    for layer in range(depth):
        mod = _adaln(c, w_ada[layer], b_ada[layer])
        sh_m, sc_m, g_m, sh_f, sc_f, g_f = jnp.split(mod, 6, axis=-1)

        xm = _norm_mod_call(h, norm_mix_gain[layer][None], sc_m, sh_m, BF16)
        qkv, qkv4, qkv16 = _qkv_proj(xm, w_in[layer, :, :3 * ATTN_WIDTH].astype(BF16), cos_t, sin1_t, sin2_t)
        outs, lses = [], []
        for dil, arr in zip(DILATIONS, (qkv[None], qkv4, qkv16)):
            o, l = _dilated_attention(arr, dil)
            outs.append(o)
            lses.append(l)
        attn_n = _merge_patterns(outs, lses, norm_attn_out_gain[layer][None])

        u_t = _u_proj(xm, w_in[layer, :, 3 * ATTN_WIDTH:].astype(BF16))
        w_t, p_all, q_all, a_all = _ssm_operators(
            lam_re_fwd[layer], lam_im_fwd[layer], log_dt_fwd[layer],
            lam_re_bwd[layer], lam_im_bwd[layer], log_dt_bwd[layer],
            ssm_b_re[layer], ssm_b_im[layer], ssm_c_re[layer], ssm_c_im[layer])
        d_t = jnp.tile(ssm_d[layer].reshape(SSM_GROUPS, 1, SSM_GROUP), (1, 1, SSM_CHUNK))
        g_t = _ssm_mixer(u_t, w_t, p_all, q_all, a_all, d_t)
        ssm_n = _glu_norm(g_t, w_glu[layer].astype(BF16), b_glu[layer][None], norm_ssm_out_gain[layer][None])

        h = _out_proj(attn_n, ssm_n, w_out[layer].astype(BF16), h, g_m)

        xf, logits_t = _ffn_prep(h, norm_ffn_gain[layer][None], sc_f, sh_f, jnp.transpose(w_router[layer]))
        sel, slot, aff = _route(logits_t, cap)
        idx, gates = _compact(sel, slot, aff, cap)
        hdn = _moe_up(idx, xf, w_exp_gate[layer], w_exp_up[layer], cap)
        y = _moe_down(hdn, w_exp_down[layer], gates)
        tm = 256
        lo = jnp.concatenate([slot[:, ::tm], jnp.full((N_EXPERTS, 1), cap, I32)], axis=1)
        h = _combine(h, y, jnp.transpose(sel), jnp.transpose(slot), lo, g_f, norm_final_gain[None], cap)
    return h[None]
```

```python
import functools
import math

import jax
import jax.numpy as jnp
from jax import lax
from jax.experimental import pallas as pl
from jax.experimental.pallas import tpu as pltpu

F32 = jnp.float32
BF16 = jnp.bfloat16
I32 = jnp.int32
HIGHEST = lax.Precision.HIGHEST

D_MODEL = 4096
ATTN_WIDTH = 2048
SSM_WIDTH = 2048
HEAD_DIM = 128
N_HEADS = 16
IN_PROJ_WIDTH = 3 * ATTN_WIDTH + SSM_WIDTH
ROPE_DIM = 32
ROPE_THETA = 500000.0
DILATIONS = (1, 4, 16)
ATTN_HALF = 64
NEG_INF = -1e30
SSM_GROUP = 16
SSM_GROUPS = 128
SSM_STATE = 64
SSM_CHUNK = 16
SSM_GB = 8
N_EXPERTS = 16
EXPERT_FF = 2048
CAPACITY_FACTOR = 2
EPS = 1e-6

LANES = 128
MIB = 1024 * 1024


def _params(semantics, vmem_mib):
    return pltpu.CompilerParams(dimension_semantics=semantics, vmem_limit_bytes=vmem_mib * MIB)


def _ada_kernel(c_ref, w_ref, b_ref, o_ref):
    c = c_ref[...]
    ca = c * jax.nn.sigmoid(c)
    o_ref[...] = jnp.dot(ca, w_ref[...], precision=HIGHEST, preferred_element_type=F32) + b_ref[...]


def _adaln(c, w_ada, b_ada):
    d, n = w_ada.shape
    tn = 512
    c8 = jnp.pad(c, ((0, 7), (0, 0)))
    out = pl.pallas_call(
        _ada_kernel,
        grid=(n // tn,),
        in_specs=[pl.BlockSpec((8, d), lambda j: (0, 0)),
                  pl.BlockSpec((d, tn), lambda j: (0, j)),
                  pl.BlockSpec((1, tn), lambda j: (0, j))],
        out_specs=pl.BlockSpec((8, tn), lambda j: (0, j)),
        out_shape=jax.ShapeDtypeStruct((8, n), F32),
        compiler_params=_params(("parallel",), 40),
        name="adaln",
    )(c8, w_ada, b_ada.reshape(1, n))
    return out[0:1]


def _norm_mod(x, gain, scale, shift):
    ms = jnp.mean(x * x, axis=-1, keepdims=True)
    y = x * lax.rsqrt(ms + EPS) * gain
    return y * (1.0 + scale) + shift


def _norm_mod_kernel(x_ref, g_ref, sc_ref, sh_ref, o_ref):
    o_ref[...] = _norm_mod(x_ref[...], g_ref[...], sc_ref[...], sh_ref[...]).astype(o_ref.dtype)


def _norm_mod_call(x, gain, scale, shift, out_dtype):
    s, d = x.shape
    tm = 256
    row = pl.BlockSpec((1, d), lambda i: (0, 0))
    return pl.pallas_call(
        _norm_mod_kernel,
        grid=(s // tm,),
        in_specs=[pl.BlockSpec((tm, d), lambda i: (i, 0)), row, row, row],
        out_specs=pl.BlockSpec((tm, d), lambda i: (i, 0)),
        out_shape=jax.ShapeDtypeStruct((s, d), out_dtype),
        compiler_params=_params(("parallel",), 40),
        name="norm_mod",
    )(x, gain, scale, shift)


def _rope_kernel(pos_ref, c_ref, s1_ref, s2_ref):
    half = ROPE_DIM // 2
    pos = pos_ref[...].astype(F32)
    lane = lax.broadcasted_iota(I32, (1, LANES), 1)
    fidx = (lane & (half - 1)).astype(F32)
    inv_freq = jnp.exp(fidx * (-math.log(ROPE_THETA) / half))
    ang = pos * inv_freq
    cs = jnp.cos(ang)
    sn = jnp.sin(ang)
    c_ref[...] = jnp.where(lane < ROPE_DIM, cs, 1.0)
    s1_ref[...] = jnp.where(lane < half, -sn, 0.0)
    s2_ref[...] = jnp.where((lane >= half) & (lane < ROPE_DIM), sn, 0.0)


def _rope_tables(positions):
    s = positions.shape[0]
    tm = 512
    spec = pl.BlockSpec((tm, LANES), lambda i: (i, 0))
    shp = jax.ShapeDtypeStruct((s, LANES), F32)
    return pl.pallas_call(
        _rope_kernel,
        grid=(s // tm,),
        in_specs=[pl.BlockSpec((tm, 1), lambda i: (i, 0))],
        out_specs=[spec, spec, spec],
        out_shape=[shp, shp, shp],
        compiler_params=_params(("parallel",), 32),
        name="rope_tables",
    )(positions.reshape(s, 1))


PROJ_TM = 1024
PROJ_TN = 512


def _piece_transpose8(xs):
    xs = list(xs)
    lane = lax.broadcasted_iota(I32, xs[0].shape, 1)
    for h in (4, 2, 1):
        low = (lane & (16 * h)) == 0
        for a in range(8):
            if a & h:
                continue
            lo, hi = xs[a], xs[a + h]
            xs[a] = jnp.where(low, lo, pltpu.roll(hi, 16 * h, 1))
            xs[a + h] = jnp.where(low, pltpu.roll(lo, LANES - 16 * h, 1), hi)
    return xs


PROJ_RC = 256


def _qkv_kernel(a_ref, w_ref, c_ref, s1_ref, s2_ref, nat_ref, p4_ref, p16_ref, slab, *, tm, tn):
    j = pl.program_id(1)
    rep = tn // HEAD_DIM
    half = ROPE_DIM // 2
    is_qk = j < 2 * ATTN_WIDTH // tn
    scale = jnp.where(j < ATTN_WIDTH // tn, HEAD_DIM ** -0.5 * math.log2(math.e), 1.0).astype(F32)
    rc = PROJ_RC
    for ch in range(tm // rc):
        rows = slice(ch * rc, (ch + 1) * rc)
        acc = jnp.dot(a_ref[rows, :], w_ref[...], preferred_element_type=F32)
        c = jnp.concatenate([jnp.where(is_qk, c_ref[rows, :], 1.0) * scale] * rep, axis=1)
        s1 = jnp.concatenate([jnp.where(is_qk, s1_ref[rows, :], 0.0) * scale] * rep, axis=1)
        s2 = jnp.concatenate([jnp.where(is_qk, s2_ref[rows, :], 0.0) * scale] * rep, axis=1)
        val = acc * c + pltpu.roll(acc, tn - half, 1) * s1 + pltpu.roll(acc, half, 1) * s2
        for cc in range(rep):
            piece = val[:, cc * LANES:(cc + 1) * LANES]
            nat_ref[cc, rows, :] = piece.astype(nat_ref.dtype)
            slab[cc, rows, :] = piece
        for d, ref in ((4, p4_ref), (16, p16_ref)):
            nr = rc // d
            for r in range(d):
                for cc in range(rep):
                    ref[r, cc, ch * nr:(ch + 1) * nr, :] = (
                        slab[cc, pl.ds(ch * rc + r, nr, stride=d), :].astype(ref.dtype))


def _qkv_proj(xm, w_qkv, cos_t, sin1_t, sin2_t):
    s, d = xm.shape
    n = w_qkv.shape[1]
    tm, tn = PROJ_TM, PROJ_TN
    nh, hpt = n // HEAD_DIM, tn // HEAD_DIM
    tab = pl.BlockSpec((tm, LANES), lambda i, j: (i, 0))
    return pl.pallas_call(
        functools.partial(_qkv_kernel, tm=tm, tn=tn),
        grid=(s // tm, n // tn),
        in_specs=[pl.BlockSpec((tm, d), lambda i, j: (i, 0)),
                  pl.BlockSpec((d, tn), lambda i, j: (0, j)),
                  tab, tab, tab],
        out_specs=[pl.BlockSpec((hpt, tm, HEAD_DIM), lambda i, j: (j, i, 0)),
                   pl.BlockSpec((4, hpt, tm // 4, HEAD_DIM), lambda i, j: (0, j, i, 0)),
                   pl.BlockSpec((16, hpt, tm // 16, HEAD_DIM), lambda i, j: (0, j, i, 0))],
        out_shape=[jax.ShapeDtypeStruct((nh, s, HEAD_DIM), BF16),
                   jax.ShapeDtypeStruct((4, nh, s // 4, HEAD_DIM), BF16),
                   jax.ShapeDtypeStruct((16, nh, s // 16, HEAD_DIM), BF16)],
        scratch_shapes=[pltpu.VMEM((tn // LANES, tm, LANES), F32)],
        compiler_params=_params(("parallel", "arbitrary"), 52),
        name="qkv_proj",
    )(xm, w_qkv, cos_t, sin1_t, sin2_t)


def _uproj_kernel(a_ref, w_ref, u_ref, slab, *, tm, tn):
    rc = PROJ_RC
    nchunk = rc // SSM_CHUNK
    for ch in range(tm // rc):
        acc = jnp.dot(a_ref[ch * rc:(ch + 1) * rc, :], w_ref[...], preferred_element_type=F32)
        out_rows = slice(ch * nchunk, (ch + 1) * nchunk)
        for c in range(tn // LANES):
            slab[c, ch * rc:(ch + 1) * rc, :] = acc[:, c * LANES:(c + 1) * LANES]
            rows = [slab[c, pl.ds(ch * rc + i, nchunk, stride=SSM_CHUNK), :]
                    for i in range(SSM_CHUNK)]
            first = _piece_transpose8(rows[:8])
            second = _piece_transpose8(rows[8:])
            for gp in range(8):
                u_ref[c * 8 + gp, out_rows, 0:LANES] = first[gp].astype(u_ref.dtype)
                u_ref[c * 8 + gp, out_rows, LANES:2 * LANES] = second[gp].astype(u_ref.dtype)


def _u_proj(xm, w_u):
    s, d = xm.shape
    n = w_u.shape[1]
    tm, tn = PROJ_TM, PROJ_TN
    gpt = tn // SSM_GROUP
    return pl.pallas_call(
        functools.partial(_uproj_kernel, tm=tm, tn=tn),
        grid=(s // tm, n // tn),
        in_specs=[pl.BlockSpec((tm, d), lambda i, j: (i, 0)),
                  pl.BlockSpec((d, tn), lambda i, j: (0, j))],
        out_specs=pl.BlockSpec((gpt, tm // SSM_CHUNK, SSM_CHUNK * SSM_GROUP), lambda i, j: (j, i, 0)),
        out_shape=jax.ShapeDtypeStruct((n // SSM_GROUP, s // SSM_CHUNK, SSM_CHUNK * SSM_GROUP), BF16),
        scratch_shapes=[pltpu.VMEM((tn // LANES, tm, LANES), F32)],
        compiler_params=_params(("parallel", "arbitrary"), 48),
        name="u_proj",
    )(xm, w_u)


def _attn_kernel(q_ref, k_ref, v_ref, o_ref, l_ref, *, n):
    tq = LANES
    tk = 2 * LANES
    nsb = n // tq
    col_minus_row = lax.broadcasted_iota(I32, (tq, tk), 1) - lax.broadcasted_iota(I32, (tq, tk), 0)
    eye = lax.broadcasted_iota(I32, (tq, tq), 0) == lax.broadcasted_iota(I32, (tq, tq), 1)

    def band_bias(key_start_minus_q0):
        return jnp.where(jnp.abs(col_minus_row + key_start_minus_q0) <= ATTN_HALF, 0.0, NEG_INF).astype(F32)

    def block(sb, q0, ks, bias):
        q = q_ref[pl.ds(q0, tq), :]
        k = k_ref[pl.ds(ks, tk), :]
        v = v_ref[pl.ds(ks, tk), :]
        s = lax.dot_general(q, k, (((1,), (1,)), ((), ())), preferred_element_type=F32) + bias
        m = jnp.max(s, axis=-1, keepdims=True)
        p = jnp.exp2(s - m)
        l = jnp.sum(p, axis=-1, keepdims=True)
        o = jnp.dot(p.astype(BF16), v, preferred_element_type=F32) / l
        o_ref[pl.ds(q0, tq), :] = o.astype(o_ref.dtype)
        lse = (m + jnp.log2(l)) * math.log(2.0)
        l_ref[0, pl.ds(sb, 1), :] = jnp.sum(jnp.where(eye, lse, 0.0), axis=0, keepdims=True)

    block(0, 0, 0, band_bias(0))
    block(nsb - 1, n - tq, n - tk, band_bias(-tq))
    interior = nsb - 2
    if 0 < interior <= 6:
        bias = band_bias(-ATTN_HALF)
        for sb in range(1, nsb - 1):
            block(sb, sb * tq, sb * tq - ATTN_HALF, bias)
    elif interior > 0:
        unroll = max(u for u in (6, 5, 4, 3, 2, 1) if interior % u == 0)
        bias = band_bias(-ATTN_HALF)

        def body(it, carry):
            for u in range(unroll):
                sb = 1 + it * unroll + u
                q0 = pl.multiple_of(sb * tq, tq)
                block(sb, q0, pl.multiple_of(q0 - ATTN_HALF, ATTN_HALF), bias)
            return carry

        lax.fori_loop(0, interior // unroll, body, 0)


def _dilated_attention(qkv, dil):
    _, _, n, _ = qkv.shape
    s = n * dil
    nsb = n // LANES

    def spec(off):
        return pl.BlockSpec((None, None, n, HEAD_DIM), lambda r, h: (r, off + h, 0, 0))

    o, lse = pl.pallas_call(
        functools.partial(_attn_kernel, n=n),
        grid=(dil, N_HEADS),
        in_specs=[spec(0), spec(N_HEADS), spec(2 * N_HEADS)],
        out_specs=[pl.BlockSpec((None, None, n, HEAD_DIM), lambda r, h: (r, h, 0, 0)),
                   pl.BlockSpec((1, nsb, LANES), lambda r, h: (r * N_HEADS + h, 0, 0))],
        out_shape=[jax.ShapeDtypeStruct((dil, N_HEADS, n, HEAD_DIM), BF16),
                   jax.ShapeDtypeStruct((dil * N_HEADS, nsb, LANES), F32)],
        compiler_params=_params(("parallel", "parallel"), 48),
        name=f"dilated_attn_d{dil}",
    )(qkv, qkv, qkv)
    lse = lse.reshape(dil, N_HEADS, n).transpose(2, 0, 1).reshape(s, N_HEADS)
    return o, lse


def _merge_kernel(o1_ref, o4_ref, o16_ref, l1_ref, l2_ref, l3_ref, g_ref, out_ref, acc_ref, s4_ref, s16_ref, *, tm):
    for d, src, dst in ((4, o4_ref, s4_ref), (16, o16_ref, s16_ref)):
        for r in range(d):
            for h in range(N_HEADS):
                dst[h, pl.ds(r, tm // d, stride=d), :] = src[r, h].astype(F32)
    la, lb, lc = l1_ref[...], l2_ref[...], l3_ref[...]
    m = jnp.maximum(jnp.maximum(la, lb), lc)
    ea, eb, ec = jnp.exp(la - m), jnp.exp(lb - m), jnp.exp(lc - m)
    den = ea + eb + ec
    wa, wb, wc = ea / den, eb / den, ec / den
    sq = jnp.zeros((tm, 1), F32)
    for h in range(N_HEADS):
        cs = slice(h * HEAD_DIM, (h + 1) * HEAD_DIM)
        slab = (wa[:, h:h + 1] * o1_ref[h].astype(F32)
                + wb[:, h:h + 1] * s4_ref[h]
                + wc[:, h:h + 1] * s16_ref[h])
        acc_ref[:, cs] = slab
        sq = sq + jnp.sum(slab * slab, axis=-1, keepdims=True)
    inv = lax.rsqrt(sq * (1.0 / ATTN_WIDTH) + EPS)
    out_ref[...] = (acc_ref[...] * inv * g_ref[...]).astype(out_ref.dtype)


def _merge_patterns(outs, lses, gain):
    o1, o4, o16 = outs
    s = o1.shape[2]
    tm = 256
    ospec = pl.BlockSpec((tm, ATTN_WIDTH), lambda i: (i, 0))
    lspec = pl.BlockSpec((tm, N_HEADS), lambda i: (i, 0))
    return pl.pallas_call(
        functools.partial(_merge_kernel, tm=tm),
        grid=(s // tm,),
        in_specs=[pl.BlockSpec((None, N_HEADS, tm, HEAD_DIM), lambda i: (0, 0, i, 0)),
                  pl.BlockSpec((4, N_HEADS, tm // 4, HEAD_DIM), lambda i: (0, 0, i, 0)),
                  pl.BlockSpec((16, N_HEADS, tm // 16, HEAD_DIM), lambda i: (0, 0, i, 0)),
                  lspec, lspec, lspec,
                  pl.BlockSpec((1, ATTN_WIDTH), lambda i: (0, 0))],
        out_specs=ospec,
        out_shape=jax.ShapeDtypeStruct((s, ATTN_WIDTH), BF16),
        scratch_shapes=[pltpu.VMEM((tm, ATTN_WIDTH), F32),
                        pltpu.VMEM((N_HEADS, tm, HEAD_DIM), F32),
                        pltpu.VMEM((N_HEADS, tm, HEAD_DIM), F32)],
        compiler_params=_params(("parallel",), 32),
        name="attn_merge_norm",
    )(o1, o4, o16, *lses, gain)


def _ssm_direction_terms(lam_re, lam_im, log_dt, b_re, b_im, c_re, c_im):
    dt = jnp.exp(log_dt)[:, None]
    mag = jnp.exp(lam_re * dt)
    ang = lam_im * dt
    lb_re = mag * jnp.cos(ang)
    lb_im = mag * jnp.sin(ang)
    den = lam_re * lam_re + lam_im * lam_im
    ar = lb_re - 1.0
    ai = lb_im
    coef_re = (ar * lam_re + ai * lam_im) / den
    coef_im = (ai * lam_re - ar * lam_im) / den
    bb_re = coef_re[..., None] * b_re - coef_im[..., None] * b_im
    bb_im = coef_re[..., None] * b_im + coef_im[..., None] * b_re
    tau = jnp.arange(SSM_CHUNK + 1, dtype=F32)[:, None, None]
    pw_mag = jnp.exp(tau * (lam_re * dt)[None])
    pw_re = pw_mag * jnp.cos(tau * ang[None])
    pw_im = pw_mag * jnp.sin(tau * ang[None])
    e_re = pw_re[..., None] * bb_re[None] - pw_im[..., None] * bb_im[None]
    e_im = pw_re[..., None] * bb_im[None] + pw_im[..., None] * bb_re[None]
    kern = (jnp.einsum('ghp,tgpi->tghi', c_re, e_re[:SSM_CHUNK], precision=HIGHEST)
            - jnp.einsum('ghp,tgpi->tghi', c_im, e_im[:SSM_CHUNK], precision=HIGHEST))
    ca_re = c_re[None] * pw_re[:, :, None, :] - c_im[None] * pw_im[:, :, None, :]
    ca_im = c_re[None] * pw_im[:, :, None, :] + c_im[None] * pw_re[:, :, None, :]
    return kern, e_re, e_im, ca_re, ca_im, pw_re[SSM_CHUNK], pw_im[SSM_CHUNK]


def _ssm_operators(lam_re_f, lam_im_f, log_dt_f, lam_re_b, lam_im_b, log_dt_b, b_re, b_im, c_re, c_im):
    L, G, P = SSM_CHUNK, SSM_GROUPS, SSM_STATE
    kf, ef_re, ef_im, caf_re, caf_im, af_re, af_im = _ssm_direction_terms(
        lam_re_f, lam_im_f, log_dt_f, b_re, b_im, c_re, c_im)
    kb, eb_re, eb_im, cab_re, cab_im, ab_re, ab_im = _ssm_direction_terms(
        lam_re_b, lam_im_b, log_dt_b, b_re, b_im, c_re, c_im)
    lag = jnp.arange(L)[None, :] - jnp.arange(L)[:, None]
    tau = jnp.arange(L)[:, None, None]
    place = jnp.concatenate([(lag[None] == tau), (-lag[None] == tau)], axis=0).astype(F32)
    w = jnp.einsum('tij,tgoh->gihjo', place, jnp.concatenate([kf, kb], axis=0), precision=HIGHEST)
    w = w.reshape(G, L * SSM_GROUP, L * SSM_GROUP)

    def p_mat(e, order):
        return jnp.transpose(e[order], (1, 0, 3, 2)).reshape(G, L * SSM_GROUP, P)

    def q_mat(ca, order):
        return jnp.transpose(ca[order], (1, 3, 0, 2)).reshape(G, P, L * SSM_GROUP)

    f_ord = (L - 1) - jnp.arange(L)
    b_ord = jnp.arange(L)
    parts_p = [p_mat(ef_re, f_ord), p_mat(ef_im, f_ord), p_mat(eb_re, b_ord), p_mat(eb_im, b_ord)]
    parts_q = [q_mat(caf_re, jnp.arange(L) + 1), -q_mat(caf_im, jnp.arange(L) + 1),
               q_mat(cab_re, L - jnp.arange(L)), -q_mat(cab_im, L - jnp.arange(L))]
    even = (jnp.arange(G) % 2 == 0)[:, None, None]

    def pad_p(x):
        z = jnp.zeros_like(x)
        return jnp.where(even, jnp.concatenate([x, z], -1), jnp.concatenate([z, x], -1))

    def pad_q(x):
        z = jnp.zeros_like(x)
        return jnp.where(even, jnp.concatenate([x, z], 1), jnp.concatenate([z, x], 1))

    p_all = jnp.stack([pad_p(x) for x in parts_p], axis=1).astype(BF16)
    q_all = jnp.stack([pad_q(x) for x in parts_q], axis=1).astype(BF16)

    def tiles(re, im):
        re = re.reshape(G // SSM_GB, SSM_GB // 2, 2 * P)
        im = im.reshape(G // SSM_GB, SSM_GB // 2, 2 * P)
        return jnp.concatenate([re, re], axis=1), jnp.concatenate([-im, im], axis=1)

    a_all = jnp.stack([*tiles(af_re, af_im), *tiles(ab_re, ab_im)], axis=1)
    return w.astype(BF16), p_all, q_all, a_all


def _gelu_tanh(x):
    return 0.5 * x * (1.0 + jnp.tanh(math.sqrt(2.0 / math.pi) * (x + 0.044715 * (x * x * x))))


SUBLANES = 8


def _ssm_kernel(u_ref, w_ref, p_ref, q_ref, a_ref, d_ref, y_ref, sf, sb, *, nc, gb):
    npair = gb // 2
    assert 2 * npair == SUBLANES
    for pr in range(npair):
        g0, g1 = 2 * pr, 2 * pr + 1
        for k, (scr, row) in enumerate(((sf, pr), (sf, npair + pr), (sb, pr), (sb, npair + pr))):
            contrib = (jnp.dot(u_ref[g0], p_ref[g0, k], preferred_element_type=F32)
                       + jnp.dot(u_ref[g1], p_ref[g1, k], preferred_element_type=F32))
            scr[pl.ds(row, nc, stride=SUBLANES), :] = contrib

    a1f, a2f, a1b, a2b = a_ref[0, 0], a_ref[0, 1], a_ref[0, 2], a_ref[0, 3]

    def step(c, carry):
        hf, hb = carry
        cf = pl.multiple_of(c * SUBLANES, SUBLANES)
        cb = pl.multiple_of((nc - 1 - c) * SUBLANES, SUBLANES)
        rf = sf[pl.ds(cf, SUBLANES), :]
        rb = sb[pl.ds(cb, SUBLANES), :]
        sf[pl.ds(cf, SUBLANES), :] = hf
        sb[pl.ds(cb, SUBLANES), :] = hb
        return (a1f * hf + a2f * pltpu.roll(hf, npair, 0) + rf,
                a1b * hb + a2b * pltpu.roll(hb, npair, 0) + rb)

    z = jnp.zeros((SUBLANES, LANES), F32)
    lax.fori_loop(0, nc, step, (z, z))

    for pr in range(npair):
        states = [scr[pl.ds(row, nc, stride=SUBLANES), :].astype(BF16)
                  for scr, row in ((sf, pr), (sf, npair + pr), (sb, pr), (sb, npair + pr))]
        for g in (2 * pr, 2 * pr + 1):
            u = u_ref[g]
            y = jnp.dot(u, w_ref[g], preferred_element_type=F32)
            for k in range(4):
                y = y + jnp.dot(states[k], q_ref[g, k], preferred_element_type=F32)
            y = y + d_ref[g] * u.astype(F32)
            y_ref[g] = _gelu_tanh(y).astype(y_ref.dtype)


def _ssm_mixer(u_t, w, p_all, q_all, a_all, d_t):
    g, nc, k = u_t.shape
    gb = SSM_GB
    return pl.pallas_call(
        functools.partial(_ssm_kernel, nc=nc, gb=gb),
        grid=(g // gb,),
        in_specs=[pl.BlockSpec((gb, nc, k), lambda i: (i, 0, 0)),
                  pl.BlockSpec((gb, k, k), lambda i: (i, 0, 0)),
                  pl.BlockSpec((gb, 4, k, LANES), lambda i: (i, 0, 0, 0)),
                  pl.BlockSpec((gb, 4, LANES, k), lambda i: (i, 0, 0, 0)),
                  pl.BlockSpec((1, 4, SUBLANES, LANES), lambda i: (i, 0, 0, 0)),
                  pl.BlockSpec((gb, 1, k), lambda i: (i, 0, 0))],
        out_specs=pl.BlockSpec((gb, nc, k), lambda i: (i, 0, 0)),
        out_shape=jax.ShapeDtypeStruct((g, nc, k), BF16),
        scratch_shapes=[pltpu.VMEM((nc * SUBLANES, LANES), F32)] * 2,
        compiler_params=_params(("parallel",), 48),
        name="ssm_scan",
    )(u_t, w, p_all, q_all, a_all, d_t)


def _glu_kernel(g_ref, w_ref, b_ref, n_ref, o_ref, nat, gb, *, tm):
    nchunk = tm // SSM_CHUNK
    for sl in range(SSM_WIDTH // LANES):
        for half in range(2):
            xs = [g_ref[8 * sl + gp, :, half * LANES:(half + 1) * LANES].astype(F32) for gp in range(8)]
            ys = _piece_transpose8(xs)
            for jj in range(8):
                nat[sl, pl.ds(half * 8 + jj, nchunk, stride=SSM_CHUNK), :] = ys[jj]
        gb[:, sl * LANES:(sl + 1) * LANES] = nat[sl].astype(BF16)
    z = jnp.dot(gb[...], w_ref[...], preferred_element_type=F32) + b_ref[...]
    sq = jnp.zeros((tm, 1), F32)
    for sl in range(SSM_WIDTH // LANES):
        cs = slice(sl * LANES, (sl + 1) * LANES)
        out = nat[sl] * jax.nn.sigmoid(z[:, cs])
        nat[sl] = out
        sq = sq + jnp.sum(out * out, axis=-1, keepdims=True)
    inv = lax.rsqrt(sq * (1.0 / SSM_WIDTH) + EPS)
    for sl in range(SSM_WIDTH // LANES):
        cs = slice(sl * LANES, (sl + 1) * LANES)
        o_ref[:, cs] = (nat[sl] * inv * n_ref[:, cs]).astype(o_ref.dtype)


def _glu_norm(g_t, w_glu, b_glu, gain):
    ng, nc, k = g_t.shape
    s, d = nc * SSM_CHUNK, ng * SSM_GROUP
    tm = 512
    row = pl.BlockSpec((1, d), lambda i: (0, 0))
    return pl.pallas_call(
        functools.partial(_glu_kernel, tm=tm),
        grid=(s // tm,),
        in_specs=[pl.BlockSpec((ng, tm // SSM_CHUNK, k), lambda i: (0, i, 0)),
                  pl.BlockSpec((d, d), lambda i: (0, 0)), row, row],
        out_specs=pl.BlockSpec((tm, d), lambda i: (i, 0)),
        out_shape=jax.ShapeDtypeStruct((s, d), BF16),
        scratch_shapes=[pltpu.VMEM((d // LANES, tm, LANES), F32),
                        pltpu.VMEM((tm, d), BF16)],
        compiler_params=_params(("parallel",), 48),
        name="ssm_glu_norm",
    )(g_t, w_glu, b_glu, gain)


def _outproj_kernel(a1_ref, a2_ref, w1_ref, w2_ref, x_ref, gm_ref, o_ref):
    mix = (jnp.dot(a1_ref[...], w1_ref[...], preferred_element_type=F32)
           + jnp.dot(a2_ref[...], w2_ref[...], preferred_element_type=F32))
    o_ref[...] = x_ref[...] + gm_ref[...] * mix


def _out_proj(attn_n, ssm_n, w_out, x, g_m):
    s, k = attn_n.shape
    n = w_out.shape[1]
    tm, tn = 1024, 512
    return pl.pallas_call(
        _outproj_kernel,
        grid=(s // tm, n // tn),
        in_specs=[pl.BlockSpec((tm, k), lambda i, j: (i, 0)),
                  pl.BlockSpec((tm, k), lambda i, j: (i, 0)),
                  pl.BlockSpec((k, tn), lambda i, j: (0, j)),
                  pl.BlockSpec((k, tn), lambda i, j: (1, j)),
                  pl.BlockSpec((tm, tn), lambda i, j: (i, j)),
                  pl.BlockSpec((1, tn), lambda i, j: (0, j))],
        out_specs=pl.BlockSpec((tm, tn), lambda i, j: (i, j)),
        out_shape=jax.ShapeDtypeStruct((s, n), F32),
        compiler_params=_params(("parallel", "arbitrary"), 48),
        name="out_proj_residual",
    )(attn_n, ssm_n, w_out, w_out, x, g_m)


def _ffn_prep_kernel(h_ref, g_ref, sc_ref, sh_ref, wr_ref, xf_ref, lg_ref):
    xf = _norm_mod(h_ref[...], g_ref[...], sc_ref[...], sh_ref[...])
    half = xf.shape[1] // 2
    lo = pltpu.bitcast(xf[:, :half].astype(BF16).astype(F32), jnp.uint32)
    hi = pltpu.bitcast(xf[:, half:].astype(BF16).astype(F32), jnp.uint32)
    xf_ref[...] = (lo >> 16) | hi
    lg_ref[...] = lax.dot_general(wr_ref[...], xf, (((1,), (1,)), ((), ())),
                                  precision=HIGHEST, preferred_element_type=F32)


def _ffn_prep(h1, gain, scale, shift, w_router_t):
    s, d = h1.shape
    tm = 256
    row = pl.BlockSpec((1, d), lambda i: (0, 0))
    return pl.pallas_call(
        _ffn_prep_kernel,
        grid=(s // tm,),
        in_specs=[pl.BlockSpec((tm, d), lambda i: (i, 0)), row, row, row,
                  pl.BlockSpec((N_EXPERTS, d), lambda i: (0, 0))],
        out_specs=[pl.BlockSpec((tm, d // 2), lambda i: (i, 0)),
                   pl.BlockSpec((N_EXPERTS, tm), lambda i: (0, i))],
        out_shape=[jax.ShapeDtypeStruct((s, d // 2), jnp.uint32),
                   jax.ShapeDtypeStruct((N_EXPERTS, s), F32)],
        compiler_params=_params(("parallel",), 40),
        name="ffn_norm_router",
    )(h1, gain, scale, shift, w_router_t)


def _route_kernel(lg_ref, sel_ref, pos_ref, aff_ref, *, cap, nk):
    e_n = N_EXPERTS
    lg = lg_ref[...]
    mx = jnp.max(lg, axis=0, keepdims=True)
    ex = jnp.exp(lg - mx)
    aff = ex / jnp.sum(ex, axis=0, keepdims=True)
    aff_ref[...] = aff
    bits = pltpu.bitcast(aff, I32)

    def count(mask):
        return jnp.sum(jnp.sum(mask.astype(F32), axis=2, keepdims=True), axis=1, keepdims=True)

    def radix(i, thr):
        cand = thr | jnp.left_shift(jnp.int32(1), 30 - i)
        return jnp.where(count(bits >= cand) >= cap, cand, thr)

    thr = lax.fori_loop(0, 31, radix, jnp.zeros((e_n, 1, 1), I32))
    gt = bits > thr
    eq = bits == thr
    need = cap - count(gt)

    li = lax.broadcasted_iota(I32, (LANES, LANES), 0)
    lj = lax.broadcasted_iota(I32, (LANES, LANES), 1)
    tri_lane = (li < lj).astype(BF16)
    ki = lax.broadcasted_iota(I32, (nk, nk), 0)
    kj = lax.broadcasted_iota(I32, (nk, nk), 1)
    tri_row = (kj < ki).astype(BF16)

    def excl_prefix(mask):
        mf = mask.astype(F32)
        within = jnp.dot(mf.reshape(e_n * nk, LANES).astype(BF16), tri_lane,
                         preferred_element_type=F32).reshape(e_n, nk, LANES)
        rowtot = jnp.broadcast_to(jnp.sum(mf, axis=2, keepdims=True), (e_n, nk, LANES)).astype(BF16)
        across = jnp.stack([jnp.dot(tri_row, rowtot[e], preferred_element_type=F32) for e in range(e_n)])
        return within + across

    sel = gt | (eq & (excl_prefix(eq) < need))
    sel_ref[...] = sel.astype(I32)
    pos_ref[...] = excl_prefix(sel).astype(I32)


def _route(logits_t, cap):
    e_n, s = logits_t.shape
    nk = s // LANES
    spec = pl.BlockSpec((e_n, nk, LANES), lambda i: (0, 0, 0))
    sel, pos, aff = pl.pallas_call(
        functools.partial(_route_kernel, cap=cap, nk=nk),
        grid=(1,),
        in_specs=[spec],
        out_specs=[spec, spec, spec],
        out_shape=[jax.ShapeDtypeStruct((e_n, nk, LANES), I32),
                   jax.ShapeDtypeStruct((e_n, nk, LANES), I32),
                   jax.ShapeDtypeStruct((e_n, nk, LANES), F32)],
        compiler_params=_params(("arbitrary",), 48),
        name="expert_choice_route",
    )(logits_t.reshape(e_n, nk, LANES))
    return sel.reshape(e_n, s), pos.reshape(e_n, s), aff.reshape(e_n, s)


def _compact_kernel(off_ref, sel_ref, pos_ref, aff_ref, idx_ref, gate_ref, *, nk):
    e = pl.program_id(0)
    idx_ref[...] = jnp.zeros(idx_ref.shape, I32)
    gate_ref[...] = jnp.zeros(gate_ref.shape, F32)
    local = lax.broadcasted_iota(I32, (LANES, LANES), 0)
    lane = lax.broadcasted_iota(I32, (LANES, LANES), 1)

    unroll = 4 if nk % 4 == 0 else 1

    def body(it, carry):
        for u in range(unroll):
            k = it * unroll + u
            off = off_ref[e, k]
            sel = sel_ref[0, pl.ds(k, 1), :]
            pos = pos_ref[0, pl.ds(k, 1), :]
            aff = aff_ref[0, pl.ds(k, 1), :]
            hit = (sel > 0) & ((pos - off) == local)
            tok = jnp.sum(jnp.where(hit, (lane + k * LANES).astype(F32), 0.0), axis=1, keepdims=True)
            gat = jnp.sum(jnp.where(hit, aff, 0.0), axis=1, keepdims=True)
            idx_ref[0, pl.ds(off, LANES), :] = tok.astype(I32)
            gate_ref[0, pl.ds(off, LANES), :] = gat
        return carry

    lax.fori_loop(0, nk // unroll, body, 0)


def _compact(sel, pos, aff, cap):
    e_n, s = sel.shape
    nk = s // LANES
    capp = cap + LANES
    offs = pos[:, ::LANES]
    tok = pl.BlockSpec((1, nk, LANES), lambda e, off: (e, 0, 0))
    col = pl.BlockSpec((1, capp, 1), lambda e, off: (e, 0, 0))
    grid_spec = pltpu.PrefetchScalarGridSpec(
        num_scalar_prefetch=1, grid=(e_n,), in_specs=[tok, tok, tok], out_specs=[col, col])
    idx, gate = pl.pallas_call(
        functools.partial(_compact_kernel, nk=nk),
        grid_spec=grid_spec,
        out_shape=[jax.ShapeDtypeStruct((e_n, capp, 1), I32),
                   jax.ShapeDtypeStruct((e_n, capp, 1), F32)],
        compiler_params=_params(("arbitrary",), 32),
        name="expert_compact",
    )(offs, sel.reshape(e_n, nk, LANES), pos.reshape(e_n, nk, LANES), aff.reshape(e_n, nk, LANES))
    return idx[:, :cap, 0], gate[:, :cap, :]


def _moe_up_kernel(idx_hbm, xf_hbm, wg_ref, wu_ref, h_ref, xbuf, idx_s, sem_x, sem_i, *, cap, tm, rps):
    nf, nh = pl.num_programs(1), pl.num_programs(2)
    phase = pl.program_id(0)
    n_phase = pl.num_programs(0)
    f = pl.program_id(1) * nh + pl.program_id(2)
    nf = nf * nh
    step = phase * nf + f
    b = phase % 2
    row0 = pl.multiple_of(pl.program_id(2) * tm, tm)

    def row_copy(token, buf, r):
        return pltpu.make_async_copy(xf_hbm.at[pl.ds(token, 1), :], xbuf.at[buf, pl.ds(r, 1), :], sem_x.at[buf])

    def wait_rows(buf, n):
        def one(r, c):
            row_copy(0, buf, r).wait()
            return c
        lax.fori_loop(0, n, one, 0)

    @pl.when(step == 0)
    def _prologue():
        c_idx = pltpu.make_async_copy(idx_hbm, idx_s, sem_i.at[0])
        c_idx.start()
        c_idx.wait()

        def one(r, c):
            row_copy(idx_s[r], 0, r).start()
            return c
        lax.fori_loop(0, cap, one, 0)
        wait_rows(0, cap)

    @pl.when(step > 0)
    def _():
        wait_rows(jnp.where(f == 0, b, 1 - b), rps)

    nxt = jnp.minimum(phase + 1, n_phase - 1)
    base = nxt * cap + f * rps
    for r in range(rps):
        row_copy(idx_s[base + r], 1 - b, f * rps + r).start()

    wg = wg_ref[0].astype(BF16)
    wu = wu_ref[0].astype(BF16)
    half = wg.shape[0] // 2
    rc = MOE_RC
    for ch in range(tm // rc):
        x = xbuf[b, pl.ds(row0 + ch * rc, rc), :]
        x_lo = pltpu.bitcast(x << 16, F32).astype(BF16)
        x_hi = pltpu.bitcast(x & jnp.uint32(0xFFFF0000), F32).astype(BF16)
        g = (jnp.dot(x_lo, wg[:half], preferred_element_type=F32)
             + jnp.dot(x_hi, wg[half:], preferred_element_type=F32))
        u = (jnp.dot(x_lo, wu[:half], preferred_element_type=F32)
             + jnp.dot(x_hi, wu[half:], preferred_element_type=F32))
        h_ref[0, ch * rc:(ch + 1) * rc, :] = (g * jax.nn.sigmoid(g) * u).astype(h_ref.dtype)

    @pl.when(step == n_phase * nf - 1)
    def _drain():
        wait_rows(1 - b, rps)


MOE_RC = 256


def _moe_up(idx, xf_packed, w_gate, w_up, cap):
    e_n, d, ff = w_gate.shape
    tf = 256
    tm = min(1024, cap)
    nf, nh = ff // tf, cap // tm
    rps = cap // (nf * nh)
    assert rps * nf * nh == cap and tm % MOE_RC == 0
    return pl.pallas_call(
        functools.partial(_moe_up_kernel, cap=cap, tm=tm, rps=rps),
        grid=(e_n, nf, nh),
        in_specs=[pl.BlockSpec(memory_space=pl.ANY),
                  pl.BlockSpec(memory_space=pl.ANY),
                  pl.BlockSpec((1, d, tf), lambda e, f, m: (e, 0, f)),
                  pl.BlockSpec((1, d, tf), lambda e, f, m: (e, 0, f))],
        out_specs=pl.BlockSpec((1, tm, tf), lambda e, f, m: (e, m, f)),
        out_shape=jax.ShapeDtypeStruct((e_n, cap, ff), BF16),
        scratch_shapes=[pltpu.VMEM((2, cap, d // 2), jnp.uint32),
                        pltpu.SMEM((e_n * cap,), I32),
                        pltpu.SemaphoreType.DMA((2,)),
                        pltpu.SemaphoreType.DMA((1,))],
        compiler_params=_params(("arbitrary", "arbitrary", "arbitrary"), 60),
        name="moe_gather_up",
    )(idx.reshape(e_n * cap), xf_packed, w_gate, w_up)


def _moe_down_kernel(h_ref, wd_ref, gate_ref, y_ref):
    y = jnp.dot(h_ref[0], wd_ref[0].astype(BF16), preferred_element_type=F32)
    y_ref[0] = (y * gate_ref[0]).astype(y_ref.dtype)


def _moe_down(hdn, w_down, gates):
    e_n, cap, ff = hdn.shape
    d = w_down.shape[2]
    tn = 512
    return pl.pallas_call(
        _moe_down_kernel,
        grid=(e_n, d // tn),
        in_specs=[pl.BlockSpec((1, cap, ff), lambda e, n: (e, 0, 0)),
                  pl.BlockSpec((1, ff, tn), lambda e, n: (e, 0, n)),
                  pl.BlockSpec((1, cap, 1), lambda e, n: (e, 0, 0))],
        out_specs=pl.BlockSpec((1, cap, tn), lambda e, n: (e, 0, n)),
        out_shape=jax.ShapeDtypeStruct((e_n, cap, d), BF16),
        compiler_params=_params(("parallel", "arbitrary"), 48),
        name="moe_down",
    )(hdn, w_down, gates)


COMBINE_WINDOW = 64
SLOT_ALIGN = 16


def _combine_kernel(lo_ref, h_ref, sel_ref, pos_ref, gf_ref, gn_ref, y_hbm, o_ref,
                    ybuf, yextra, acc_ref, sem, sem_x, *, cap, tm):
    i = pl.program_id(0)
    win = COMBINE_WINDOW
    e_n = N_EXPERTS

    n_steps = pl.num_programs(0)
    cur = i % 2

    def window(e, j, step=None):
        step = i if step is None else step
        nominal = (lo_ref[e, step] // SLOT_ALIGN) * SLOT_ALIGN + j * win
        start = pl.multiple_of(jnp.minimum(nominal, cap - win), SLOT_ALIGN)
        return nominal, start

    def first_copy(e, step, buf):
        _, start = window(e, 0, step)
        return pltpu.make_async_copy(y_hbm.at[e, pl.ds(start, win), :],
                                     ybuf.at[buf, pl.ds(e * win, win), :], sem.at[buf, e])

    @pl.when(i == 0)
    def _():
        for e in range(e_n):
            first_copy(e, 0, 0).start()

    @pl.when(i + 1 < n_steps)
    def _():
        for e in range(e_n):
            first_copy(e, i + 1, 1 - cur).start()

    sel = sel_ref[...]
    pos = pos_ref[...]
    lane = lax.broadcasted_iota(I32, (tm, LANES), 1)
    left = lane < win
    pieces = []
    for e2 in range(e_n // 2):
        ea, eb = 2 * e2, 2 * e2 + 1
        na, sa = window(ea, 0)
        nb, sb = window(eb, 0)
        slot = jnp.where(left, sa + lane, sb + lane - win)
        nominal = jnp.where(left, na, nb)
        pe = jnp.where(left, pos[:, ea:ea + 1], pos[:, eb:eb + 1])
        se = jnp.where(left, sel[:, ea:ea + 1], sel[:, eb:eb + 1])
        pieces.append(((pe == slot) & (se > 0) & (slot >= nominal)).astype(BF16))
    onehot = jnp.concatenate(pieces, axis=1)
    for e in range(e_n):
        first_copy(e, i, cur).wait()
    acc_ref[...] = jnp.dot(onehot, ybuf[cur], preferred_element_type=F32)

    lane_w = lax.broadcasted_iota(I32, (tm, win), 1)
    for e in range(e_n):
        base = (lo_ref[e, i] // SLOT_ALIGN) * SLOT_ALIGN
        nwin = (lo_ref[e, i + 1] - base + win - 1) // win

        def extra(j, c, e=e):
            nominal, start = window(e, j)
            cp = pltpu.make_async_copy(y_hbm.at[e, pl.ds(start, win), :], yextra, sem_x.at[0])
            cp.start()
            cp.wait()
            slot = start + lane_w
            oh = ((pos[:, e:e + 1] == slot) & (sel[:, e:e + 1] > 0) & (slot >= nominal)).astype(BF16)
            acc_ref[...] += jnp.dot(oh, yextra[...], preferred_element_type=F32)
            return c

        lax.fori_loop(1, nwin, extra, 0)

    h2 = h_ref[...] + gf_ref[...] * acc_ref[...]
    ms = jnp.mean(h2 * h2, axis=-1, keepdims=True)
    o_ref[...] = h2 * lax.rsqrt(ms + EPS) * gn_ref[...]


def _combine(h1, y, sel_tm, pos_tm, lo, g_f, gain, cap):
    s, d = h1.shape
    tm = 256
    win = COMBINE_WINDOW
    row = pl.BlockSpec((1, d), lambda i, lo_r: (0, 0))
    tok = pl.BlockSpec((tm, N_EXPERTS), lambda i, lo_r: (i, 0))
    grid_spec = pltpu.PrefetchScalarGridSpec(
        num_scalar_prefetch=1,
        grid=(s // tm,),
        in_specs=[pl.BlockSpec((tm, d), lambda i, lo_r: (i, 0)), tok, tok, row, row,
                  pl.BlockSpec(memory_space=pl.ANY)],
        out_specs=pl.BlockSpec((tm, d), lambda i, lo_r: (i, 0)),
        scratch_shapes=[pltpu.VMEM((2, N_EXPERTS * win, d), BF16),
                        pltpu.VMEM((win, d), BF16),
                        pltpu.VMEM((tm, d), F32),
                        pltpu.SemaphoreType.DMA((2, N_EXPERTS)),
                        pltpu.SemaphoreType.DMA((1,))],
    )
    return pl.pallas_call(
        functools.partial(_combine_kernel, cap=cap, tm=tm),
        grid_spec=grid_spec,
        out_shape=jax.ShapeDtypeStruct((s, d), F32),
        compiler_params=_params(("arbitrary",), 56),
        name="moe_combine_norm",
    )(lo, h1, sel_tm, pos_tm, g_f, gain, y)


def kernel(x, c, positions, w_ada, b_ada, norm_mix_gain, w_in, lam_re_fwd, lam_im_fwd, log_dt_fwd, lam_re_bwd, lam_im_bwd, log_dt_bwd, ssm_b_re, ssm_b_im, ssm_c_re, ssm_c_im, ssm_d, w_glu, b_glu, norm_attn_out_gain, norm_ssm_out_gain, w_out, norm_ffn_gain, w_router, w_exp_gate, w_exp_up, w_exp_down, norm_final_gain):
    batch, s, d = x.shape
    depth = w_ada.shape[0]
    assert batch == 1 and d == D_MODEL and depth == 1
    cap = max(1, CAPACITY_FACTOR * s // N_EXPERTS)
    h = x[0]
    pos = positions[0]
    cos_t, sin1_t, sin2_t = _rope_tables(pos)
    for layer in range(depth):
        mod = _adaln(c, w_ada[layer], b_ada[layer])
        sh_m, sc_m, g_m, sh_f, sc_f, g_f = jnp.split(mod, 6, axis=-1)

        xm = _norm_mod_call(h, norm_mix_gain[layer][None], sc_m, sh_m, BF16)
        qkv, qkv4, qkv16 = _qkv_proj(xm, w_in[layer, :, :3 * ATTN_WIDTH].astype(BF16), cos_t, sin1_t, sin2_t)
        outs, lses = [], []
        for dil, arr in zip(DILATIONS, (qkv[None], qkv4, qkv16)):
            o, l = _dilated_attention(arr, dil)
            outs.append(o)
            lses.append(l)
        attn_n = _merge_patterns(outs, lses, norm_attn_out_gain[layer][None])

        u_t = _u_proj(xm, w_in[layer, :, 3 * ATTN_WIDTH:].astype(BF16))
        w_t, p_all, q_all, a_all = _ssm_operators(
            lam_re_fwd[layer], lam_im_fwd[layer], log_dt_fwd[layer],
            lam_re_bwd[layer], lam_im_bwd[layer], log_dt_bwd[layer],
            ssm_b_re[layer], ssm_b_im[layer], ssm_c_re[layer], ssm_c_im[layer])
        d_t = jnp.tile(ssm_d[layer].reshape(SSM_GROUPS, 1, SSM_GROUP), (1, 1, SSM_CHUNK))
        g_t = _ssm_mixer(u_t, w_t, p_all, q_all, a_all, d_t)
        ssm_n = _glu_norm(g_t, w_glu[layer].astype(BF16), b_glu[layer][None], norm_ssm_out_gain[layer][None])

        h = _out_proj(attn_n, ssm_n, w_out[layer].astype(BF16), h, g_m)

        xf, logits_t = _ffn_prep(h, norm_ffn_gain[layer][None], sc_f, sh_f, jnp.transpose(w_router[layer]))
        sel, slot, aff = _route(logits_t, cap)
        idx, gates = _compact(sel, slot, aff, cap)
        hdn = _moe_up(idx, xf, w_exp_gate[layer], w_exp_up[layer], cap)
        y = _moe_down(hdn, w_exp_down[layer], gates)
        tm = 256
        lo = jnp.concatenate([slot[:, ::tm], jnp.full((N_EXPERTS, 1), cap, I32)], axis=1)
        h = _combine(h, y, jnp.transpose(sel), jnp.transpose(slot), lo, g_f, norm_final_gain[None], cap)
    return h[None]
```

```python
import functools
import math

import jax
import jax.numpy as jnp
from jax import lax
from jax.experimental import pallas as pl
from jax.experimental.pallas import tpu as pltpu

F32 = jnp.float32
BF16 = jnp.bfloat16
I32 = jnp.int32
HIGHEST = lax.Precision.HIGHEST

D_MODEL = 4096
ATTN_WIDTH = 2048
SSM_WIDTH = 2048
HEAD_DIM = 128
N_HEADS = 16
IN_PROJ_WIDTH = 3 * ATTN_WIDTH + SSM_WIDTH
ROPE_DIM = 32
ROPE_THETA = 500000.0
DILATIONS = (1, 4, 16)
ATTN_HALF = 64
ATTN_UNROLL = 15
NEG_INF = -1e30
SSM_GROUP = 16
SSM_GROUPS = 128
SSM_STATE = 64
SSM_CHUNK = 16
SSM_GB = 8
N_EXPERTS = 16
EXPERT_FF = 2048
CAPACITY_FACTOR = 2
EPS = 1e-6

LANES = 128
MIB = 1024 * 1024


def _params(semantics, vmem_mib):
    return pltpu.CompilerParams(dimension_semantics=semantics, vmem_limit_bytes=vmem_mib * MIB)


def _ada_kernel(c_ref, w_ref, b_ref, o_ref):
    c = c_ref[...]
    ca = c * jax.nn.sigmoid(c)
    o_ref[...] = jnp.dot(ca, w_ref[...], precision=HIGHEST, preferred_element_type=F32) + b_ref[...]


def _adaln(c, w_ada, b_ada):
    d, n = w_ada.shape
    tn = 512
    c8 = jnp.pad(c, ((0, 7), (0, 0)))
    out = pl.pallas_call(
        _ada_kernel,
        grid=(n // tn,),
        in_specs=[pl.BlockSpec((8, d), lambda j: (0, 0)),
                  pl.BlockSpec((d, tn), lambda j: (0, j)),
                  pl.BlockSpec((1, tn), lambda j: (0, j))],
        out_specs=pl.BlockSpec((8, tn), lambda j: (0, j)),
        out_shape=jax.ShapeDtypeStruct((8, n), F32),
        compiler_params=_params(("parallel",), 40),
        name="adaln",
    )(c8, w_ada, b_ada.reshape(1, n))
    return out[0:1]


def _norm_mod(x, gain, scale, shift):
    ms = jnp.mean(x * x, axis=-1, keepdims=True)
    y = x * lax.rsqrt(ms + EPS) * gain
    return y * (1.0 + scale) + shift


def _norm_mod_kernel(x_ref, g_ref, sc_ref, sh_ref, o_ref):
    o_ref[...] = _norm_mod(x_ref[...], g_ref[...], sc_ref[...], sh_ref[...]).astype(o_ref.dtype)


def _norm_mod_call(x, gain, scale, shift, out_dtype):
    s, d = x.shape
    tm = 256
    row = pl.BlockSpec((1, d), lambda i: (0, 0))
    return pl.pallas_call(
        _norm_mod_kernel,
        grid=(s // tm,),
        in_specs=[pl.BlockSpec((tm, d), lambda i: (i, 0)), row, row, row],
        out_specs=pl.BlockSpec((tm, d), lambda i: (i, 0)),
        out_shape=jax.ShapeDtypeStruct((s, d), out_dtype),
        compiler_params=_params(("parallel",), 40),
        name="norm_mod",
    )(x, gain, scale, shift)


def _rope_kernel(pos_ref, c_ref, s1_ref, s2_ref):
    half = ROPE_DIM // 2
    pos = pos_ref[...].astype(F32)
    lane = lax.broadcasted_iota(I32, (1, LANES), 1)
    fidx = (lane & (half - 1)).astype(F32)
    inv_freq = jnp.exp(fidx * (-math.log(ROPE_THETA) / half))
    ang = pos * inv_freq
    cs = jnp.cos(ang)
    sn = jnp.sin(ang)
    c_ref[...] = jnp.where(lane < ROPE_DIM, cs, 1.0)
    s1_ref[...] = jnp.where(lane < half, -sn, 0.0)
    s2_ref[...] = jnp.where((lane >= half) & (lane < ROPE_DIM), sn, 0.0)


def _rope_tables(positions):
    s = positions.shape[0]
    tm = 512
    spec = pl.BlockSpec((tm, LANES), lambda i: (i, 0))
    shp = jax.ShapeDtypeStruct((s, LANES), F32)
    return pl.pallas_call(
        _rope_kernel,
        grid=(s // tm,),
        in_specs=[pl.BlockSpec((tm, 1), lambda i: (i, 0))],
        out_specs=[spec, spec, spec],
        out_shape=[shp, shp, shp],
        compiler_params=_params(("parallel",), 32),
        name="rope_tables",
    )(positions.reshape(s, 1))


PROJ_TM = 1024
PROJ_TN = 512


def _piece_transpose8(xs):
    xs = list(xs)
    lane = lax.broadcasted_iota(I32, xs[0].shape, 1)
    for h in (4, 2, 1):
        low = (lane & (16 * h)) == 0
        for a in range(8):
            if a & h:
                continue
            lo, hi = xs[a], xs[a + h]
            xs[a] = jnp.where(low, lo, pltpu.roll(hi, 16 * h, 1))
            xs[a + h] = jnp.where(low, pltpu.roll(lo, LANES - 16 * h, 1), hi)
    return xs


PROJ_RC = 256


def _qkv_kernel(a_ref, w_ref, c_ref, s1_ref, s2_ref, nat_ref, p4_ref, p16_ref, slab, *, tm, tn):
    j = pl.program_id(1)
    rep = tn // HEAD_DIM
    half = ROPE_DIM // 2
    is_qk = j < 2 * ATTN_WIDTH // tn
    scale = jnp.where(j < ATTN_WIDTH // tn, HEAD_DIM ** -0.5 * math.log2(math.e), 1.0).astype(F32)
    rc = PROJ_RC
    for ch in range(tm // rc):
        rows = slice(ch * rc, (ch + 1) * rc)
        acc = jnp.dot(a_ref[rows, :], w_ref[...], preferred_element_type=F32)
        c = jnp.concatenate([jnp.where(is_qk, c_ref[rows, :], 1.0) * scale] * rep, axis=1)
        s1 = jnp.concatenate([jnp.where(is_qk, s1_ref[rows, :], 0.0) * scale] * rep, axis=1)
        s2 = jnp.concatenate([jnp.where(is_qk, s2_ref[rows, :], 0.0) * scale] * rep, axis=1)
        val = acc * c + pltpu.roll(acc, tn - half, 1) * s1 + pltpu.roll(acc, half, 1) * s2
        for cc in range(rep):
            piece = val[:, cc * LANES:(cc + 1) * LANES]
            nat_ref[cc, rows, :] = piece.astype(nat_ref.dtype)
            slab[cc, rows, :] = piece
        for d, ref in ((4, p4_ref), (16, p16_ref)):
            nr = rc // d
            for r in range(d):
                for cc in range(rep):
                    ref[r, cc, ch * nr:(ch + 1) * nr, :] = (
                        slab[cc, pl.ds(ch * rc + r, nr, stride=d), :].astype(ref.dtype))


def _qkv_proj(xm, w_qkv, cos_t, sin1_t, sin2_t):
    s, d = xm.shape
    n = w_qkv.shape[1]
    tm, tn = PROJ_TM, PROJ_TN
    nh, hpt = n // HEAD_DIM, tn // HEAD_DIM
    tab = pl.BlockSpec((tm, LANES), lambda i, j: (i, 0))
    return pl.pallas_call(
        functools.partial(_qkv_kernel, tm=tm, tn=tn),
        grid=(s // tm, n // tn),
        in_specs=[pl.BlockSpec((tm, d), lambda i, j: (i, 0)),
                  pl.BlockSpec((d, tn), lambda i, j: (0, j)),
                  tab, tab, tab],
        out_specs=[pl.BlockSpec((hpt, tm, HEAD_DIM), lambda i, j: (j, i, 0)),
                   pl.BlockSpec((4, hpt, tm // 4, HEAD_DIM), lambda i, j: (0, j, i, 0)),
                   pl.BlockSpec((16, hpt, tm // 16, HEAD_DIM), lambda i, j: (0, j, i, 0))],
        out_shape=[jax.ShapeDtypeStruct((nh, s, HEAD_DIM), BF16),
                   jax.ShapeDtypeStruct((4, nh, s // 4, HEAD_DIM), BF16),
                   jax.ShapeDtypeStruct((16, nh, s // 16, HEAD_DIM), BF16)],
        scratch_shapes=[pltpu.VMEM((tn // LANES, tm, LANES), F32)],
        compiler_params=_params(("parallel", "arbitrary"), 52),
        name="qkv_proj",
    )(xm, w_qkv, cos_t, sin1_t, sin2_t)


def _uproj_kernel(a_ref, w_ref, u_ref, slab, *, tm, tn):
    rc = PROJ_RC
    nchunk = rc // SSM_CHUNK
    for ch in range(tm // rc):
        acc = jnp.dot(a_ref[ch * rc:(ch + 1) * rc, :], w_ref[...], preferred_element_type=F32)
        out_rows = slice(ch * nchunk, (ch + 1) * nchunk)
        for c in range(tn // LANES):
            slab[c, ch * rc:(ch + 1) * rc, :] = acc[:, c * LANES:(c + 1) * LANES]
            rows = [slab[c, pl.ds(ch * rc + i, nchunk, stride=SSM_CHUNK), :]
                    for i in range(SSM_CHUNK)]
            first = _piece_transpose8(rows[:8])
            second = _piece_transpose8(rows[8:])
            for gp in range(8):
                u_ref[c * 8 + gp, out_rows, 0:LANES] = first[gp].astype(u_ref.dtype)
                u_ref[c * 8 + gp, out_rows, LANES:2 * LANES] = second[gp].astype(u_ref.dtype)


def _u_proj(xm, w_u):
    s, d = xm.shape
    n = w_u.shape[1]
    tm, tn = PROJ_TM, PROJ_TN
    gpt = tn // SSM_GROUP
    return pl.pallas_call(
        functools.partial(_uproj_kernel, tm=tm, tn=tn),
        grid=(s // tm, n // tn),
        in_specs=[pl.BlockSpec((tm, d), lambda i, j: (i, 0)),
                  pl.BlockSpec((d, tn), lambda i, j: (0, j))],
        out_specs=pl.BlockSpec((gpt, tm // SSM_CHUNK, SSM_CHUNK * SSM_GROUP), lambda i, j: (j, i, 0)),
        out_shape=jax.ShapeDtypeStruct((n // SSM_GROUP, s // SSM_CHUNK, SSM_CHUNK * SSM_GROUP), BF16),
        scratch_shapes=[pltpu.VMEM((tn // LANES, tm, LANES), F32)],
        compiler_params=_params(("parallel", "arbitrary"), 48),
        name="u_proj",
    )(xm, w_u)


def _attn_kernel(q_ref, k_ref, v_ref, o_ref, l_ref, *, n, hps):
    tq = LANES
    tk = 2 * LANES
    nsb = n // tq
    col_minus_row = lax.broadcasted_iota(I32, (tq, tk), 1) - lax.broadcasted_iota(I32, (tq, tk), 0)
    eye = lax.broadcasted_iota(I32, (tq, tq), 0) == lax.broadcasted_iota(I32, (tq, tq), 1)

    def band_bias(key_start_minus_q0):
        return jnp.where(jnp.abs(col_minus_row + key_start_minus_q0) <= ATTN_HALF, 0.0, NEG_INF).astype(F32)

    def block(hh, sb, q0, ks, bias):
        q = q_ref[hh, pl.ds(q0, tq), :]
        k = k_ref[hh, pl.ds(ks, tk), :]
        v = v_ref[hh, pl.ds(ks, tk), :]
        s = lax.dot_general(q, k, (((1,), (1,)), ((), ())), preferred_element_type=F32) + bias
        m = jnp.max(s, axis=-1, keepdims=True)
        p = jnp.exp2(s - m)
        l = jnp.sum(p, axis=-1, keepdims=True)
        o = jnp.dot(p.astype(BF16), v, preferred_element_type=F32) / l
        o_ref[hh, pl.ds(q0, tq), :] = o.astype(o_ref.dtype)
        lse = (m + jnp.log2(l)) * math.log(2.0)
        l_ref[hh, pl.ds(sb, 1), :] = jnp.sum(jnp.where(eye, lse, 0.0), axis=0, keepdims=True)

    interior = nsb - 2
    edge_first, edge_last = band_bias(0), band_bias(-tq)
    bias = band_bias(-ATTN_HALF)
    for hh in range(hps):
        block(hh, 0, 0, 0, edge_first)
        block(hh, nsb - 1, n - tq, n - tk, edge_last)
        if 0 < interior <= 6:
            for sb in range(1, nsb - 1):
                block(hh, sb, sb * tq, sb * tq - ATTN_HALF, bias)
        elif interior > 0:
            unroll = max(u for u in range(1, ATTN_UNROLL + 1) if interior % u == 0)

            def body(it, carry, hh=hh, unroll=unroll):
                for u in range(unroll):
                    sb = 1 + it * unroll + u
                    q0 = pl.multiple_of(sb * tq, tq)
                    block(hh, sb, q0, pl.multiple_of(q0 - ATTN_HALF, ATTN_HALF), bias)
                return carry

            lax.fori_loop(0, interior // unroll, body, 0)


def _dilated_attention(qkv, dil):
    _, _, n, _ = qkv.shape
    s = n * dil
    nsb = n // LANES
    hps = 2 if nsb <= 8 else 1

    def spec(off):
        return pl.BlockSpec((None, hps, n, HEAD_DIM), lambda r, h: (r, off // hps + h, 0, 0))

    o, lse = pl.pallas_call(
        functools.partial(_attn_kernel, n=n, hps=hps),
        grid=(dil, N_HEADS // hps),
        in_specs=[spec(0), spec(N_HEADS), spec(2 * N_HEADS)],
        out_specs=[pl.BlockSpec((None, hps, n, HEAD_DIM), lambda r, h: (r, h, 0, 0)),
                   pl.BlockSpec((hps, nsb, LANES), lambda r, h: (r * (N_HEADS // hps) + h, 0, 0))],
        out_shape=[jax.ShapeDtypeStruct((dil, N_HEADS, n, HEAD_DIM), BF16),
                   jax.ShapeDtypeStruct((dil * N_HEADS, nsb, LANES), F32)],
        compiler_params=_params(("parallel", "parallel"), 48),
        name=f"dilated_attn_d{dil}",
    )(qkv, qkv, qkv)
    lse = lse.reshape(dil, N_HEADS, n).transpose(2, 0, 1).reshape(s, N_HEADS)
    return o, lse


def _merge_kernel(o1_ref, o4_ref, o16_ref, l1_ref, l2_ref, l3_ref, g_ref, out_ref, acc_ref, s4_ref, s16_ref, *, tm):
    for d, src, dst in ((4, o4_ref, s4_ref), (16, o16_ref, s16_ref)):
        for r in range(d):
            for h in range(N_HEADS):
                dst[h, pl.ds(r, tm // d, stride=d), :] = src[r, h].astype(F32)
    la, lb, lc = l1_ref[...], l2_ref[...], l3_ref[...]
    m = jnp.maximum(jnp.maximum(la, lb), lc)
    ea, eb, ec = jnp.exp(la - m), jnp.exp(lb - m), jnp.exp(lc - m)
    den = ea + eb + ec
    wa, wb, wc = ea / den, eb / den, ec / den
    sq = jnp.zeros((tm, 1), F32)
    for h in range(N_HEADS):
        cs = slice(h * HEAD_DIM, (h + 1) * HEAD_DIM)
        slab = (wa[:, h:h + 1] * o1_ref[h].astype(F32)
                + wb[:, h:h + 1] * s4_ref[h]
                + wc[:, h:h + 1] * s16_ref[h])
        acc_ref[:, cs] = slab
        sq = sq + jnp.sum(slab * slab, axis=-1, keepdims=True)
    inv = lax.rsqrt(sq * (1.0 / ATTN_WIDTH) + EPS)
    out_ref[...] = (acc_ref[...] * inv * g_ref[...]).astype(out_ref.dtype)


def _merge_patterns(outs, lses, gain):
    o1, o4, o16 = outs
    s = o1.shape[2]
    tm = 256
    ospec = pl.BlockSpec((tm, ATTN_WIDTH), lambda i: (i, 0))
    lspec = pl.BlockSpec((tm, N_HEADS), lambda i: (i, 0))
    return pl.pallas_call(
        functools.partial(_merge_kernel, tm=tm),
        grid=(s // tm,),
        in_specs=[pl.BlockSpec((None, N_HEADS, tm, HEAD_DIM), lambda i: (0, 0, i, 0)),
                  pl.BlockSpec((4, N_HEADS, tm // 4, HEAD_DIM), lambda i: (0, 0, i, 0)),
                  pl.BlockSpec((16, N_HEADS, tm // 16, HEAD_DIM), lambda i: (0, 0, i, 0)),
                  lspec, lspec, lspec,
                  pl.BlockSpec((1, ATTN_WIDTH), lambda i: (0, 0))],
        out_specs=ospec,
        out_shape=jax.ShapeDtypeStruct((s, ATTN_WIDTH), BF16),
        scratch_shapes=[pltpu.VMEM((tm, ATTN_WIDTH), F32),
                        pltpu.VMEM((N_HEADS, tm, HEAD_DIM), F32),
                        pltpu.VMEM((N_HEADS, tm, HEAD_DIM), F32)],
        compiler_params=_params(("parallel",), 32),
        name="attn_merge_norm",
    )(o1, o4, o16, *lses, gain)


def _ssm_direction_terms(lam_re, lam_im, log_dt, b_re, b_im, c_re, c_im):
    dt = jnp.exp(log_dt)[:, None]
    mag = jnp.exp(lam_re * dt)
    ang = lam_im * dt
    lb_re = mag * jnp.cos(ang)
    lb_im = mag * jnp.sin(ang)
    den = lam_re * lam_re + lam_im * lam_im
    ar = lb_re - 1.0
    ai = lb_im
    coef_re = (ar * lam_re + ai * lam_im) / den
    coef_im = (ai * lam_re - ar * lam_im) / den
    bb_re = coef_re[..., None] * b_re - coef_im[..., None] * b_im
    bb_im = coef_re[..., None] * b_im + coef_im[..., None] * b_re
    tau = jnp.arange(SSM_CHUNK + 1, dtype=F32)[:, None, None]
    pw_mag = jnp.exp(tau * (lam_re * dt)[None])
    pw_re = pw_mag * jnp.cos(tau * ang[None])
    pw_im = pw_mag * jnp.sin(tau * ang[None])
    e_re = pw_re[..., None] * bb_re[None] - pw_im[..., None] * bb_im[None]
    e_im = pw_re[..., None] * bb_im[None] + pw_im[..., None] * bb_re[None]
    kern = (jnp.einsum('ghp,tgpi->tghi', c_re, e_re[:SSM_CHUNK], precision=HIGHEST)
            - jnp.einsum('ghp,tgpi->tghi', c_im, e_im[:SSM_CHUNK], precision=HIGHEST))
    ca_re = c_re[None] * pw_re[:, :, None, :] - c_im[None] * pw_im[:, :, None, :]
    ca_im = c_re[None] * pw_im[:, :, None, :] + c_im[None] * pw_re[:, :, None, :]
    return kern, e_re, e_im, ca_re, ca_im, pw_re[SSM_CHUNK], pw_im[SSM_CHUNK]


def _ssm_operators(lam_re_f, lam_im_f, log_dt_f, lam_re_b, lam_im_b, log_dt_b, b_re, b_im, c_re, c_im):
    L, G, P = SSM_CHUNK, SSM_GROUPS, SSM_STATE
    kf, ef_re, ef_im, caf_re, caf_im, af_re, af_im = _ssm_direction_terms(
        lam_re_f, lam_im_f, log_dt_f, b_re, b_im, c_re, c_im)
    kb, eb_re, eb_im, cab_re, cab_im, ab_re, ab_im = _ssm_direction_terms(
        lam_re_b, lam_im_b, log_dt_b, b_re, b_im, c_re, c_im)
    lag = jnp.arange(L)[None, :] - jnp.arange(L)[:, None]
    tau = jnp.arange(L)[:, None, None]
    place = jnp.concatenate([(lag[None] == tau), (-lag[None] == tau)], axis=0).astype(F32)
    w = jnp.einsum('tij,tgoh->gihjo', place, jnp.concatenate([kf, kb], axis=0), precision=HIGHEST)
    w = w.reshape(G, L * SSM_GROUP, L * SSM_GROUP)

    def p_mat(e, order):
        return jnp.transpose(e[order], (1, 0, 3, 2)).reshape(G, L * SSM_GROUP, P)

    def q_mat(ca, order):
        return jnp.transpose(ca[order], (1, 3, 0, 2)).reshape(G, P, L * SSM_GROUP)

    f_ord = (L - 1) - jnp.arange(L)
    b_ord = jnp.arange(L)
    parts_p = [p_mat(ef_re, f_ord), p_mat(ef_im, f_ord), p_mat(eb_re, b_ord), p_mat(eb_im, b_ord)]
    parts_q = [q_mat(caf_re, jnp.arange(L) + 1), -q_mat(caf_im, jnp.arange(L) + 1),
               q_mat(cab_re, L - jnp.arange(L)), -q_mat(cab_im, L - jnp.arange(L))]
    even = (jnp.arange(G) % 2 == 0)[:, None, None]

    def pad_p(x):
        z = jnp.zeros_like(x)
        return jnp.where(even, jnp.concatenate([x, z], -1), jnp.concatenate([z, x], -1))

    def pad_q(x):
        z = jnp.zeros_like(x)
        return jnp.where(even, jnp.concatenate([x, z], 1), jnp.concatenate([z, x], 1))

    p_all = jnp.stack([pad_p(x) for x in parts_p], axis=1).astype(BF16)
    q_all = jnp.stack([pad_q(x) for x in parts_q], axis=1).astype(BF16)

    def tiles(re, im):
        re = re.reshape(G // SSM_GB, SSM_GB // 2, 2 * P)
        im = im.reshape(G // SSM_GB, SSM_GB // 2, 2 * P)
        return jnp.concatenate([re, re], axis=1), jnp.concatenate([-im, im], axis=1)

    a_all = jnp.stack([*tiles(af_re, af_im), *tiles(ab_re, ab_im)], axis=1)
    return w.astype(BF16), p_all, q_all, a_all


def _gelu_tanh(x):
    return 0.5 * x * (1.0 + jnp.tanh(math.sqrt(2.0 / math.pi) * (x + 0.044715 * (x * x * x))))


SUBLANES = 8


def _ssm_kernel(u_ref, w_ref, p_ref, q_ref, a_ref, d_ref, y_ref, sf, sb, *, nc, gb):
    npair = gb // 2
    assert 2 * npair == SUBLANES
    for pr in range(npair):
        g0, g1 = 2 * pr, 2 * pr + 1
        for k, (scr, row) in enumerate(((sf, pr), (sf, npair + pr), (sb, pr), (sb, npair + pr))):
            contrib = (jnp.dot(u_ref[g0], p_ref[g0, k], preferred_element_type=F32)
                       + jnp.dot(u_ref[g1], p_ref[g1, k], preferred_element_type=F32))
            scr[pl.ds(row, nc, stride=SUBLANES), :] = contrib

    a1f, a2f, a1b, a2b = a_ref[0, 0], a_ref[0, 1], a_ref[0, 2], a_ref[0, 3]

    def step(c, carry):
        hf, hb = carry
        cf = pl.multiple_of(c * SUBLANES, SUBLANES)
        cb = pl.multiple_of((nc - 1 - c) * SUBLANES, SUBLANES)
        rf = sf[pl.ds(cf, SUBLANES), :]
        rb = sb[pl.ds(cb, SUBLANES), :]
        sf[pl.ds(cf, SUBLANES), :] = hf
        sb[pl.ds(cb, SUBLANES), :] = hb
        return (a1f * hf + a2f * pltpu.roll(hf, npair, 0) + rf,
                a1b * hb + a2b * pltpu.roll(hb, npair, 0) + rb)

    z = jnp.zeros((SUBLANES, LANES), F32)
    lax.fori_loop(0, nc, step, (z, z))

    for pr in range(npair):
        states = [scr[pl.ds(row, nc, stride=SUBLANES), :].astype(BF16)
                  for scr, row in ((sf, pr), (sf, npair + pr), (sb, pr), (sb, npair + pr))]
        for g in (2 * pr, 2 * pr + 1):
            u = u_ref[g]
            y = jnp.dot(u, w_ref[g], preferred_element_type=F32)
            for k in range(4):
                y = y + jnp.dot(states[k], q_ref[g, k], preferred_element_type=F32)
            y = y + d_ref[g] * u.astype(F32)
            y_ref[g] = _gelu_tanh(y).astype(y_ref.dtype)


def _ssm_mixer(u_t, w, p_all, q_all, a_all, d_t):
    g, nc, k = u_t.shape
    gb = SSM_GB
    return pl.pallas_call(
        functools.partial(_ssm_kernel, nc=nc, gb=gb),
        grid=(g // gb,),
        in_specs=[pl.BlockSpec((gb, nc, k), lambda i: (i, 0, 0)),
                  pl.BlockSpec((gb, k, k), lambda i: (i, 0, 0)),
                  pl.BlockSpec((gb, 4, k, LANES), lambda i: (i, 0, 0, 0)),
                  pl.BlockSpec((gb, 4, LANES, k), lambda i: (i, 0, 0, 0)),
                  pl.BlockSpec((1, 4, SUBLANES, LANES), lambda i: (i, 0, 0, 0)),
                  pl.BlockSpec((gb, 1, k), lambda i: (i, 0, 0))],
        out_specs=pl.BlockSpec((gb, nc, k), lambda i: (i, 0, 0)),
        out_shape=jax.ShapeDtypeStruct((g, nc, k), BF16),
        scratch_shapes=[pltpu.VMEM((nc * SUBLANES, LANES), F32)] * 2,
        compiler_params=_params(("parallel",), 48),
        name="ssm_scan",
    )(u_t, w, p_all, q_all, a_all, d_t)


def _glu_kernel(g_ref, w_ref, b_ref, n_ref, o_ref, nat, gb, *, tm):
    nchunk = tm // SSM_CHUNK
    for sl in range(SSM_WIDTH // LANES):
        for half in range(2):
            xs = [g_ref[8 * sl + gp, :, half * LANES:(half + 1) * LANES].astype(F32) for gp in range(8)]
            ys = _piece_transpose8(xs)
            for jj in range(8):
                nat[sl, pl.ds(half * 8 + jj, nchunk, stride=SSM_CHUNK), :] = ys[jj]
        gb[:, sl * LANES:(sl + 1) * LANES] = nat[sl].astype(BF16)
    z = jnp.dot(gb[...], w_ref[...], preferred_element_type=F32) + b_ref[...]
    sq = jnp.zeros((tm, 1), F32)
    for sl in range(SSM_WIDTH // LANES):
        cs = slice(sl * LANES, (sl + 1) * LANES)
        out = nat[sl] * jax.nn.sigmoid(z[:, cs])
        nat[sl] = out
        sq = sq + jnp.sum(out * out, axis=-1, keepdims=True)
    inv = lax.rsqrt(sq * (1.0 / SSM_WIDTH) + EPS)
    for sl in range(SSM_WIDTH // LANES):
        cs = slice(sl * LANES, (sl + 1) * LANES)
        o_ref[:, cs] = (nat[sl] * inv * n_ref[:, cs]).astype(o_ref.dtype)


def _glu_norm(g_t, w_glu, b_glu, gain):
    ng, nc, k = g_t.shape
    s, d = nc * SSM_CHUNK, ng * SSM_GROUP
    tm = 512
    row = pl.BlockSpec((1, d), lambda i: (0, 0))
    return pl.pallas_call(
        functools.partial(_glu_kernel, tm=tm),
        grid=(s // tm,),
        in_specs=[pl.BlockSpec((ng, tm // SSM_CHUNK, k), lambda i: (0, i, 0)),
                  pl.BlockSpec((d, d), lambda i: (0, 0)), row, row],
        out_specs=pl.BlockSpec((tm, d), lambda i: (i, 0)),
        out_shape=jax.ShapeDtypeStruct((s, d), BF16),
        scratch_shapes=[pltpu.VMEM((d // LANES, tm, LANES), F32),
                        pltpu.VMEM((tm, d), BF16)],
        compiler_params=_params(("parallel",), 48),
        name="ssm_glu_norm",
    )(g_t, w_glu, b_glu, gain)


def _outproj_kernel(a1_ref, a2_ref, w1_ref, w2_ref, x_ref, gm_ref, o_ref):
    mix = (jnp.dot(a1_ref[...], w1_ref[...], preferred_element_type=F32)
           + jnp.dot(a2_ref[...], w2_ref[...], preferred_element_type=F32))
    o_ref[...] = x_ref[...] + gm_ref[...] * mix


def _out_proj(attn_n, ssm_n, w_out, x, g_m):
    s, k = attn_n.shape
    n = w_out.shape[1]
    tm, tn = 1024, 512
    return pl.pallas_call(
        _outproj_kernel,
        grid=(s // tm, n // tn),
        in_specs=[pl.BlockSpec((tm, k), lambda i, j: (i, 0)),
                  pl.BlockSpec((tm, k), lambda i, j: (i, 0)),
                  pl.BlockSpec((k, tn), lambda i, j: (0, j)),
                  pl.BlockSpec((k, tn), lambda i, j: (1, j)),
                  pl.BlockSpec((tm, tn), lambda i, j: (i, j)),
                  pl.BlockSpec((1, tn), lambda i, j: (0, j))],
        out_specs=pl.BlockSpec((tm, tn), lambda i, j: (i, j)),
        out_shape=jax.ShapeDtypeStruct((s, n), F32),
        compiler_params=_params(("parallel", "arbitrary"), 48),
        name="out_proj_residual",
    )(attn_n, ssm_n, w_out, w_out, x, g_m)


def _ffn_prep_kernel(h_ref, g_ref, sc_ref, sh_ref, wr_ref, xf_ref, lg_ref):
    xf = _norm_mod(h_ref[...], g_ref[...], sc_ref[...], sh_ref[...])
    half = xf.shape[1] // 2
    lo = pltpu.bitcast(xf[:, :half].astype(BF16).astype(F32), jnp.uint32)
    hi = pltpu.bitcast(xf[:, half:].astype(BF16).astype(F32), jnp.uint32)
    xf_ref[...] = (lo >> 16) | hi
    lg_ref[...] = lax.dot_general(wr_ref[...], xf, (((1,), (1,)), ((), ())),
                                  precision=HIGHEST, preferred_element_type=F32)


def _ffn_prep(h1, gain, scale, shift, w_router_t):
    s, d = h1.shape
    tm = 256
    row = pl.BlockSpec((1, d), lambda i: (0, 0))
    return pl.pallas_call(
        _ffn_prep_kernel,
        grid=(s // tm,),
        in_specs=[pl.BlockSpec((tm, d), lambda i: (i, 0)), row, row, row,
                  pl.BlockSpec((N_EXPERTS, d), lambda i: (0, 0))],
        out_specs=[pl.BlockSpec((tm, d // 2), lambda i: (i, 0)),
                   pl.BlockSpec((N_EXPERTS, tm), lambda i: (0, i))],
        out_shape=[jax.ShapeDtypeStruct((s, d // 2), jnp.uint32),
                   jax.ShapeDtypeStruct((N_EXPERTS, s), F32)],
        compiler_params=_params(("parallel",), 40),
        name="ffn_norm_router",
    )(h1, gain, scale, shift, w_router_t)


def _route_kernel(lg_ref, sel_ref, pos_ref, aff_ref, *, cap, nk):
    e_n = N_EXPERTS
    lg = lg_ref[...]
    mx = jnp.max(lg, axis=0, keepdims=True)
    ex = jnp.exp(lg - mx)
    aff = ex / jnp.sum(ex, axis=0, keepdims=True)
    aff_ref[...] = aff
    bits = pltpu.bitcast(aff, I32)

    def count(mask):
        return jnp.sum(jnp.sum(mask.astype(F32), axis=2, keepdims=True), axis=1, keepdims=True)

    def radix(i, thr):
        cand = thr | jnp.left_shift(jnp.int32(1), 30 - i)
        return jnp.where(count(bits >= cand) >= cap, cand, thr)

    thr = lax.fori_loop(0, 31, radix, jnp.zeros((e_n, 1, 1), I32))
    gt = bits > thr
    eq = bits == thr
    need = cap - count(gt)

    li = lax.broadcasted_iota(I32, (LANES, LANES), 0)
    lj = lax.broadcasted_iota(I32, (LANES, LANES), 1)
    tri_lane = (li < lj).astype(BF16)
    ki = lax.broadcasted_iota(I32, (nk, nk), 0)
    kj = lax.broadcasted_iota(I32, (nk, nk), 1)
    tri_row = (kj < ki).astype(BF16)

    def excl_prefix(mask):
        mf = mask.astype(F32)
        within = jnp.dot(mf.reshape(e_n * nk, LANES).astype(BF16), tri_lane,
                         preferred_element_type=F32).reshape(e_n, nk, LANES)
        rowtot = jnp.broadcast_to(jnp.sum(mf, axis=2, keepdims=True), (e_n, nk, LANES)).astype(BF16)
        across = jnp.stack([jnp.dot(tri_row, rowtot[e], preferred_element_type=F32) for e in range(e_n)])
        return within + across

    sel = gt | (eq & (excl_prefix(eq) < need))
    sel_ref[...] = sel.astype(I32)
    pos_ref[...] = excl_prefix(sel).astype(I32)


def _route(logits_t, cap):
    e_n, s = logits_t.shape
    nk = s // LANES
    spec = pl.BlockSpec((e_n, nk, LANES), lambda i: (0, 0, 0))
    sel, pos, aff = pl.pallas_call(
        functools.partial(_route_kernel, cap=cap, nk=nk),
        grid=(1,),
        in_specs=[spec],
        out_specs=[spec, spec, spec],
        out_shape=[jax.ShapeDtypeStruct((e_n, nk, LANES), I32),
                   jax.ShapeDtypeStruct((e_n, nk, LANES), I32),
                   jax.ShapeDtypeStruct((e_n, nk, LANES), F32)],
        compiler_params=_params(("arbitrary",), 48),
        name="expert_choice_route",
    )(logits_t.reshape(e_n, nk, LANES))
    return sel.reshape(e_n, s), pos.reshape(e_n, s), aff.reshape(e_n, s)


def _compact_kernel(off_ref, sel_ref, pos_ref, aff_ref, idx_ref, gate_ref, *, nk):
    e = pl.program_id(0)
    idx_ref[...] = jnp.zeros(idx_ref.shape, I32)
    gate_ref[...] = jnp.zeros(gate_ref.shape, F32)
    local = lax.broadcasted_iota(I32, (LANES, LANES), 0)
    lane = lax.broadcasted_iota(I32, (LANES, LANES), 1)

    unroll = 4 if nk % 4 == 0 else 1

    def body(it, carry):
        for u in range(unroll):
            k = it * unroll + u
            off = off_ref[e, k]
            sel = sel_ref[0, pl.ds(k, 1), :]
            pos = pos_ref[0, pl.ds(k, 1), :]
            aff = aff_ref[0, pl.ds(k, 1), :]
            hit = (sel > 0) & ((pos - off) == local)
            tok = jnp.sum(jnp.where(hit, (lane + k * LANES).astype(F32), 0.0), axis=1, keepdims=True)
            gat = jnp.sum(jnp.where(hit, aff, 0.0), axis=1, keepdims=True)
            idx_ref[0, pl.ds(off, LANES), :] = tok.astype(I32)
            gate_ref[0, pl.ds(off, LANES), :] = gat
        return carry

    lax.fori_loop(0, nk // unroll, body, 0)


def _compact(sel, pos, aff, cap):
    e_n, s = sel.shape
    nk = s // LANES
    capp = cap + LANES
    offs = pos[:, ::LANES]
    tok = pl.BlockSpec((1, nk, LANES), lambda e, off: (e, 0, 0))
    col = pl.BlockSpec((1, capp, 1), lambda e, off: (e, 0, 0))
    grid_spec = pltpu.PrefetchScalarGridSpec(
        num_scalar_prefetch=1, grid=(e_n,), in_specs=[tok, tok, tok], out_specs=[col, col])
    idx, gate = pl.pallas_call(
        functools.partial(_compact_kernel, nk=nk),
        grid_spec=grid_spec,
        out_shape=[jax.ShapeDtypeStruct((e_n, capp, 1), I32),
                   jax.ShapeDtypeStruct((e_n, capp, 1), F32)],
        compiler_params=_params(("arbitrary",), 32),
        name="expert_compact",
    )(offs, sel.reshape(e_n, nk, LANES), pos.reshape(e_n, nk, LANES), aff.reshape(e_n, nk, LANES))
    return idx[:, :cap, 0], gate[:, :cap, :]


def _moe_up_kernel(idx_hbm, xf_hbm, wg_ref, wu_ref, h_ref, xbuf, idx_s, sem_x, sem_i, *, cap, tm, rps):
    nf, nh = pl.num_programs(1), pl.num_programs(2)
    phase = pl.program_id(0)
    n_phase = pl.num_programs(0)
    f = pl.program_id(1) * nh + pl.program_id(2)
    nf = nf * nh
    step = phase * nf + f
    b = phase % 2
    row0 = pl.multiple_of(pl.program_id(2) * tm, tm)

    def row_copy(token, buf, r):
        return pltpu.make_async_copy(xf_hbm.at[pl.ds(token, 1), :], xbuf.at[buf, pl.ds(r, 1), :], sem_x.at[buf])

    def wait_rows(buf, n):
        pltpu.make_async_copy(xf_hbm.at[pl.ds(0, n), :], xbuf.at[buf, pl.ds(0, n), :], sem_x.at[buf]).wait()

    @pl.when(step == 0)
    def _prologue():
        c_idx = pltpu.make_async_copy(idx_hbm, idx_s, sem_i.at[0])
        c_idx.start()
        c_idx.wait()

        def one(r, c):
            row_copy(idx_s[r], 0, r).start()
            return c
        lax.fori_loop(0, cap, one, 0)
        wait_rows(0, cap)

    @pl.when(step > 0)
    def _():
        wait_rows(jnp.where(f == 0, b, 1 - b), rps)

    nxt = jnp.minimum(phase + 1, n_phase - 1)
    base = nxt * cap + f * rps
    for r in range(rps):
        row_copy(idx_s[base + r], 1 - b, f * rps + r).start()

    wg = wg_ref[0].astype(BF16)
    wu = wu_ref[0].astype(BF16)
    half = wg.shape[0] // 2
    rc = MOE_RC
    for ch in range(tm // rc):
        x = xbuf[b, pl.ds(row0 + ch * rc, rc), :]
        x_lo = pltpu.bitcast(x << 16, F32).astype(BF16)
        x_hi = pltpu.bitcast(x & jnp.uint32(0xFFFF0000), F32).astype(BF16)
        g = (jnp.dot(x_lo, wg[:half], preferred_element_type=F32)
             + jnp.dot(x_hi, wg[half:], preferred_element_type=F32))
        u = (jnp.dot(x_lo, wu[:half], preferred_element_type=F32)
             + jnp.dot(x_hi, wu[half:], preferred_element_type=F32))
        h_ref[0, ch * rc:(ch + 1) * rc, :] = (g * jax.nn.sigmoid(g) * u).astype(h_ref.dtype)

    @pl.when(step == n_phase * nf - 1)
    def _drain():
        wait_rows(1 - b, rps)


MOE_RC = 256


def _moe_up(idx, xf_packed, w_gate, w_up, cap):
    e_n, d, ff = w_gate.shape
    tf = 256
    tm = min(1024, cap)
    nf, nh = ff // tf, cap // tm
    rps = cap // (nf * nh)
    assert rps * nf * nh == cap and tm % MOE_RC == 0
    return pl.pallas_call(
        functools.partial(_moe_up_kernel, cap=cap, tm=tm, rps=rps),
        grid=(e_n, nf, nh),
        in_specs=[pl.BlockSpec(memory_space=pl.ANY),
                  pl.BlockSpec(memory_space=pl.ANY),
                  pl.BlockSpec((1, d, tf), lambda e, f, m: (e, 0, f)),
                  pl.BlockSpec((1, d, tf), lambda e, f, m: (e, 0, f))],
        out_specs=pl.BlockSpec((1, tm, tf), lambda e, f, m: (e, m, f)),
        out_shape=jax.ShapeDtypeStruct((e_n, cap, ff), BF16),
        scratch_shapes=[pltpu.VMEM((2, cap, d // 2), jnp.uint32),
                        pltpu.SMEM((e_n * cap,), I32),
                        pltpu.SemaphoreType.DMA((2,)),
                        pltpu.SemaphoreType.DMA((1,))],
        compiler_params=_params(("arbitrary", "arbitrary", "arbitrary"), 60),
        name="moe_gather_up",
    )(idx.reshape(e_n * cap), xf_packed, w_gate, w_up)


def _moe_down_kernel(h_ref, wd_ref, gate_ref, y_ref):
    y = jnp.dot(h_ref[0], wd_ref[0].astype(BF16), preferred_element_type=F32)
    y_ref[0] = (y * gate_ref[0]).astype(y_ref.dtype)


def _moe_down(hdn, w_down, gates):
    e_n, cap, ff = hdn.shape
    d = w_down.shape[2]
    tn = 512
    return pl.pallas_call(
        _moe_down_kernel,
        grid=(e_n, d // tn),
        in_specs=[pl.BlockSpec((1, cap, ff), lambda e, n: (e, 0, 0)),
                  pl.BlockSpec((1, ff, tn), lambda e, n: (e, 0, n)),
                  pl.BlockSpec((1, cap, 1), lambda e, n: (e, 0, 0))],
        out_specs=pl.BlockSpec((1, cap, tn), lambda e, n: (e, 0, n)),
        out_shape=jax.ShapeDtypeStruct((e_n, cap, d), BF16),
        compiler_params=_params(("parallel", "arbitrary"), 48),
        name="moe_down",
    )(hdn, w_down, gates)


COMBINE_WINDOW = 64
SLOT_ALIGN = 16


def _combine_kernel(lo_ref, h_ref, sel_ref, pos_ref, gf_ref, gn_ref, y_hbm, o_ref,
                    ybuf, yextra, acc_ref, sem, sem_x, *, cap, tm):
    i = pl.program_id(0)
    win = COMBINE_WINDOW
    e_n = N_EXPERTS

    n_steps = pl.num_programs(0)
    cur = i % 2

    def window(e, j, step=None):
        step = i if step is None else step
        nominal = (lo_ref[e, step] // SLOT_ALIGN) * SLOT_ALIGN + j * win
        start = pl.multiple_of(jnp.minimum(nominal, cap - win), SLOT_ALIGN)
        return nominal, start

    def first_copy(e, step, buf):
        _, start = window(e, 0, step)
        return pltpu.make_async_copy(y_hbm.at[e, pl.ds(start, win), :],
                                     ybuf.at[buf, pl.ds(e * win, win), :], sem.at[buf, e])

    @pl.when(i == 0)
    def _():
        for e in range(e_n):
            first_copy(e, 0, 0).start()

    @pl.when(i + 1 < n_steps)
    def _():
        for e in range(e_n):
            first_copy(e, i + 1, 1 - cur).start()

    sel = sel_ref[...]
    pos = pos_ref[...]
    lane = lax.broadcasted_iota(I32, (tm, LANES), 1)
    left = lane < win
    pieces = []
    for e2 in range(e_n // 2):
        ea, eb = 2 * e2, 2 * e2 + 1
        na, sa = window(ea, 0)
        nb, sb = window(eb, 0)
        slot = jnp.where(left, sa + lane, sb + lane - win)
        nominal = jnp.where(left, na, nb)
        pe = jnp.where(left, pos[:, ea:ea + 1], pos[:, eb:eb + 1])
        se = jnp.where(left, sel[:, ea:ea + 1], sel[:, eb:eb + 1])
        pieces.append(((pe == slot) & (se > 0) & (slot >= nominal)).astype(BF16))
    onehot = jnp.concatenate(pieces, axis=1)
    for e in range(e_n):
        first_copy(e, i, cur).wait()
    acc_ref[...] = jnp.dot(onehot, ybuf[cur], preferred_element_type=F32)

    lane_w = lax.broadcasted_iota(I32, (tm, win), 1)
    for e in range(e_n):
        base = (lo_ref[e, i] // SLOT_ALIGN) * SLOT_ALIGN
        nwin = (lo_ref[e, i + 1] - base + win - 1) // win

        def extra(j, c, e=e):
            nominal, start = window(e, j)
            cp = pltpu.make_async_copy(y_hbm.at[e, pl.ds(start, win), :], yextra, sem_x.at[0])
            cp.start()
            cp.wait()
            slot = start + lane_w
            oh = ((pos[:, e:e + 1] == slot) & (sel[:, e:e + 1] > 0) & (slot >= nominal)).astype(BF16)
            acc_ref[...] += jnp.dot(oh, yextra[...], preferred_element_type=F32)
            return c

        lax.fori_loop(1, nwin, extra, 0)

    h2 = h_ref[...] + gf_ref[...] * acc_ref[...]
    ms = jnp.mean(h2 * h2, axis=-1, keepdims=True)
    o_ref[...] = h2 * lax.rsqrt(ms + EPS) * gn_ref[...]


def _combine(h1, y, sel_tm, pos_tm, lo, g_f, gain, cap):
    s, d = h1.shape
    tm = 256
    win = COMBINE_WINDOW
    row = pl.BlockSpec((1, d), lambda i, lo_r: (0, 0))
    tok = pl.BlockSpec((tm, N_EXPERTS), lambda i, lo_r: (i, 0))
    grid_spec = pltpu.PrefetchScalarGridSpec(
        num_scalar_prefetch=1,
        grid=(s // tm,),
        in_specs=[pl.BlockSpec((tm, d), lambda i, lo_r: (i, 0)), tok, tok, row, row,
                  pl.BlockSpec(memory_space=pl.ANY)],
        out_specs=pl.BlockSpec((tm, d), lambda i, lo_r: (i, 0)),
        scratch_shapes=[pltpu.VMEM((2, N_EXPERTS * win, d), BF16),
                        pltpu.VMEM((win, d), BF16),
                        pltpu.VMEM((tm, d), F32),
                        pltpu.SemaphoreType.DMA((2, N_EXPERTS)),
                        pltpu.SemaphoreType.DMA((1,))],
    )
    return pl.pallas_call(
        functools.partial(_combine_kernel, cap=cap, tm=tm),
        grid_spec=grid_spec,
        out_shape=jax.ShapeDtypeStruct((s, d), F32),
        compiler_params=_params(("arbitrary",), 56),
        name="moe_combine_norm",
    )(lo, h1, sel_tm, pos_tm, g_f, gain, y)


def kernel(x, c, positions, w_ada, b_ada, norm_mix_gain, w_in, lam_re_fwd, lam_im_fwd, log_dt_fwd, lam_re_bwd, lam_im_bwd, log_dt_bwd, ssm_b_re, ssm_b_im, ssm_c_re, ssm_c_im, ssm_d, w_glu, b_glu, norm_attn_out_gain, norm_ssm_out_gain, w_out, norm_ffn_gain, w_router, w_exp_gate, w_exp_up, w_exp_down, norm_final_gain):
    batch, s, d = x.shape
    depth = w_ada.shape[0]
    assert batch == 1 and d == D_MODEL and depth == 1
    cap = max(1, CAPACITY_FACTOR * s // N_EXPERTS)
    h = x[0]
    pos = positions[0]
    cos_t, sin1_t, sin2_t = _rope_tables(pos)
    for layer in range(depth):
        mod = _adaln(c, w_ada[layer], b_ada[layer])
        sh_m, sc_m, g_m, sh_f, sc_f, g_f = jnp.split(mod, 6, axis=-1)

        xm = _norm_mod_call(h, norm_mix_gain[layer][None], sc_m, sh_m, BF16)
        qkv, qkv4, qkv16 = _qkv_proj(xm, w_in[layer, :, :3 * ATTN_WIDTH].astype(BF16), cos_t, sin1_t, sin2_t)
        outs, lses = [], []
        for dil, arr in zip(DILATIONS, (qkv[None], qkv4, qkv16)):
            o, l = _dilated_attention(arr, dil)
            outs.append(o)
            lses.append(l)
        attn_n = _merge_patterns(outs, lses, norm_attn_out_gain[layer][None])

        u_t = _u_proj(xm, w_in[layer, :, 3 * ATTN_WIDTH:].astype(BF16))
        w_t, p_all, q_all, a_all = _ssm_operators(
            lam_re_fwd[layer], lam_im_fwd[layer], log_dt_fwd[layer],
            lam_re_bwd[layer], lam_im_bwd[layer], log_dt_bwd[layer],
            ssm_b_re[layer], ssm_b_im[layer], ssm_c_re[layer], ssm_c_im[layer])
        d_t = jnp.tile(ssm_d[layer].reshape(SSM_GROUPS, 1, SSM_GROUP), (1, 1, SSM_CHUNK))
        g_t = _ssm_mixer(u_t, w_t, p_all, q_all, a_all, d_t)
        ssm_n = _glu_norm(g_t, w_glu[layer].astype(BF16), b_glu[layer][None], norm_ssm_out_gain[layer][None])

        h = _out_proj(attn_n, ssm_n, w_out[layer].astype(BF16), h, g_m)

        xf, logits_t = _ffn_prep(h, norm_ffn_gain[layer][None], sc_f, sh_f, jnp.transpose(w_router[layer]))
        sel, slot, aff = _route(logits_t, cap)
        idx, gates = _compact(sel, slot, aff, cap)
        hdn = _moe_up(idx, xf, w_exp_gate[layer], w_exp_up[layer], cap)
        y = _moe_down(hdn, w_exp_down[layer], gates)
        tm = 256
        lo = jnp.concatenate([slot[:, ::tm], jnp.full((N_EXPERTS, 1), cap, I32)], axis=1)
        h = _combine(h, y, jnp.transpose(sel), jnp.transpose(slot), lo, g_f, norm_final_gain[None], cap)
    return h[None]
```

```python
import functools
import math

import jax
import jax.numpy as jnp
from jax import lax
from jax.experimental import pallas as pl
from jax.experimental.pallas import tpu as pltpu

F32 = jnp.float32
BF16 = jnp.bfloat16
I32 = jnp.int32
HIGHEST = lax.Precision.HIGHEST

D_MODEL = 4096
ATTN_WIDTH = 2048
SSM_WIDTH = 2048
HEAD_DIM = 128
N_HEADS = 16
IN_PROJ_WIDTH = 3 * ATTN_WIDTH + SSM_WIDTH
ROPE_DIM = 32
ROPE_THETA = 500000.0
DILATIONS = (1, 4, 16)
ATTN_HALF = 64
ATTN_UNROLL = 15
NEG_INF = -1e30
SSM_GROUP = 16
SSM_GROUPS = 128
SSM_STATE = 64
SSM_CHUNK = 16
SSM_GB = 8
N_EXPERTS = 16
EXPERT_FF = 2048
CAPACITY_FACTOR = 2
EPS = 1e-6

LANES = 128
MIB = 1024 * 1024


def _params(semantics, vmem_mib):
    return pltpu.CompilerParams(dimension_semantics=semantics, vmem_limit_bytes=vmem_mib * MIB)


def _ada_kernel(c_ref, w_ref, b_ref, o_ref):
    c = c_ref[...]
    ca = jnp.broadcast_to(c * jax.nn.sigmoid(c), (c.shape[0], LANES))
    for sl in range(w_ref.shape[1] // LANES):
        cs = slice(sl * LANES, (sl + 1) * LANES)
        row = jnp.sum(w_ref[:, cs] * ca, axis=0, keepdims=True) + b_ref[:, cs]
        o_ref[:, cs] = jnp.broadcast_to(row, (o_ref.shape[0], LANES))


def _adaln(c, w_ada, b_ada):
    d, n = w_ada.shape
    tn = 512
    out = pl.pallas_call(
        _ada_kernel,
        grid=(n // tn,),
        in_specs=[pl.BlockSpec((d, 1), lambda j: (0, 0)),
                  pl.BlockSpec((d, tn), lambda j: (0, j)),
                  pl.BlockSpec((1, tn), lambda j: (0, j))],
        out_specs=pl.BlockSpec((8, tn), lambda j: (0, j)),
        out_shape=jax.ShapeDtypeStruct((8, n), F32),
        compiler_params=_params(("parallel",), 40),
        name="adaln",
    )(c.reshape(d, 1), w_ada, b_ada.reshape(1, n))
    return out[0:1]


def _norm_mod(x, gain, scale, shift):
    ms = jnp.mean(x * x, axis=-1, keepdims=True)
    y = x * lax.rsqrt(ms + EPS) * gain
    return y * (1.0 + scale) + shift


def _norm_mod_kernel(x_ref, g_ref, sc_ref, sh_ref, o_ref):
    o_ref[...] = _norm_mod(x_ref[...], g_ref[...], sc_ref[...], sh_ref[...]).astype(o_ref.dtype)


def _norm_mod_call(x, gain, scale, shift, out_dtype):
    s, d = x.shape
    tm = 256
    row = pl.BlockSpec((1, d), lambda i: (0, 0))
    return pl.pallas_call(
        _norm_mod_kernel,
        grid=(s // tm,),
        in_specs=[pl.BlockSpec((tm, d), lambda i: (i, 0)), row, row, row],
        out_specs=pl.BlockSpec((tm, d), lambda i: (i, 0)),
        out_shape=jax.ShapeDtypeStruct((s, d), out_dtype),
        compiler_params=_params(("parallel",), 40),
        name="norm_mod",
    )(x, gain, scale, shift)


def _rope_kernel(pos_ref, c_ref, s1_ref, s2_ref):
    half = ROPE_DIM // 2
    pos = pos_ref[...].astype(F32)
    lane = lax.broadcasted_iota(I32, (1, LANES), 1)
    fidx = (lane & (half - 1)).astype(F32)
    inv_freq = jnp.exp(fidx * (-math.log(ROPE_THETA) / half))
    ang = pos * inv_freq
    cs = jnp.cos(ang)
    sn = jnp.sin(ang)
    c_ref[...] = jnp.where(lane < ROPE_DIM, cs, 1.0)
    s1_ref[...] = jnp.where(lane < half, -sn, 0.0)
    s2_ref[...] = jnp.where((lane >= half) & (lane < ROPE_DIM), sn, 0.0)


def _rope_tables(positions):
    s = positions.shape[0]
    tm = 512
    spec = pl.BlockSpec((tm, LANES), lambda i: (i, 0))
    shp = jax.ShapeDtypeStruct((s, LANES), F32)
    return pl.pallas_call(
        _rope_kernel,
        grid=(s // tm,),
        in_specs=[pl.BlockSpec((tm, 1), lambda i: (i, 0))],
        out_specs=[spec, spec, spec],
        out_shape=[shp, shp, shp],
        compiler_params=_params(("parallel",), 32),
        name="rope_tables",
    )(positions.reshape(s, 1))


PROJ_TM = 1024
PROJ_TN = 512


def _piece_transpose8(xs):
    xs = list(xs)
    lane = lax.broadcasted_iota(I32, xs[0].shape, 1)
    for h in (4, 2, 1):
        low = (lane & (16 * h)) == 0
        for a in range(8):
            if a & h:
                continue
            lo, hi = xs[a], xs[a + h]
            xs[a] = jnp.where(low, lo, pltpu.roll(hi, 16 * h, 1))
            xs[a + h] = jnp.where(low, pltpu.roll(lo, LANES - 16 * h, 1), hi)
    return xs


PROJ_RC = 256


def _qkv_kernel(a_ref, w_ref, c_ref, s1_ref, s2_ref, nat_ref, p4_ref, p16_ref, slab, *, tm, tn):
    j = pl.program_id(1)
    rep = tn // HEAD_DIM
    half = ROPE_DIM // 2
    is_qk = j < 2 * ATTN_WIDTH // tn
    scale = jnp.where(j < ATTN_WIDTH // tn, HEAD_DIM ** -0.5 * math.log2(math.e), 1.0).astype(F32)
    rc = PROJ_RC
    for ch in range(tm // rc):
        rows = slice(ch * rc, (ch + 1) * rc)
        acc = jnp.dot(a_ref[rows, :], w_ref[...], preferred_element_type=F32)
        c = jnp.concatenate([jnp.where(is_qk, c_ref[rows, :], 1.0) * scale] * rep, axis=1)
        s1 = jnp.concatenate([jnp.where(is_qk, s1_ref[rows, :], 0.0) * scale] * rep, axis=1)
        s2 = jnp.concatenate([jnp.where(is_qk, s2_ref[rows, :], 0.0) * scale] * rep, axis=1)
        val = acc * c + pltpu.roll(acc, tn - half, 1) * s1 + pltpu.roll(acc, half, 1) * s2
        for cc in range(rep):
            piece = val[:, cc * LANES:(cc + 1) * LANES]
            nat_ref[cc, rows, :] = piece.astype(nat_ref.dtype)
            slab[cc, rows, :] = piece
        for d, ref in ((4, p4_ref), (16, p16_ref)):
            nr = rc // d
            for r in range(d):
                for cc in range(rep):
                    ref[r, cc, ch * nr:(ch + 1) * nr, :] = (
                        slab[cc, pl.ds(ch * rc + r, nr, stride=d), :].astype(ref.dtype))


def _qkv_proj(xm, w_qkv, cos_t, sin1_t, sin2_t):
    s, d = xm.shape
    n = w_qkv.shape[1]
    tm, tn = PROJ_TM, PROJ_TN
    nh, hpt = n // HEAD_DIM, tn // HEAD_DIM
    tab = pl.BlockSpec((tm, LANES), lambda i, j: (i, 0))
    return pl.pallas_call(
        functools.partial(_qkv_kernel, tm=tm, tn=tn),
        grid=(s // tm, n // tn),
        in_specs=[pl.BlockSpec((tm, d), lambda i, j: (i, 0)),
                  pl.BlockSpec((d, tn), lambda i, j: (0, j)),
                  tab, tab, tab],
        out_specs=[pl.BlockSpec((hpt, tm, HEAD_DIM), lambda i, j: (j, i, 0)),
                   pl.BlockSpec((4, hpt, tm // 4, HEAD_DIM), lambda i, j: (0, j, i, 0)),
                   pl.BlockSpec((16, hpt, tm // 16, HEAD_DIM), lambda i, j: (0, j, i, 0))],
        out_shape=[jax.ShapeDtypeStruct((nh, s, HEAD_DIM), BF16),
                   jax.ShapeDtypeStruct((4, nh, s // 4, HEAD_DIM), BF16),
                   jax.ShapeDtypeStruct((16, nh, s // 16, HEAD_DIM), BF16)],
        scratch_shapes=[pltpu.VMEM((tn // LANES, tm, LANES), F32)],
        compiler_params=_params(("parallel", "arbitrary"), 52),
        name="qkv_proj",
    )(xm, w_qkv, cos_t, sin1_t, sin2_t)


def _uproj_kernel(a_ref, w_ref, u_ref, slab, *, tm, tn):
    rc = PROJ_RC
    nchunk = rc // SSM_CHUNK
    for ch in range(tm // rc):
        acc = jnp.dot(a_ref[ch * rc:(ch + 1) * rc, :], w_ref[...], preferred_element_type=F32)
        out_rows = slice(ch * nchunk, (ch + 1) * nchunk)
        for c in range(tn // LANES):
            slab[c, ch * rc:(ch + 1) * rc, :] = acc[:, c * LANES:(c + 1) * LANES]
            rows = [slab[c, pl.ds(ch * rc + i, nchunk, stride=SSM_CHUNK), :]
                    for i in range(SSM_CHUNK)]
            first = _piece_transpose8(rows[:8])
            second = _piece_transpose8(rows[8:])
            for gp in range(8):
                u_ref[c * 8 + gp, out_rows, 0:LANES] = first[gp].astype(u_ref.dtype)
                u_ref[c * 8 + gp, out_rows, LANES:2 * LANES] = second[gp].astype(u_ref.dtype)


def _u_proj(xm, w_u):
    s, d = xm.shape
    n = w_u.shape[1]
    tm, tn = PROJ_TM, PROJ_TN
    gpt = tn // SSM_GROUP
    return pl.pallas_call(
        functools.partial(_uproj_kernel, tm=tm, tn=tn),
        grid=(s // tm, n // tn),
        in_specs=[pl.BlockSpec((tm, d), lambda i, j: (i, 0)),
                  pl.BlockSpec((d, tn), lambda i, j: (0, j))],
        out_specs=pl.BlockSpec((gpt, tm // SSM_CHUNK, SSM_CHUNK * SSM_GROUP), lambda i, j: (j, i, 0)),
        out_shape=jax.ShapeDtypeStruct((n // SSM_GROUP, s // SSM_CHUNK, SSM_CHUNK * SSM_GROUP), BF16),
        scratch_shapes=[pltpu.VMEM((tn // LANES, tm, LANES), F32)],
        compiler_params=_params(("parallel", "arbitrary"), 48),
        name="u_proj",
    )(xm, w_u)


def _attn_kernel(q_ref, k_ref, v_ref, o_ref, l_ref, *, n, hps):
    tq = LANES
    tk = 2 * LANES
    nsb = n // tq
    col_minus_row = lax.broadcasted_iota(I32, (tq, tk), 1) - lax.broadcasted_iota(I32, (tq, tk), 0)
    eye = lax.broadcasted_iota(I32, (tq, tq), 0) == lax.broadcasted_iota(I32, (tq, tq), 1)

    def band_bias(key_start_minus_q0):
        return jnp.where(jnp.abs(col_minus_row + key_start_minus_q0) <= ATTN_HALF, 0.0, NEG_INF).astype(F32)

    def block(hh, sb, q0, ks, bias):
        q = q_ref[hh, pl.ds(q0, tq), :]
        k = k_ref[hh, pl.ds(ks, tk), :]
        v = v_ref[hh, pl.ds(ks, tk), :]
        s = lax.dot_general(q, k, (((1,), (1,)), ((), ())), preferred_element_type=F32) + bias
        m = jnp.max(s, axis=-1, keepdims=True)
        p = jnp.exp2(s - m)
        l = jnp.sum(p, axis=-1, keepdims=True)
        o = jnp.dot(p.astype(BF16), v, preferred_element_type=F32) / l
        o_ref[hh, pl.ds(q0, tq), :] = o.astype(o_ref.dtype)
        lse = (m + jnp.log2(l)) * math.log(2.0)
        l_ref[hh, pl.ds(sb, 1), :] = jnp.sum(jnp.where(eye, lse, 0.0), axis=0, keepdims=True)

    interior = nsb - 2
    edge_first, edge_last = band_bias(0), band_bias(-tq)
    bias = band_bias(-ATTN_HALF)
    for hh in range(hps):
        block(hh, 0, 0, 0, edge_first)
        block(hh, nsb - 1, n - tq, n - tk, edge_last)
        if 0 < interior <= 6:
            for sb in range(1, nsb - 1):
                block(hh, sb, sb * tq, sb * tq - ATTN_HALF, bias)
        elif interior > 0:
            unroll = max(u for u in range(1, ATTN_UNROLL + 1) if interior % u == 0)

            def body(it, carry, hh=hh, unroll=unroll):
                for u in range(unroll):
                    sb = 1 + it * unroll + u
                    q0 = pl.multiple_of(sb * tq, tq)
                    block(hh, sb, q0, pl.multiple_of(q0 - ATTN_HALF, ATTN_HALF), bias)
                return carry

            lax.fori_loop(0, interior // unroll, body, 0)


def _dilated_attention(qkv, dil):
    _, _, n, _ = qkv.shape
    s = n * dil
    nsb = n // LANES
    hps = 2 if nsb <= 8 else 1

    def spec(off):
        return pl.BlockSpec((None, hps, n, HEAD_DIM), lambda r, h: (r, off // hps + h, 0, 0))

    o, lse = pl.pallas_call(
        functools.partial(_attn_kernel, n=n, hps=hps),
        grid=(dil, N_HEADS // hps),
        in_specs=[spec(0), spec(N_HEADS), spec(2 * N_HEADS)],
        out_specs=[pl.BlockSpec((None, hps, n, HEAD_DIM), lambda r, h: (r, h, 0, 0)),
                   pl.BlockSpec((hps, nsb, LANES), lambda r, h: (r * (N_HEADS // hps) + h, 0, 0))],
        out_shape=[jax.ShapeDtypeStruct((dil, N_HEADS, n, HEAD_DIM), BF16),
                   jax.ShapeDtypeStruct((dil * N_HEADS, nsb, LANES), F32)],
        compiler_params=_params(("parallel", "parallel"), 48),
        name=f"dilated_attn_d{dil}",
    )(qkv, qkv, qkv)
    lse = lse.reshape(dil, N_HEADS, n).transpose(2, 0, 1).reshape(s, N_HEADS)
    return o, lse


def _merge_kernel(o1_ref, o4_ref, o16_ref, l1_ref, l2_ref, l3_ref, g_ref, out_ref, acc_ref, s4_ref, s16_ref, *, tm):
    for d, src, dst in ((4, o4_ref, s4_ref), (16, o16_ref, s16_ref)):
        for r in range(d):
            for h in range(N_HEADS):
                dst[h, pl.ds(r, tm // d, stride=d), :] = src[r, h].astype(F32)
    la, lb, lc = l1_ref[...], l2_ref[...], l3_ref[...]
    m = jnp.maximum(jnp.maximum(la, lb), lc)
    ea, eb, ec = jnp.exp(la - m), jnp.exp(lb - m), jnp.exp(lc - m)
    den = ea + eb + ec
    wa, wb, wc = ea / den, eb / den, ec / den
    sq = jnp.zeros((tm, 1), F32)
    for h in range(N_HEADS):
        cs = slice(h * HEAD_DIM, (h + 1) * HEAD_DIM)
        slab = (wa[:, h:h + 1] * o1_ref[h].astype(F32)
                + wb[:, h:h + 1] * s4_ref[h]
                + wc[:, h:h + 1] * s16_ref[h])
        acc_ref[:, cs] = slab
        sq = sq + jnp.sum(slab * slab, axis=-1, keepdims=True)
    inv = lax.rsqrt(sq * (1.0 / ATTN_WIDTH) + EPS)
    out_ref[...] = (acc_ref[...] * inv * g_ref[...]).astype(out_ref.dtype)


def _merge_patterns(outs, lses, gain):
    o1, o4, o16 = outs
    s = o1.shape[2]
    tm = 256
    ospec = pl.BlockSpec((tm, ATTN_WIDTH), lambda i: (i, 0))
    lspec = pl.BlockSpec((tm, N_HEADS), lambda i: (i, 0))
    return pl.pallas_call(
        functools.partial(_merge_kernel, tm=tm),
        grid=(s // tm,),
        in_specs=[pl.BlockSpec((None, N_HEADS, tm, HEAD_DIM), lambda i: (0, 0, i, 0)),
                  pl.BlockSpec((4, N_HEADS, tm // 4, HEAD_DIM), lambda i: (0, 0, i, 0)),
                  pl.BlockSpec((16, N_HEADS, tm // 16, HEAD_DIM), lambda i: (0, 0, i, 0)),
                  lspec, lspec, lspec,
                  pl.BlockSpec((1, ATTN_WIDTH), lambda i: (0, 0))],
        out_specs=ospec,
        out_shape=jax.ShapeDtypeStruct((s, ATTN_WIDTH), BF16),
        scratch_shapes=[pltpu.VMEM((tm, ATTN_WIDTH), F32),
                        pltpu.VMEM((N_HEADS, tm, HEAD_DIM), F32),
                        pltpu.VMEM((N_HEADS, tm, HEAD_DIM), F32)],
        compiler_params=_params(("parallel",), 32),
        name="attn_merge_norm",
    )(o1, o4, o16, *lses, gain)


def _ssm_direction_terms(lam_re, lam_im, log_dt, b_re, b_im, c_re, c_im):
    dt = jnp.exp(log_dt)[:, None]
    mag = jnp.exp(lam_re * dt)
    ang = lam_im * dt
    lb_re = mag * jnp.cos(ang)
    lb_im = mag * jnp.sin(ang)
    den = lam_re * lam_re + lam_im * lam_im
    ar = lb_re - 1.0
    ai = lb_im
    coef_re = (ar * lam_re + ai * lam_im) / den
    coef_im = (ai * lam_re - ar * lam_im) / den
    bb_re = coef_re[..., None] * b_re - coef_im[..., None] * b_im
    bb_im = coef_re[..., None] * b_im + coef_im[..., None] * b_re
    tau = jnp.arange(SSM_CHUNK + 1, dtype=F32)[:, None, None]
    pw_mag = jnp.exp(tau * (lam_re * dt)[None])
    pw_re = pw_mag * jnp.cos(tau * ang[None])
    pw_im = pw_mag * jnp.sin(tau * ang[None])
    e_re = pw_re[..., None] * bb_re[None] - pw_im[..., None] * bb_im[None]
    e_im = pw_re[..., None] * bb_im[None] + pw_im[..., None] * bb_re[None]
    kern = (jnp.einsum('ghp,tgpi->tghi', c_re, e_re[:SSM_CHUNK], precision=HIGHEST)
            - jnp.einsum('ghp,tgpi->tghi', c_im, e_im[:SSM_CHUNK], precision=HIGHEST))
    ca_re = c_re[None] * pw_re[:, :, None, :] - c_im[None] * pw_im[:, :, None, :]
    ca_im = c_re[None] * pw_im[:, :, None, :] + c_im[None] * pw_re[:, :, None, :]
    return kern, e_re, e_im, ca_re, ca_im, pw_re[SSM_CHUNK], pw_im[SSM_CHUNK]


def _ssm_operators(lam_re_f, lam_im_f, log_dt_f, lam_re_b, lam_im_b, log_dt_b, b_re, b_im, c_re, c_im):
    L, G, P = SSM_CHUNK, SSM_GROUPS, SSM_STATE
    kf, ef_re, ef_im, caf_re, caf_im, af_re, af_im = _ssm_direction_terms(
        lam_re_f, lam_im_f, log_dt_f, b_re, b_im, c_re, c_im)
    kb, eb_re, eb_im, cab_re, cab_im, ab_re, ab_im = _ssm_direction_terms(
        lam_re_b, lam_im_b, log_dt_b, b_re, b_im, c_re, c_im)
    lag = jnp.arange(L)[None, :] - jnp.arange(L)[:, None]
    tau = jnp.arange(L)[:, None, None]
    place = jnp.concatenate([(lag[None] == tau), (-lag[None] == tau)], axis=0).astype(F32)
    w = jnp.einsum('tij,tgoh->gihjo', place, jnp.concatenate([kf, kb], axis=0), precision=HIGHEST)
    w = w.reshape(G, L * SSM_GROUP, L * SSM_GROUP)

    def p_mat(e, order):
        return jnp.transpose(e[order], (1, 0, 3, 2)).reshape(G, L * SSM_GROUP, P)

    def q_mat(ca, order):
        return jnp.transpose(ca[order], (1, 3, 0, 2)).reshape(G, P, L * SSM_GROUP)

    f_ord = (L - 1) - jnp.arange(L)
    b_ord = jnp.arange(L)
    parts_p = [p_mat(ef_re, f_ord), p_mat(ef_im, f_ord), p_mat(eb_re, b_ord), p_mat(eb_im, b_ord)]
    parts_q = [q_mat(caf_re, jnp.arange(L) + 1), -q_mat(caf_im, jnp.arange(L) + 1),
               q_mat(cab_re, L - jnp.arange(L)), -q_mat(cab_im, L - jnp.arange(L))]
    even = (jnp.arange(G) % 2 == 0)[:, None, None]

    def pad_p(x):
        z = jnp.zeros_like(x)
        return jnp.where(even, jnp.concatenate([x, z], -1), jnp.concatenate([z, x], -1))

    def pad_q(x):
        z = jnp.zeros_like(x)
        return jnp.where(even, jnp.concatenate([x, z], 1), jnp.concatenate([z, x], 1))

    p_all = jnp.stack([pad_p(x) for x in parts_p], axis=1).astype(BF16)
    q_all = jnp.stack([pad_q(x) for x in parts_q], axis=1).astype(BF16)

    def tiles(re, im):
        re = re.reshape(G // SSM_GB, SSM_GB // 2, 2 * P)
        im = im.reshape(G // SSM_GB, SSM_GB // 2, 2 * P)
        return jnp.concatenate([re, re], axis=1), jnp.concatenate([-im, im], axis=1)

    a_all = jnp.stack([*tiles(af_re, af_im), *tiles(ab_re, ab_im)], axis=1)
    return w.astype(BF16), p_all, q_all, a_all


def _gelu_tanh(x):
    return 0.5 * x * (1.0 + jnp.tanh(math.sqrt(2.0 / math.pi) * (x + 0.044715 * (x * x * x))))


SUBLANES = 8


def _ssm_kernel(u_ref, w_ref, p_ref, q_ref, a_ref, d_ref, y_ref, sf, sb, *, nc, gb):
    npair = gb // 2
    assert 2 * npair == SUBLANES
    for pr in range(npair):
        g0, g1 = 2 * pr, 2 * pr + 1
        for k, (scr, row) in enumerate(((sf, pr), (sf, npair + pr), (sb, pr), (sb, npair + pr))):
            contrib = (jnp.dot(u_ref[g0], p_ref[g0, k], preferred_element_type=F32)
                       + jnp.dot(u_ref[g1], p_ref[g1, k], preferred_element_type=F32))
            scr[pl.ds(row, nc, stride=SUBLANES), :] = contrib

    a1f, a2f, a1b, a2b = a_ref[0, 0], a_ref[0, 1], a_ref[0, 2], a_ref[0, 3]

    def step(c, carry):
        hf, hb = carry
        cf = pl.multiple_of(c * SUBLANES, SUBLANES)
        cb = pl.multiple_of((nc - 1 - c) * SUBLANES, SUBLANES)
        rf = sf[pl.ds(cf, SUBLANES), :]
        rb = sb[pl.ds(cb, SUBLANES), :]
        sf[pl.ds(cf, SUBLANES), :] = hf
        sb[pl.ds(cb, SUBLANES), :] = hb
        return (a1f * hf + a2f * pltpu.roll(hf, npair, 0) + rf,
                a1b * hb + a2b * pltpu.roll(hb, npair, 0) + rb)

    z = jnp.zeros((SUBLANES, LANES), F32)
    lax.fori_loop(0, nc, step, (z, z))

    for pr in range(npair):
        states = [scr[pl.ds(row, nc, stride=SUBLANES), :].astype(BF16)
                  for scr, row in ((sf, pr), (sf, npair + pr), (sb, pr), (sb, npair + pr))]
        for g in (2 * pr, 2 * pr + 1):
            u = u_ref[g]
            y = jnp.dot(u, w_ref[g], preferred_element_type=F32)
            for k in range(4):
                y = y + jnp.dot(states[k], q_ref[g, k], preferred_element_type=F32)
            y = y + d_ref[g] * u.astype(F32)
            y_ref[g] = _gelu_tanh(y).astype(y_ref.dtype)


def _ssm_mixer(u_t, w, p_all, q_all, a_all, d_t):
    g, nc, k = u_t.shape
    gb = SSM_GB
    return pl.pallas_call(
        functools.partial(_ssm_kernel, nc=nc, gb=gb),
        grid=(g // gb,),
        in_specs=[pl.BlockSpec((gb, nc, k), lambda i: (i, 0, 0)),
                  pl.BlockSpec((gb, k, k), lambda i: (i, 0, 0)),
                  pl.BlockSpec((gb, 4, k, LANES), lambda i: (i, 0, 0, 0)),
                  pl.BlockSpec((gb, 4, LANES, k), lambda i: (i, 0, 0, 0)),
                  pl.BlockSpec((1, 4, SUBLANES, LANES), lambda i: (i, 0, 0, 0)),
                  pl.BlockSpec((gb, 1, k), lambda i: (i, 0, 0))],
        out_specs=pl.BlockSpec((gb, nc, k), lambda i: (i, 0, 0)),
        out_shape=jax.ShapeDtypeStruct((g, nc, k), BF16),
        scratch_shapes=[pltpu.VMEM((nc * SUBLANES, LANES), F32)] * 2,
        compiler_params=_params(("parallel",), 48),
        name="ssm_scan",
    )(u_t, w, p_all, q_all, a_all, d_t)


def _glu_kernel(g_ref, w_ref, b_ref, n_ref, o_ref, nat, gb, *, tm):
    nchunk = tm // SSM_CHUNK
    for sl in range(SSM_WIDTH // LANES):
        for half in range(2):
            xs = [g_ref[8 * sl + gp, :, half * LANES:(half + 1) * LANES].astype(F32) for gp in range(8)]
            ys = _piece_transpose8(xs)
            for jj in range(8):
                nat[sl, pl.ds(half * 8 + jj, nchunk, stride=SSM_CHUNK), :] = ys[jj]
        gb[:, sl * LANES:(sl + 1) * LANES] = nat[sl].astype(BF16)
    z = jnp.dot(gb[...], w_ref[...], preferred_element_type=F32) + b_ref[...]
    sq = jnp.zeros((tm, 1), F32)
    for sl in range(SSM_WIDTH // LANES):
        cs = slice(sl * LANES, (sl + 1) * LANES)
        out = nat[sl] * jax.nn.sigmoid(z[:, cs])
        nat[sl] = out
        sq = sq + jnp.sum(out * out, axis=-1, keepdims=True)
    inv = lax.rsqrt(sq * (1.0 / SSM_WIDTH) + EPS)
    for sl in range(SSM_WIDTH // LANES):
        cs = slice(sl * LANES, (sl + 1) * LANES)
        o_ref[:, cs] = (nat[sl] * inv * n_ref[:, cs]).astype(o_ref.dtype)


def _glu_norm(g_t, w_glu, b_glu, gain):
    ng, nc, k = g_t.shape
    s, d = nc * SSM_CHUNK, ng * SSM_GROUP
    tm = 512
    row = pl.BlockSpec((1, d), lambda i: (0, 0))
    return pl.pallas_call(
        functools.partial(_glu_kernel, tm=tm),
        grid=(s // tm,),
        in_specs=[pl.BlockSpec((ng, tm // SSM_CHUNK, k), lambda i: (0, i, 0)),
                  pl.BlockSpec((d, d), lambda i: (0, 0)), row, row],
        out_specs=pl.BlockSpec((tm, d), lambda i: (i, 0)),
        out_shape=jax.ShapeDtypeStruct((s, d), BF16),
        scratch_shapes=[pltpu.VMEM((d // LANES, tm, LANES), F32),
                        pltpu.VMEM((tm, d), BF16)],
        compiler_params=_params(("parallel",), 48),
        name="ssm_glu_norm",
    )(g_t, w_glu, b_glu, gain)


def _outproj_kernel(a1_ref, a2_ref, w1_ref, w2_ref, x_ref, gm_ref, o_ref):
    mix = (jnp.dot(a1_ref[...], w1_ref[...], preferred_element_type=F32)
           + jnp.dot(a2_ref[...], w2_ref[...], preferred_element_type=F32))
    o_ref[...] = x_ref[...] + gm_ref[...] * mix


def _out_proj(attn_n, ssm_n, w_out, x, g_m):
    s, k = attn_n.shape
    n = w_out.shape[1]
    tm, tn = 1024, 512
    return pl.pallas_call(
        _outproj_kernel,
        grid=(s // tm, n // tn),
        in_specs=[pl.BlockSpec((tm, k), lambda i, j: (i, 0)),
                  pl.BlockSpec((tm, k), lambda i, j: (i, 0)),
                  pl.BlockSpec((k, tn), lambda i, j: (0, j)),
                  pl.BlockSpec((k, tn), lambda i, j: (1, j)),
                  pl.BlockSpec((tm, tn), lambda i, j: (i, j)),
                  pl.BlockSpec((1, tn), lambda i, j: (0, j))],
        out_specs=pl.BlockSpec((tm, tn), lambda i, j: (i, j)),
        out_shape=jax.ShapeDtypeStruct((s, n), F32),
        compiler_params=_params(("parallel", "arbitrary"), 48),
        name="out_proj_residual",
    )(attn_n, ssm_n, w_out, w_out, x, g_m)


def _ffn_prep_kernel(h_ref, g_ref, sc_ref, sh_ref, wr_ref, xf_ref, lg_ref):
    xf = _norm_mod(h_ref[...], g_ref[...], sc_ref[...], sh_ref[...])
    half = xf.shape[1] // 2
    lo = pltpu.bitcast(xf[:, :half].astype(BF16).astype(F32), jnp.uint32)
    hi = pltpu.bitcast(xf[:, half:].astype(BF16).astype(F32), jnp.uint32)
    xf_ref[...] = (lo >> 16) | hi
    lg_ref[...] = lax.dot_general(wr_ref[...], xf, (((1,), (1,)), ((), ())),
                                  precision=HIGHEST, preferred_element_type=F32)


def _ffn_prep(h1, gain, scale, shift, w_router_t):
    s, d = h1.shape
    tm = 256
    row = pl.BlockSpec((1, d), lambda i: (0, 0))
    return pl.pallas_call(
        _ffn_prep_kernel,
        grid=(s // tm,),
        in_specs=[pl.BlockSpec((tm, d), lambda i: (i, 0)), row, row, row,
                  pl.BlockSpec((N_EXPERTS, d), lambda i: (0, 0))],
        out_specs=[pl.BlockSpec((tm, d // 2), lambda i: (i, 0)),
                   pl.BlockSpec((N_EXPERTS, tm), lambda i: (0, i))],
        out_shape=[jax.ShapeDtypeStruct((s, d // 2), jnp.uint32),
                   jax.ShapeDtypeStruct((N_EXPERTS, s), F32)],
        compiler_params=_params(("parallel",), 40),
        name="ffn_norm_router",
    )(h1, gain, scale, shift, w_router_t)


def _route_kernel(lg_ref, sel_ref, pos_ref, aff_ref, *, cap, nk):
    e_n = N_EXPERTS
    lg = lg_ref[...]
    mx = jnp.max(lg, axis=0, keepdims=True)
    ex = jnp.exp(lg - mx)
    aff = ex / jnp.sum(ex, axis=0, keepdims=True)
    aff_ref[...] = aff
    bits = pltpu.bitcast(aff, I32)

    def count(mask):
        return jnp.sum(jnp.sum(mask.astype(F32), axis=2, keepdims=True), axis=1, keepdims=True)

    def radix(i, thr):
        cand = thr | jnp.left_shift(jnp.int32(1), 30 - i)
        return jnp.where(count(bits >= cand) >= cap, cand, thr)

    thr = lax.fori_loop(0, 31, radix, jnp.zeros((e_n, 1, 1), I32))
    gt = bits > thr
    eq = bits == thr
    need = cap - count(gt)

    li = lax.broadcasted_iota(I32, (LANES, LANES), 0)
    lj = lax.broadcasted_iota(I32, (LANES, LANES), 1)
    tri_lane = (li < lj).astype(BF16)
    ki = lax.broadcasted_iota(I32, (nk, nk), 0)
    kj = lax.broadcasted_iota(I32, (nk, nk), 1)
    tri_row = (kj < ki).astype(BF16)

    def excl_prefix(mask):
        mf = mask.astype(F32)
        within = jnp.dot(mf.reshape(e_n * nk, LANES).astype(BF16), tri_lane,
                         preferred_element_type=F32).reshape(e_n, nk, LANES)
        rowtot = jnp.broadcast_to(jnp.sum(mf, axis=2, keepdims=True), (e_n, nk, LANES)).astype(BF16)
        across = jnp.stack([jnp.dot(tri_row, rowtot[e], preferred_element_type=F32) for e in range(e_n)])
        return within + across

    sel = gt | (eq & (excl_prefix(eq) < need))
    sel_ref[...] = sel.astype(I32)
    pos_ref[...] = excl_prefix(sel).astype(I32)


def _route(logits_t, cap):
    e_n, s = logits_t.shape
    nk = s // LANES
    spec = pl.BlockSpec((e_n, nk, LANES), lambda i: (0, 0, 0))
    sel, pos, aff = pl.pallas_call(
        functools.partial(_route_kernel, cap=cap, nk=nk),
        grid=(1,),
        in_specs=[spec],
        out_specs=[spec, spec, spec],
        out_shape=[jax.ShapeDtypeStruct((e_n, nk, LANES), I32),
                   jax.ShapeDtypeStruct((e_n, nk, LANES), I32),
                   jax.ShapeDtypeStruct((e_n, nk, LANES), F32)],
        compiler_params=_params(("arbitrary",), 48),
        name="expert_choice_route",
    )(logits_t.reshape(e_n, nk, LANES))
    return sel.reshape(e_n, s), pos.reshape(e_n, s), aff.reshape(e_n, s)


def _compact_kernel(off_ref, sel_ref, pos_ref, aff_ref, idx_ref, gate_ref, *, nk):
    e = pl.program_id(0)
    idx_ref[...] = jnp.zeros(idx_ref.shape, I32)
    gate_ref[...] = jnp.zeros(gate_ref.shape, F32)
    local = lax.broadcasted_iota(I32, (LANES, LANES), 0)
    lane = lax.broadcasted_iota(I32, (LANES, LANES), 1)

    unroll = 4 if nk % 4 == 0 else 1

    def body(it, carry):
        for u in range(unroll):
            k = it * unroll + u
            off = off_ref[e, k]
            sel = sel_ref[0, pl.ds(k, 1), :]
            pos = pos_ref[0, pl.ds(k, 1), :]
            aff = aff_ref[0, pl.ds(k, 1), :]
            hit = (sel > 0) & ((pos - off) == local)
            tok = jnp.sum(jnp.where(hit, (lane + k * LANES).astype(F32), 0.0), axis=1, keepdims=True)
            gat = jnp.sum(jnp.where(hit, aff, 0.0), axis=1, keepdims=True)
            idx_ref[0, pl.ds(off, LANES), :] = tok.astype(I32)
            gate_ref[0, pl.ds(off, LANES), :] = gat
        return carry

    lax.fori_loop(0, nk // unroll, body, 0)


def _compact(sel, pos, aff, cap):
    e_n, s = sel.shape
    nk = s // LANES
    capp = cap + LANES
    offs = pos[:, ::LANES]
    tok = pl.BlockSpec((1, nk, LANES), lambda e, off: (e, 0, 0))
    col = pl.BlockSpec((1, capp, 1), lambda e, off: (e, 0, 0))
    grid_spec = pltpu.PrefetchScalarGridSpec(
        num_scalar_prefetch=1, grid=(e_n,), in_specs=[tok, tok, tok], out_specs=[col, col])
    idx, gate = pl.pallas_call(
        functools.partial(_compact_kernel, nk=nk),
        grid_spec=grid_spec,
        out_shape=[jax.ShapeDtypeStruct((e_n, capp, 1), I32),
                   jax.ShapeDtypeStruct((e_n, capp, 1), F32)],
        compiler_params=_params(("arbitrary",), 32),
        name="expert_compact",
    )(offs, sel.reshape(e_n, nk, LANES), pos.reshape(e_n, nk, LANES), aff.reshape(e_n, nk, LANES))
    return idx[:, :cap, 0], gate[:, :cap, :]


def _moe_up_kernel(idx_hbm, xf_hbm, wg_ref, wu_ref, h_ref, xbuf, idx_s, sem_x, sem_i, *, cap, tm, rps):
    nf, nh = pl.num_programs(1), pl.num_programs(2)
    phase = pl.program_id(0)
    n_phase = pl.num_programs(0)
    f = pl.program_id(1) * nh + pl.program_id(2)
    nf = nf * nh
    step = phase * nf + f
    b = phase % 2
    row0 = pl.multiple_of(pl.program_id(2) * tm, tm)

    def row_copy(token, buf, r):
        return pltpu.make_async_copy(xf_hbm.at[pl.ds(token, 1), :], xbuf.at[buf, pl.ds(r, 1), :], sem_x.at[buf])

    def wait_rows(buf, n):
        pltpu.make_async_copy(xf_hbm.at[pl.ds(0, n), :], xbuf.at[buf, pl.ds(0, n), :], sem_x.at[buf]).wait()

    @pl.when(step == 0)
    def _prologue():
        c_idx = pltpu.make_async_copy(idx_hbm, idx_s, sem_i.at[0])
        c_idx.start()
        c_idx.wait()

        def one(r, c):
            row_copy(idx_s[r], 0, r).start()
            return c
        lax.fori_loop(0, cap, one, 0)
        wait_rows(0, cap)

    @pl.when(step > 0)
    def _():
        wait_rows(jnp.where(f == 0, b, 1 - b), rps)

    nxt = jnp.minimum(phase + 1, n_phase - 1)
    base = nxt * cap + f * rps
    for r in range(rps):
        row_copy(idx_s[base + r], 1 - b, f * rps + r).start()

    wg = wg_ref[0].astype(BF16)
    wu = wu_ref[0].astype(BF16)
    half = wg.shape[0] // 2
    rc = MOE_RC
    for ch in range(tm // rc):
        x = xbuf[b, pl.ds(row0 + ch * rc, rc), :]
        x_lo = pltpu.bitcast(x << 16, F32).astype(BF16)
        x_hi = pltpu.bitcast(x & jnp.uint32(0xFFFF0000), F32).astype(BF16)
        g = (jnp.dot(x_lo, wg[:half], preferred_element_type=F32)
             + jnp.dot(x_hi, wg[half:], preferred_element_type=F32))
        u = (jnp.dot(x_lo, wu[:half], preferred_element_type=F32)
             + jnp.dot(x_hi, wu[half:], preferred_element_type=F32))
        h_ref[0, ch * rc:(ch + 1) * rc, :] = (g * jax.nn.sigmoid(g) * u).astype(h_ref.dtype)

    @pl.when(step == n_phase * nf - 1)
    def _drain():
        wait_rows(1 - b, rps)


MOE_RC = 256


def _moe_up(idx, xf_packed, w_gate, w_up, cap):
    e_n, d, ff = w_gate.shape
    tf = 256
    tm = min(1024, cap)
    nf, nh = ff // tf, cap // tm
    rps = cap // (nf * nh)
    assert rps * nf * nh == cap and tm % MOE_RC == 0
    return pl.pallas_call(
        functools.partial(_moe_up_kernel, cap=cap, tm=tm, rps=rps),
        grid=(e_n, nf, nh),
        in_specs=[pl.BlockSpec(memory_space=pl.ANY),
                  pl.BlockSpec(memory_space=pl.ANY),
                  pl.BlockSpec((1, d, tf), lambda e, f, m: (e, 0, f)),
                  pl.BlockSpec((1, d, tf), lambda e, f, m: (e, 0, f))],
        out_specs=pl.BlockSpec((1, tm, tf), lambda e, f, m: (e, m, f)),
        out_shape=jax.ShapeDtypeStruct((e_n, cap, ff), BF16),
        scratch_shapes=[pltpu.VMEM((2, cap, d // 2), jnp.uint32),
                        pltpu.SMEM((e_n * cap,), I32),
                        pltpu.SemaphoreType.DMA((2,)),
                        pltpu.SemaphoreType.DMA((1,))],
        compiler_params=_params(("arbitrary", "arbitrary", "arbitrary"), 60),
        name="moe_gather_up",
    )(idx.reshape(e_n * cap), xf_packed, w_gate, w_up)


def _moe_down_kernel(h_ref, wd_ref, gate_ref, y_ref):
    y = jnp.dot(h_ref[0], wd_ref[0].astype(BF16), preferred_element_type=F32)
    y_ref[0] = (y * gate_ref[0]).astype(y_ref.dtype)


def _moe_down(hdn, w_down, gates):
    e_n, cap, ff = hdn.shape
    d = w_down.shape[2]
    tn = 512
    return pl.pallas_call(
        _moe_down_kernel,
        grid=(e_n, d // tn),
        in_specs=[pl.BlockSpec((1, cap, ff), lambda e, n: (e, 0, 0)),
                  pl.BlockSpec((1, ff, tn), lambda e, n: (e, 0, n)),
                  pl.BlockSpec((1, cap, 1), lambda e, n: (e, 0, 0))],
        out_specs=pl.BlockSpec((1, cap, tn), lambda e, n: (e, 0, n)),
        out_shape=jax.ShapeDtypeStruct((e_n, cap, d), BF16),
        compiler_params=_params(("parallel", "arbitrary"), 48),
        name="moe_down",
    )(hdn, w_down, gates)


COMBINE_WINDOW = 64
SLOT_ALIGN = 16


def _combine_kernel(lo_ref, h_ref, sel_ref, pos_ref, gf_ref, gn_ref, y_hbm, o_ref,
                    ybuf, yextra, acc_ref, sem, sem_x, *, cap, tm):
    i = pl.program_id(0)
    win = COMBINE_WINDOW
    e_n = N_EXPERTS

    n_steps = pl.num_programs(0)
    cur = i % 2

    def window(e, j, step=None):
        step = i if step is None else step
        nominal = (lo_ref[e, step] // SLOT_ALIGN) * SLOT_ALIGN + j * win
        start = pl.multiple_of(jnp.minimum(nominal, cap - win), SLOT_ALIGN)
        return nominal, start

    def first_copy(e, step, buf):
        _, start = window(e, 0, step)
        return pltpu.make_async_copy(y_hbm.at[e, pl.ds(start, win), :],
                                     ybuf.at[buf, pl.ds(e * win, win), :], sem.at[buf, e])

    @pl.when(i == 0)
    def _():
        for e in range(e_n):
            first_copy(e, 0, 0).start()

    @pl.when(i + 1 < n_steps)
    def _():
        for e in range(e_n):
            first_copy(e, i + 1, 1 - cur).start()

    sel = sel_ref[...]
    pos = pos_ref[...]
    lane = lax.broadcasted_iota(I32, (tm, LANES), 1)
    left = lane < win
    pieces = []
    for e2 in range(e_n // 2):
        ea, eb = 2 * e2, 2 * e2 + 1
        na, sa = window(ea, 0)
        nb, sb = window(eb, 0)
        slot = jnp.where(left, sa + lane, sb + lane - win)
        nominal = jnp.where(left, na, nb)
        pe = jnp.where(left, pos[:, ea:ea + 1], pos[:, eb:eb + 1])
        se = jnp.where(left, sel[:, ea:ea + 1], sel[:, eb:eb + 1])
        pieces.append(((pe == slot) & (se > 0) & (slot >= nominal)).astype(BF16))
    onehot = jnp.concatenate(pieces, axis=1)
    for e in range(e_n):
        first_copy(e, i, cur).wait()
    acc_ref[...] = jnp.dot(onehot, ybuf[cur], preferred_element_type=F32)

    lane_w = lax.broadcasted_iota(I32, (tm, win), 1)
    for e in range(e_n):
        base = (lo_ref[e, i] // SLOT_ALIGN) * SLOT_ALIGN
        nwin = (lo_ref[e, i + 1] - base + win - 1) // win

        def extra(j, c, e=e):
            nominal, start = window(e, j)
            cp = pltpu.make_async_copy(y_hbm.at[e, pl.ds(start, win), :], yextra, sem_x.at[0])
            cp.start()
            cp.wait()
            slot = start + lane_w
            oh = ((pos[:, e:e + 1] == slot) & (sel[:, e:e + 1] > 0) & (slot >= nominal)).astype(BF16)
            acc_ref[...] += jnp.dot(oh, yextra[...], preferred_element_type=F32)
            return c

        lax.fori_loop(1, nwin, extra, 0)

    h2 = h_ref[...] + gf_ref[...] * acc_ref[...]
    ms = jnp.mean(h2 * h2, axis=-1, keepdims=True)
    o_ref[...] = h2 * lax.rsqrt(ms + EPS) * gn_ref[...]


def _combine(h1, y, sel_tm, pos_tm, lo, g_f, gain, cap):
    s, d = h1.shape
    tm = 256
    win = COMBINE_WINDOW
    row = pl.BlockSpec((1, d), lambda i, lo_r: (0, 0))
    tok = pl.BlockSpec((tm, N_EXPERTS), lambda i, lo_r: (i, 0))
    grid_spec = pltpu.PrefetchScalarGridSpec(
        num_scalar_prefetch=1,
        grid=(s // tm,),
        in_specs=[pl.BlockSpec((tm, d), lambda i, lo_r: (i, 0)), tok, tok, row, row,
                  pl.BlockSpec(memory_space=pl.ANY)],
        out_specs=pl.BlockSpec((tm, d), lambda i, lo_r: (i, 0)),
        scratch_shapes=[pltpu.VMEM((2, N_EXPERTS * win, d), BF16),
                        pltpu.VMEM((win, d), BF16),
                        pltpu.VMEM((tm, d), F32),
                        pltpu.SemaphoreType.DMA((2, N_EXPERTS)),
                        pltpu.SemaphoreType.DMA((1,))],
    )
    return pl.pallas_call(
        functools.partial(_combine_kernel, cap=cap, tm=tm),
        grid_spec=grid_spec,
        out_shape=jax.ShapeDtypeStruct((s, d), F32),
        compiler_params=_params(("arbitrary",), 56),
        name="moe_combine_norm",
    )(lo, h1, sel_tm, pos_tm, g_f, gain, y)


def kernel(x, c, positions, w_ada, b_ada, norm_mix_gain, w_in, lam_re_fwd, lam_im_fwd, log_dt_fwd, lam_re_bwd, lam_im_bwd, log_dt_bwd, ssm_b_re, ssm_b_im, ssm_c_re, ssm_c_im, ssm_d, w_glu, b_glu, norm_attn_out_gain, norm_ssm_out_gain, w_out, norm_ffn_gain, w_router, w_exp_gate, w_exp_up, w_exp_down, norm_final_gain):
    batch, s, d = x.shape
    depth = w_ada.shape[0]
    assert batch == 1 and d == D_MODEL and depth == 1
    cap = max(1, CAPACITY_FACTOR * s // N_EXPERTS)
    h = x[0]
    pos = positions[0]
    cos_t, sin1_t, sin2_t = _rope_tables(pos)
    for layer in range(depth):
        mod = _adaln(c, w_ada[layer], b_ada[layer])
        sh_m, sc_m, g_m, sh_f, sc_f, g_f = jnp.split(mod, 6, axis=-1)

        xm = _norm_mod_call(h, norm_mix_gain[layer][None], sc_m, sh_m, BF16)
        qkv, qkv4, qkv16 = _qkv_proj(xm, w_in[layer, :, :3 * ATTN_WIDTH].astype(BF16), cos_t, sin1_t, sin2_t)
        outs, lses = [], []
        for dil, arr in zip(DILATIONS, (qkv[None], qkv4, qkv16)):
            o, l = _dilated_attention(arr, dil)
            outs.append(o)
            lses.append(l)
        attn_n = _merge_patterns(outs, lses, norm_attn_out_gain[layer][None])

        u_t = _u_proj(xm, w_in[layer, :, 3 * ATTN_WIDTH:].astype(BF16))
        w_t, p_all, q_all, a_all = _ssm_operators(
            lam_re_fwd[layer], lam_im_fwd[layer], log_dt_fwd[layer],
            lam_re_bwd[layer], lam_im_bwd[layer], log_dt_bwd[layer],
            ssm_b_re[layer], ssm_b_im[layer], ssm_c_re[layer], ssm_c_im[layer])
        d_t = jnp.tile(ssm_d[layer].reshape(SSM_GROUPS, 1, SSM_GROUP), (1, 1, SSM_CHUNK))
        g_t = _ssm_mixer(u_t, w_t, p_all, q_all, a_all, d_t)
        ssm_n = _glu_norm(g_t, w_glu[layer].astype(BF16), b_glu[layer][None], norm_ssm_out_gain[layer][None])

        h = _out_proj(attn_n, ssm_n, w_out[layer].astype(BF16), h, g_m)

        xf, logits_t = _ffn_prep(h, norm_ffn_gain[layer][None], sc_f, sh_f, jnp.transpose(w_router[layer]))
        sel, slot, aff = _route(logits_t, cap)
        idx, gates = _compact(sel, slot, aff, cap)
        hdn = _moe_up(idx, xf, w_exp_gate[layer], w_exp_up[layer], cap)
        y = _moe_down(hdn, w_exp_down[layer], gates)
        tm = 256
        lo = jnp.concatenate([slot[:, ::tm], jnp.full((N_EXPERTS, 1), cap, I32)], axis=1)
        h = _combine(h, y, jnp.transpose(sel), jnp.transpose(slot), lo, g_f, norm_final_gain[None], cap)
    return h[None]
```

```python
import functools
import math

import jax
import jax.numpy as jnp
from jax import lax
from jax.experimental import pallas as pl
from jax.experimental.pallas import tpu as pltpu

F32 = jnp.float32
BF16 = jnp.bfloat16
I32 = jnp.int32
HIGHEST = lax.Precision.HIGHEST

D_MODEL = 4096
ATTN_WIDTH = 2048
SSM_WIDTH = 2048
HEAD_DIM = 128
N_HEADS = 16
IN_PROJ_WIDTH = 3 * ATTN_WIDTH + SSM_WIDTH
ROPE_DIM = 32
ROPE_THETA = 500000.0
DILATIONS = (1, 4, 16)
ATTN_HALF = 64
ATTN_UNROLL = 15
NEG_INF = -1e30
SSM_GROUP = 16
SSM_GROUPS = 128
SSM_STATE = 64
SSM_CHUNK = 16
SSM_GB = 8
N_EXPERTS = 16
EXPERT_FF = 2048
CAPACITY_FACTOR = 2
EPS = 1e-6

LANES = 128
MIB = 1024 * 1024


def _params(semantics, vmem_mib):
    return pltpu.CompilerParams(dimension_semantics=semantics, vmem_limit_bytes=vmem_mib * MIB)


def _ada_kernel(c_ref, w_ref, b_ref, o_ref):
    c = c_ref[...]
    ca = jnp.broadcast_to(c * jax.nn.sigmoid(c), (c.shape[0], LANES))
    for sl in range(w_ref.shape[1] // LANES):
        cs = slice(sl * LANES, (sl + 1) * LANES)
        row = jnp.sum(w_ref[:, cs] * ca, axis=0, keepdims=True) + b_ref[:, cs]
        o_ref[:, cs] = jnp.broadcast_to(row, (o_ref.shape[0], LANES))


def _adaln(c, w_ada, b_ada):
    d, n = w_ada.shape
    tn = 512
    out = pl.pallas_call(
        _ada_kernel,
        grid=(n // tn,),
        in_specs=[pl.BlockSpec((d, 1), lambda j: (0, 0)),
                  pl.BlockSpec((d, tn), lambda j: (0, j)),
                  pl.BlockSpec((1, tn), lambda j: (0, j))],
        out_specs=pl.BlockSpec((8, tn), lambda j: (0, j)),
        out_shape=jax.ShapeDtypeStruct((8, n), F32),
        compiler_params=_params(("parallel",), 40),
        name="adaln",
    )(c.reshape(d, 1), w_ada, b_ada.reshape(1, n))
    return out[0:1]


def _norm_mod(x, gain, scale, shift):
    ms = jnp.mean(x * x, axis=-1, keepdims=True)
    y = x * lax.rsqrt(ms + EPS) * gain
    return y * (1.0 + scale) + shift


def _norm_mod_kernel(x_ref, g_ref, sc_ref, sh_ref, o_ref):
    o_ref[...] = _norm_mod(x_ref[...], g_ref[...], sc_ref[...], sh_ref[...]).astype(o_ref.dtype)


def _norm_mod_call(x, gain, scale, shift, out_dtype):
    s, d = x.shape
    tm = 256
    row = pl.BlockSpec((1, d), lambda i: (0, 0))
    return pl.pallas_call(
        _norm_mod_kernel,
        grid=(s // tm,),
        in_specs=[pl.BlockSpec((tm, d), lambda i: (i, 0)), row, row, row],
        out_specs=pl.BlockSpec((tm, d), lambda i: (i, 0)),
        out_shape=jax.ShapeDtypeStruct((s, d), out_dtype),
        compiler_params=_params(("parallel",), 40),
        name="norm_mod",
    )(x, gain, scale, shift)


def _rope_kernel(pos_ref, c_ref, s1_ref, s2_ref):
    half = ROPE_DIM // 2
    pos = pos_ref[...].astype(F32)
    lane = lax.broadcasted_iota(I32, (1, LANES), 1)
    fidx = (lane & (half - 1)).astype(F32)
    inv_freq = jnp.exp(fidx * (-math.log(ROPE_THETA) / half))
    ang = pos * inv_freq
    cs = jnp.cos(ang)
    sn = jnp.sin(ang)
    c_ref[...] = jnp.where(lane < ROPE_DIM, cs, 1.0)
    s1_ref[...] = jnp.where(lane < half, -sn, 0.0)
    s2_ref[...] = jnp.where((lane >= half) & (lane < ROPE_DIM), sn, 0.0)


def _rope_tables(positions):
    s = positions.shape[0]
    tm = 512
    spec = pl.BlockSpec((tm, LANES), lambda i: (i, 0))
    shp = jax.ShapeDtypeStruct((s, LANES), F32)
    return pl.pallas_call(
        _rope_kernel,
        grid=(s // tm,),
        in_specs=[pl.BlockSpec((tm, 1), lambda i: (i, 0))],
        out_specs=[spec, spec, spec],
        out_shape=[shp, shp, shp],
        compiler_params=_params(("parallel",), 32),
        name="rope_tables",
    )(positions.reshape(s, 1))


PROJ_TM = 1024
PROJ_TN = 512


def _piece_transpose8(xs):
    xs = list(xs)
    lane = lax.broadcasted_iota(I32, xs[0].shape, 1)
    for h in (4, 2, 1):
        low = (lane & (16 * h)) == 0
        for a in range(8):
            if a & h:
                continue
            lo, hi = xs[a], xs[a + h]
            xs[a] = jnp.where(low, lo, pltpu.roll(hi, 16 * h, 1))
            xs[a + h] = jnp.where(low, pltpu.roll(lo, LANES - 16 * h, 1), hi)
    return xs


PROJ_RC = 256


def _qkv_kernel(a_ref, w_ref, c_ref, s1_ref, s2_ref, nat_ref, p4_ref, p16_ref, slab, slab4, *, tm, tn):
    j = pl.program_id(1)
    rep = tn // HEAD_DIM
    half = ROPE_DIM // 2
    is_qk = j < 2 * ATTN_WIDTH // tn
    scale = jnp.where(j < ATTN_WIDTH // tn, HEAD_DIM ** -0.5 * math.log2(math.e), 1.0).astype(F32)
    rc = PROJ_RC
    for ch in range(tm // rc):
        rows = slice(ch * rc, (ch + 1) * rc)
        acc = jnp.dot(a_ref[rows, :], w_ref[...], preferred_element_type=F32)
        c = jnp.concatenate([jnp.where(is_qk, c_ref[rows, :], 1.0) * scale] * rep, axis=1)
        s1 = jnp.concatenate([jnp.where(is_qk, s1_ref[rows, :], 0.0) * scale] * rep, axis=1)
        s2 = jnp.concatenate([jnp.where(is_qk, s2_ref[rows, :], 0.0) * scale] * rep, axis=1)
        val = acc * c + pltpu.roll(acc, tn - half, 1) * s1 + pltpu.roll(acc, half, 1) * s2
        for cc in range(rep):
            piece = val[:, cc * LANES:(cc + 1) * LANES]
            nat_ref[cc, rows, :] = piece.astype(nat_ref.dtype)
            slab[cc, rows, :] = piece
        n4, n16 = rc // 4, rc // 16
        for cc in range(rep):
            for r4 in range(4):
                x4 = slab[cc, pl.ds(ch * rc + r4, n4, stride=4), :]
                p4_ref[r4, cc, ch * n4:(ch + 1) * n4, :] = x4.astype(p4_ref.dtype)
                slab4[cc, r4, ch * n4:(ch + 1) * n4, :] = x4
            for r16 in range(16):
                x16 = slab4[cc, r16 % 4, pl.ds(ch * n4 + r16 // 4, n16, stride=4), :]
                p16_ref[r16, cc, ch * n16:(ch + 1) * n16, :] = x16.astype(p16_ref.dtype)


def _qkv_proj(xm, w_qkv, cos_t, sin1_t, sin2_t):
    s, d = xm.shape
    n = w_qkv.shape[1]
    tm, tn = PROJ_TM, PROJ_TN
    nh, hpt = n // HEAD_DIM, tn // HEAD_DIM
    tab = pl.BlockSpec((tm, LANES), lambda i, j: (i, 0))
    return pl.pallas_call(
        functools.partial(_qkv_kernel, tm=tm, tn=tn),
        grid=(s // tm, n // tn),
        in_specs=[pl.BlockSpec((tm, d), lambda i, j: (i, 0)),
                  pl.BlockSpec((d, tn), lambda i, j: (0, j)),
                  tab, tab, tab],
        out_specs=[pl.BlockSpec((hpt, tm, HEAD_DIM), lambda i, j: (j, i, 0)),
                   pl.BlockSpec((4, hpt, tm // 4, HEAD_DIM), lambda i, j: (0, j, i, 0)),
                   pl.BlockSpec((16, hpt, tm // 16, HEAD_DIM), lambda i, j: (0, j, i, 0))],
        out_shape=[jax.ShapeDtypeStruct((nh, s, HEAD_DIM), BF16),
                   jax.ShapeDtypeStruct((4, nh, s // 4, HEAD_DIM), BF16),
                   jax.ShapeDtypeStruct((16, nh, s // 16, HEAD_DIM), BF16)],
        scratch_shapes=[pltpu.VMEM((tn // LANES, tm, LANES), F32),
                        pltpu.VMEM((tn // LANES, 4, tm // 4, LANES), F32)],
        compiler_params=_params(("parallel", "arbitrary"), 52),
        name="qkv_proj",
    )(xm, w_qkv, cos_t, sin1_t, sin2_t)


def _uproj_kernel(a_ref, w_ref, u_ref, slab, *, tm, tn):
    rc = PROJ_RC
    nchunk = rc // SSM_CHUNK
    for ch in range(tm // rc):
        acc = jnp.dot(a_ref[ch * rc:(ch + 1) * rc, :], w_ref[...], preferred_element_type=F32)
        out_rows = slice(ch * nchunk, (ch + 1) * nchunk)
        for c in range(tn // LANES):
            slab[c, ch * rc:(ch + 1) * rc, :] = acc[:, c * LANES:(c + 1) * LANES]
            rows = [slab[c, pl.ds(ch * rc + i, nchunk, stride=SSM_CHUNK), :]
                    for i in range(SSM_CHUNK)]
            first = _piece_transpose8(rows[:8])
            second = _piece_transpose8(rows[8:])
            for gp in range(8):
                u_ref[c * 8 + gp, out_rows, 0:LANES] = first[gp].astype(u_ref.dtype)
                u_ref[c * 8 + gp, out_rows, LANES:2 * LANES] = second[gp].astype(u_ref.dtype)


def _u_proj(xm, w_u):
    s, d = xm.shape
    n = w_u.shape[1]
    tm, tn = PROJ_TM, PROJ_TN
    gpt = tn // SSM_GROUP
    return pl.pallas_call(
        functools.partial(_uproj_kernel, tm=tm, tn=tn),
        grid=(s // tm, n // tn),
        in_specs=[pl.BlockSpec((tm, d), lambda i, j: (i, 0)),
                  pl.BlockSpec((d, tn), lambda i, j: (0, j))],
        out_specs=pl.BlockSpec((gpt, tm // SSM_CHUNK, SSM_CHUNK * SSM_GROUP), lambda i, j: (j, i, 0)),
        out_shape=jax.ShapeDtypeStruct((n // SSM_GROUP, s // SSM_CHUNK, SSM_CHUNK * SSM_GROUP), BF16),
        scratch_shapes=[pltpu.VMEM((tn // LANES, tm, LANES), F32)],
        compiler_params=_params(("parallel", "arbitrary"), 48),
        name="u_proj",
    )(xm, w_u)


def _attn_kernel(q_ref, k_ref, v_ref, o_ref, l_ref, *, n, hps):
    tq = LANES
    tk = 2 * LANES
    nsb = n // tq
    col_minus_row = lax.broadcasted_iota(I32, (tq, tk), 1) - lax.broadcasted_iota(I32, (tq, tk), 0)
    eye = lax.broadcasted_iota(I32, (tq, tq), 0) == lax.broadcasted_iota(I32, (tq, tq), 1)

    def band_bias(key_start_minus_q0):
        return jnp.where(jnp.abs(col_minus_row + key_start_minus_q0) <= ATTN_HALF, 0.0, NEG_INF).astype(F32)

    def block(hh, sb, q0, ks, bias):
        q = q_ref[hh, pl.ds(q0, tq), :]
        k = k_ref[hh, pl.ds(ks, tk), :]
        v = v_ref[hh, pl.ds(ks, tk), :]
        s = lax.dot_general(q, k, (((1,), (1,)), ((), ())), preferred_element_type=F32) + bias
        m = jnp.max(s, axis=-1, keepdims=True)
        p = jnp.exp2(s - m)
        l = jnp.sum(p, axis=-1, keepdims=True)
        o = jnp.dot(p.astype(BF16), v, preferred_element_type=F32) / l
        o_ref[hh, pl.ds(q0, tq), :] = o.astype(o_ref.dtype)
        lse = (m + jnp.log2(l)) * math.log(2.0)
        l_ref[hh, pl.ds(sb, 1), :] = jnp.sum(jnp.where(eye, lse, 0.0), axis=0, keepdims=True)

    interior = nsb - 2
    edge_first, edge_last = band_bias(0), band_bias(-tq)
    bias = band_bias(-ATTN_HALF)
    for hh in range(hps):
        block(hh, 0, 0, 0, edge_first)
        block(hh, nsb - 1, n - tq, n - tk, edge_last)
        if 0 < interior <= 6:
            for sb in range(1, nsb - 1):
                block(hh, sb, sb * tq, sb * tq - ATTN_HALF, bias)
        elif interior > 0:
            unroll = max(u for u in range(1, ATTN_UNROLL + 1) if interior % u == 0)

            def body(it, carry, hh=hh, unroll=unroll):
                for u in range(unroll):
                    sb = 1 + it * unroll + u
                    q0 = pl.multiple_of(sb * tq, tq)
                    block(hh, sb, q0, pl.multiple_of(q0 - ATTN_HALF, ATTN_HALF), bias)
                return carry

            lax.fori_loop(0, interior // unroll, body, 0)


def _dilated_attention(qkv, dil):
    _, _, n, _ = qkv.shape
    s = n * dil
    nsb = n // LANES
    hps = 2 if nsb <= 8 else 1

    def spec(off):
        return pl.BlockSpec((None, hps, n, HEAD_DIM), lambda r, h: (r, off // hps + h, 0, 0))

    o, lse = pl.pallas_call(
        functools.partial(_attn_kernel, n=n, hps=hps),
        grid=(dil, N_HEADS // hps),
        in_specs=[spec(0), spec(N_HEADS), spec(2 * N_HEADS)],
        out_specs=[pl.BlockSpec((None, hps, n, HEAD_DIM), lambda r, h: (r, h, 0, 0)),
                   pl.BlockSpec((hps, nsb, LANES), lambda r, h: (r * (N_HEADS // hps) + h, 0, 0))],
        out_shape=[jax.ShapeDtypeStruct((dil, N_HEADS, n, HEAD_DIM), BF16),
                   jax.ShapeDtypeStruct((dil * N_HEADS, nsb, LANES), F32)],
        compiler_params=_params(("parallel", "parallel"), 48),
        name=f"dilated_attn_d{dil}",
    )(qkv, qkv, qkv)
    lse = lse.reshape(dil, N_HEADS, n).transpose(2, 0, 1).reshape(s, N_HEADS)
    return o, lse


def _merge_kernel(o1_ref, o4_ref, o16_ref, l1_ref, l2_ref, l3_ref, g_ref, out_ref, acc_ref, s4_ref, s16_ref, *, tm):
    for d, src, dst in ((4, o4_ref, s4_ref), (16, o16_ref, s16_ref)):
        for r in range(d):
            for h in range(N_HEADS):
                dst[h, pl.ds(r, tm // d, stride=d), :] = src[r, h].astype(F32)
    la, lb, lc = l1_ref[...], l2_ref[...], l3_ref[...]
    m = jnp.maximum(jnp.maximum(la, lb), lc)
    ea, eb, ec = jnp.exp(la - m), jnp.exp(lb - m), jnp.exp(lc - m)
    den = ea + eb + ec
    wa, wb, wc = ea / den, eb / den, ec / den
    sq = jnp.zeros((tm, 1), F32)
    for h in range(N_HEADS):
        cs = slice(h * HEAD_DIM, (h + 1) * HEAD_DIM)
        slab = (wa[:, h:h + 1] * o1_ref[h].astype(F32)
                + wb[:, h:h + 1] * s4_ref[h]
                + wc[:, h:h + 1] * s16_ref[h])
        acc_ref[:, cs] = slab
        sq = sq + jnp.sum(slab * slab, axis=-1, keepdims=True)
    inv = lax.rsqrt(sq * (1.0 / ATTN_WIDTH) + EPS)
    out_ref[...] = (acc_ref[...] * inv * g_ref[...]).astype(out_ref.dtype)


def _merge_patterns(outs, lses, gain):
    o1, o4, o16 = outs
    s = o1.shape[2]
    tm = 256
    ospec = pl.BlockSpec((tm, ATTN_WIDTH), lambda i: (i, 0))
    lspec = pl.BlockSpec((tm, N_HEADS), lambda i: (i, 0))
    return pl.pallas_call(
        functools.partial(_merge_kernel, tm=tm),
        grid=(s // tm,),
        in_specs=[pl.BlockSpec((None, N_HEADS, tm, HEAD_DIM), lambda i: (0, 0, i, 0)),
                  pl.BlockSpec((4, N_HEADS, tm // 4, HEAD_DIM), lambda i: (0, 0, i, 0)),
                  pl.BlockSpec((16, N_HEADS, tm // 16, HEAD_DIM), lambda i: (0, 0, i, 0)),
                  lspec, lspec, lspec,
                  pl.BlockSpec((1, ATTN_WIDTH), lambda i: (0, 0))],
        out_specs=ospec,
        out_shape=jax.ShapeDtypeStruct((s, ATTN_WIDTH), BF16),
        scratch_shapes=[pltpu.VMEM((tm, ATTN_WIDTH), F32),
                        pltpu.VMEM((N_HEADS, tm, HEAD_DIM), F32),
                        pltpu.VMEM((N_HEADS, tm, HEAD_DIM), F32)],
        compiler_params=_params(("parallel",), 32),
        name="attn_merge_norm",
    )(o1, o4, o16, *lses, gain)


def _ssm_direction_terms(lam_re, lam_im, log_dt, b_re, b_im, c_re, c_im):
    dt = jnp.exp(log_dt)[:, None]
    mag = jnp.exp(lam_re * dt)
    ang = lam_im * dt
    lb_re = mag * jnp.cos(ang)
    lb_im = mag * jnp.sin(ang)
    den = lam_re * lam_re + lam_im * lam_im
    ar = lb_re - 1.0
    ai = lb_im
    coef_re = (ar * lam_re + ai * lam_im) / den
    coef_im = (ai * lam_re - ar * lam_im) / den
    bb_re = coef_re[..., None] * b_re - coef_im[..., None] * b_im
    bb_im = coef_re[..., None] * b_im + coef_im[..., None] * b_re
    tau = jnp.arange(SSM_CHUNK + 1, dtype=F32)[:, None, None]
    pw_mag = jnp.exp(tau * (lam_re * dt)[None])
    pw_re = pw_mag * jnp.cos(tau * ang[None])
    pw_im = pw_mag * jnp.sin(tau * ang[None])
    e_re = pw_re[..., None] * bb_re[None] - pw_im[..., None] * bb_im[None]
    e_im = pw_re[..., None] * bb_im[None] + pw_im[..., None] * bb_re[None]
    kern = (jnp.einsum('ghp,tgpi->tghi', c_re, e_re[:SSM_CHUNK], precision=HIGHEST)
            - jnp.einsum('ghp,tgpi->tghi', c_im, e_im[:SSM_CHUNK], precision=HIGHEST))
    ca_re = c_re[None] * pw_re[:, :, None, :] - c_im[None] * pw_im[:, :, None, :]
    ca_im = c_re[None] * pw_im[:, :, None, :] + c_im[None] * pw_re[:, :, None, :]
    return kern, e_re, e_im, ca_re, ca_im, pw_re[SSM_CHUNK], pw_im[SSM_CHUNK]


def _ssm_operators(lam_re_f, lam_im_f, log_dt_f, lam_re_b, lam_im_b, log_dt_b, b_re, b_im, c_re, c_im):
    L, G, P = SSM_CHUNK, SSM_GROUPS, SSM_STATE
    kf, ef_re, ef_im, caf_re, caf_im, af_re, af_im = _ssm_direction_terms(
        lam_re_f, lam_im_f, log_dt_f, b_re, b_im, c_re, c_im)
    kb, eb_re, eb_im, cab_re, cab_im, ab_re, ab_im = _ssm_direction_terms(
        lam_re_b, lam_im_b, log_dt_b, b_re, b_im, c_re, c_im)
    lag = jnp.arange(L)[None, :] - jnp.arange(L)[:, None]
    tau = jnp.arange(L)[:, None, None]
    place = jnp.concatenate([(lag[None] == tau), (-lag[None] == tau)], axis=0).astype(F32)
    w = jnp.einsum('tij,tgoh->gihjo', place, jnp.concatenate([kf, kb], axis=0), precision=HIGHEST)
    w = w.reshape(G, L * SSM_GROUP, L * SSM_GROUP)

    def p_mat(e, order):
        return jnp.transpose(e[order], (1, 0, 3, 2)).reshape(G, L * SSM_GROUP, P)

    def q_mat(ca, order):
        return jnp.transpose(ca[order], (1, 3, 0, 2)).reshape(G, P, L * SSM_GROUP)

    f_ord = (L - 1) - jnp.arange(L)
    b_ord = jnp.arange(L)
    parts_p = [p_mat(ef_re, f_ord), p_mat(ef_im, f_ord), p_mat(eb_re, b_ord), p_mat(eb_im, b_ord)]
    parts_q = [q_mat(caf_re, jnp.arange(L) + 1), -q_mat(caf_im, jnp.arange(L) + 1),
               q_mat(cab_re, L - jnp.arange(L)), -q_mat(cab_im, L - jnp.arange(L))]
    even = (jnp.arange(G) % 2 == 0)[:, None, None]

    def pad_p(x):
        z = jnp.zeros_like(x)
        return jnp.where(even, jnp.concatenate([x, z], -1), jnp.concatenate([z, x], -1))

    def pad_q(x):
        z = jnp.zeros_like(x)
        return jnp.where(even, jnp.concatenate([x, z], 1), jnp.concatenate([z, x], 1))

    p_all = jnp.stack([pad_p(x) for x in parts_p], axis=1).astype(BF16)
    q_all = jnp.stack([pad_q(x) for x in parts_q], axis=1).astype(BF16)

    def tiles(re, im):
        re = re.reshape(G // SSM_GB, SSM_GB // 2, 2 * P)
        im = im.reshape(G // SSM_GB, SSM_GB // 2, 2 * P)
        return jnp.concatenate([re, re], axis=1), jnp.concatenate([-im, im], axis=1)

    a_all = jnp.stack([*tiles(af_re, af_im), *tiles(ab_re, ab_im)], axis=1)
    return w.astype(BF16), p_all, q_all, a_all


def _gelu_tanh(x):
    return 0.5 * x * (1.0 + jnp.tanh(math.sqrt(2.0 / math.pi) * (x + 0.044715 * (x * x * x))))


SUBLANES = 8


def _ssm_kernel(u_ref, w_ref, p_ref, q_ref, a_ref, d_ref, y_ref, sf, sb, *, nc, gb):
    npair = gb // 2
    assert 2 * npair == SUBLANES
    for pr in range(npair):
        g0, g1 = 2 * pr, 2 * pr + 1
        for k, (scr, row) in enumerate(((sf, pr), (sf, npair + pr), (sb, pr), (sb, npair + pr))):
            contrib = (jnp.dot(u_ref[g0], p_ref[g0, k], preferred_element_type=F32)
                       + jnp.dot(u_ref[g1], p_ref[g1, k], preferred_element_type=F32))
            scr[pl.ds(row, nc, stride=SUBLANES), :] = contrib

    a1f, a2f, a1b, a2b = a_ref[0, 0], a_ref[0, 1], a_ref[0, 2], a_ref[0, 3]

    def step(c, carry):
        hf, hb = carry
        cf = pl.multiple_of(c * SUBLANES, SUBLANES)
        cb = pl.multiple_of((nc - 1 - c) * SUBLANES, SUBLANES)
        rf = sf[pl.ds(cf, SUBLANES), :]
        rb = sb[pl.ds(cb, SUBLANES), :]
        sf[pl.ds(cf, SUBLANES), :] = hf
        sb[pl.ds(cb, SUBLANES), :] = hb
        return (a1f * hf + a2f * pltpu.roll(hf, npair, 0) + rf,
                a1b * hb + a2b * pltpu.roll(hb, npair, 0) + rb)

    z = jnp.zeros((SUBLANES, LANES), F32)
    lax.fori_loop(0, nc, step, (z, z))

    for pr in range(npair):
        states = [scr[pl.ds(row, nc, stride=SUBLANES), :].astype(BF16)
                  for scr, row in ((sf, pr), (sf, npair + pr), (sb, pr), (sb, npair + pr))]
        for g in (2 * pr, 2 * pr + 1):
            u = u_ref[g]
            y = jnp.dot(u, w_ref[g], preferred_element_type=F32)
            for k in range(4):
                y = y + jnp.dot(states[k], q_ref[g, k], preferred_element_type=F32)
            y = y + d_ref[g] * u.astype(F32)
            y_ref[g] = _gelu_tanh(y).astype(y_ref.dtype)


def _ssm_mixer(u_t, w, p_all, q_all, a_all, d_t):
    g, nc, k = u_t.shape
    gb = SSM_GB
    return pl.pallas_call(
        functools.partial(_ssm_kernel, nc=nc, gb=gb),
        grid=(g // gb,),
        in_specs=[pl.BlockSpec((gb, nc, k), lambda i: (i, 0, 0)),
                  pl.BlockSpec((gb, k, k), lambda i: (i, 0, 0)),
                  pl.BlockSpec((gb, 4, k, LANES), lambda i: (i, 0, 0, 0)),
                  pl.BlockSpec((gb, 4, LANES, k), lambda i: (i, 0, 0, 0)),
                  pl.BlockSpec((1, 4, SUBLANES, LANES), lambda i: (i, 0, 0, 0)),
                  pl.BlockSpec((gb, 1, k), lambda i: (i, 0, 0))],
        out_specs=pl.BlockSpec((gb, nc, k), lambda i: (i, 0, 0)),
        out_shape=jax.ShapeDtypeStruct((g, nc, k), BF16),
        scratch_shapes=[pltpu.VMEM((nc * SUBLANES, LANES), F32)] * 2,
        compiler_params=_params(("parallel",), 48),
        name="ssm_scan",
    )(u_t, w, p_all, q_all, a_all, d_t)


def _glu_kernel(g_ref, w_ref, b_ref, n_ref, o_ref, nat, gb, *, tm):
    nchunk = tm // SSM_CHUNK
    for sl in range(SSM_WIDTH // LANES):
        for half in range(2):
            xs = [g_ref[8 * sl + gp, :, half * LANES:(half + 1) * LANES].astype(F32) for gp in range(8)]
            ys = _piece_transpose8(xs)
            for jj in range(8):
                nat[sl, pl.ds(half * 8 + jj, nchunk, stride=SSM_CHUNK), :] = ys[jj]
        gb[:, sl * LANES:(sl + 1) * LANES] = nat[sl].astype(BF16)
    z = jnp.dot(gb[...], w_ref[...], preferred_element_type=F32) + b_ref[...]
    sq = jnp.zeros((tm, 1), F32)
    for sl in range(SSM_WIDTH // LANES):
        cs = slice(sl * LANES, (sl + 1) * LANES)
        out = nat[sl] * jax.nn.sigmoid(z[:, cs])
        nat[sl] = out
        sq = sq + jnp.sum(out * out, axis=-1, keepdims=True)
    inv = lax.rsqrt(sq * (1.0 / SSM_WIDTH) + EPS)
    for sl in range(SSM_WIDTH // LANES):
        cs = slice(sl * LANES, (sl + 1) * LANES)
        o_ref[:, cs] = (nat[sl] * inv * n_ref[:, cs]).astype(o_ref.dtype)


def _glu_norm(g_t, w_glu, b_glu, gain):
    ng, nc, k = g_t.shape
    s, d = nc * SSM_CHUNK, ng * SSM_GROUP
    tm = 512
    row = pl.BlockSpec((1, d), lambda i: (0, 0))
    return pl.pallas_call(
        functools.partial(_glu_kernel, tm=tm),
        grid=(s // tm,),
        in_specs=[pl.BlockSpec((ng, tm // SSM_CHUNK, k), lambda i: (0, i, 0)),
                  pl.BlockSpec((d, d), lambda i: (0, 0)), row, row],
        out_specs=pl.BlockSpec((tm, d), lambda i: (i, 0)),
        out_shape=jax.ShapeDtypeStruct((s, d), BF16),
        scratch_shapes=[pltpu.VMEM((d // LANES, tm, LANES), F32),
                        pltpu.VMEM((tm, d), BF16)],
        compiler_params=_params(("parallel",), 48),
        name="ssm_glu_norm",
    )(g_t, w_glu, b_glu, gain)


def _outproj_kernel(a1_ref, a2_ref, w1_ref, w2_ref, x_ref, gm_ref, o_ref):
    mix = (jnp.dot(a1_ref[...], w1_ref[...], preferred_element_type=F32)
           + jnp.dot(a2_ref[...], w2_ref[...], preferred_element_type=F32))
    o_ref[...] = x_ref[...] + gm_ref[...] * mix


def _out_proj(attn_n, ssm_n, w_out, x, g_m):
    s, k = attn_n.shape
    n = w_out.shape[1]
    tm, tn = 1024, 512
    return pl.pallas_call(
        _outproj_kernel,
        grid=(s // tm, n // tn),
        in_specs=[pl.BlockSpec((tm, k), lambda i, j: (i, 0)),
                  pl.BlockSpec((tm, k), lambda i, j: (i, 0)),
                  pl.BlockSpec((k, tn), lambda i, j: (0, j)),
                  pl.BlockSpec((k, tn), lambda i, j: (1, j)),
                  pl.BlockSpec((tm, tn), lambda i, j: (i, j)),
                  pl.BlockSpec((1, tn), lambda i, j: (0, j))],
        out_specs=pl.BlockSpec((tm, tn), lambda i, j: (i, j)),
        out_shape=jax.ShapeDtypeStruct((s, n), F32),
        compiler_params=_params(("parallel", "arbitrary"), 48),
        name="out_proj_residual",
    )(attn_n, ssm_n, w_out, w_out, x, g_m)


def _ffn_prep_kernel(h_ref, g_ref, sc_ref, sh_ref, wr_ref, xf_ref, lg_ref):
    xf = _norm_mod(h_ref[...], g_ref[...], sc_ref[...], sh_ref[...])
    half = xf.shape[1] // 2
    lo = pltpu.bitcast(xf[:, :half].astype(BF16).astype(F32), jnp.uint32)
    hi = pltpu.bitcast(xf[:, half:].astype(BF16).astype(F32), jnp.uint32)
    xf_ref[...] = (lo >> 16) | hi
    lg_ref[...] = lax.dot_general(wr_ref[...], xf, (((1,), (1,)), ((), ())),
                                  precision=HIGHEST, preferred_element_type=F32)


def _ffn_prep(h1, gain, scale, shift, w_router_t):
    s, d = h1.shape
    tm = 256
    row = pl.BlockSpec((1, d), lambda i: (0, 0))
    return pl.pallas_call(
        _ffn_prep_kernel,
        grid=(s // tm,),
        in_specs=[pl.BlockSpec((tm, d), lambda i: (i, 0)), row, row, row,
                  pl.BlockSpec((N_EXPERTS, d), lambda i: (0, 0))],
        out_specs=[pl.BlockSpec((tm, d // 2), lambda i: (i, 0)),
                   pl.BlockSpec((N_EXPERTS, tm), lambda i: (0, i))],
        out_shape=[jax.ShapeDtypeStruct((s, d // 2), jnp.uint32),
                   jax.ShapeDtypeStruct((N_EXPERTS, s), F32)],
        compiler_params=_params(("parallel",), 40),
        name="ffn_norm_router",
    )(h1, gain, scale, shift, w_router_t)


def _route_kernel(lg_ref, sel_ref, pos_ref, aff_ref, *, cap, nk):
    e_n = N_EXPERTS
    lg = lg_ref[...]
    mx = jnp.max(lg, axis=0, keepdims=True)
    ex = jnp.exp(lg - mx)
    aff = ex / jnp.sum(ex, axis=0, keepdims=True)
    aff_ref[...] = aff
    bits = pltpu.bitcast(aff, I32)

    def count(mask):
        return jnp.sum(jnp.sum(mask.astype(F32), axis=2, keepdims=True), axis=1, keepdims=True)

    def radix(i, thr):
        cand = thr | jnp.left_shift(jnp.int32(1), 30 - i)
        return jnp.where(count(bits >= cand) >= cap, cand, thr)

    thr = lax.fori_loop(0, 31, radix, jnp.zeros((e_n, 1, 1), I32))
    gt = bits > thr
    eq = bits == thr
    need = cap - count(gt)

    li = lax.broadcasted_iota(I32, (LANES, LANES), 0)
    lj = lax.broadcasted_iota(I32, (LANES, LANES), 1)
    tri_lane = (li < lj).astype(BF16)
    ki = lax.broadcasted_iota(I32, (nk, nk), 0)
    kj = lax.broadcasted_iota(I32, (nk, nk), 1)
    tri_row = (kj < ki).astype(BF16)

    def excl_prefix(mask):
        mf = mask.astype(F32)
        within = jnp.dot(mf.reshape(e_n * nk, LANES).astype(BF16), tri_lane,
                         preferred_element_type=F32).reshape(e_n, nk, LANES)
        rowtot = jnp.broadcast_to(jnp.sum(mf, axis=2, keepdims=True), (e_n, nk, LANES)).astype(BF16)
        across = jnp.stack([jnp.dot(tri_row, rowtot[e], preferred_element_type=F32) for e in range(e_n)])
        return within + across

    sel = gt | (eq & (excl_prefix(eq) < need))
    sel_ref[...] = sel.astype(I32)
    pos_ref[...] = excl_prefix(sel).astype(I32)


def _route(logits_t, cap):
    e_n, s = logits_t.shape
    nk = s // LANES
    spec = pl.BlockSpec((e_n, nk, LANES), lambda i: (0, 0, 0))
    sel, pos, aff = pl.pallas_call(
        functools.partial(_route_kernel, cap=cap, nk=nk),
        grid=(1,),
        in_specs=[spec],
        out_specs=[spec, spec, spec],
        out_shape=[jax.ShapeDtypeStruct((e_n, nk, LANES), I32),
                   jax.ShapeDtypeStruct((e_n, nk, LANES), I32),
                   jax.ShapeDtypeStruct((e_n, nk, LANES), F32)],
        compiler_params=_params(("arbitrary",), 48),
        name="expert_choice_route",
    )(logits_t.reshape(e_n, nk, LANES))
    return sel.reshape(e_n, s), pos.reshape(e_n, s), aff.reshape(e_n, s)


def _compact_kernel(off_ref, sel_ref, pos_ref, aff_ref, idx_ref, gate_ref, *, nk):
    e = pl.program_id(0)
    idx_ref[...] = jnp.zeros(idx_ref.shape, I32)
    gate_ref[...] = jnp.zeros(gate_ref.shape, F32)
    local = lax.broadcasted_iota(I32, (LANES, LANES), 0)
    lane = lax.broadcasted_iota(I32, (LANES, LANES), 1)

    unroll = 4 if nk % 4 == 0 else 1

    def body(it, carry):
        for u in range(unroll):
            k = it * unroll + u
            off = off_ref[e, k]
            sel = sel_ref[0, pl.ds(k, 1), :]
            pos = pos_ref[0, pl.ds(k, 1), :]
            aff = aff_ref[0, pl.ds(k, 1), :]
            hit = (sel > 0) & ((pos - off) == local)
            tok = jnp.sum(jnp.where(hit, (lane + k * LANES).astype(F32), 0.0), axis=1, keepdims=True)
            gat = jnp.sum(jnp.where(hit, aff, 0.0), axis=1, keepdims=True)
            idx_ref[0, pl.ds(off, LANES), :] = tok.astype(I32)
            gate_ref[0, pl.ds(off, LANES), :] = gat
        return carry

    lax.fori_loop(0, nk // unroll, body, 0)


def _compact(sel, pos, aff, cap):
    e_n, s = sel.shape
    nk = s // LANES
    capp = cap + LANES
    offs = pos[:, ::LANES]
    tok = pl.BlockSpec((1, nk, LANES), lambda e, off: (e, 0, 0))
    col = pl.BlockSpec((1, capp, 1), lambda e, off: (e, 0, 0))
    grid_spec = pltpu.PrefetchScalarGridSpec(
        num_scalar_prefetch=1, grid=(e_n,), in_specs=[tok, tok, tok], out_specs=[col, col])
    idx, gate = pl.pallas_call(
        functools.partial(_compact_kernel, nk=nk),
        grid_spec=grid_spec,
        out_shape=[jax.ShapeDtypeStruct((e_n, capp, 1), I32),
                   jax.ShapeDtypeStruct((e_n, capp, 1), F32)],
        compiler_params=_params(("arbitrary",), 32),
        name="expert_compact",
    )(offs, sel.reshape(e_n, nk, LANES), pos.reshape(e_n, nk, LANES), aff.reshape(e_n, nk, LANES))
    return idx[:, :cap, 0], gate[:, :cap, :]


def _moe_up_kernel(idx_hbm, xf_hbm, wg_ref, wu_ref, h_ref, xbuf, idx_s, sem_x, sem_i, *, cap, tm, rps):
    nf, nh = pl.num_programs(1), pl.num_programs(2)
    phase = pl.program_id(0)
    n_phase = pl.num_programs(0)
    f = pl.program_id(1) * nh + pl.program_id(2)
    nf = nf * nh
    step = phase * nf + f
    b = phase % 2
    row0 = pl.multiple_of(pl.program_id(2) * tm, tm)

    def row_copy(token, buf, r):
        return pltpu.make_async_copy(xf_hbm.at[pl.ds(token, 1), :], xbuf.at[buf, pl.ds(r, 1), :], sem_x.at[buf])

    def wait_rows(buf, n):
        pltpu.make_async_copy(xf_hbm.at[pl.ds(0, n), :], xbuf.at[buf, pl.ds(0, n), :], sem_x.at[buf]).wait()

    @pl.when(step == 0)
    def _prologue():
        c_idx = pltpu.make_async_copy(idx_hbm, idx_s, sem_i.at[0])
        c_idx.start()
        c_idx.wait()

        def one(r, c):
            row_copy(idx_s[r], 0, r).start()
            return c
        lax.fori_loop(0, cap, one, 0)
        wait_rows(0, cap)

    @pl.when(step > 0)
    def _():
        wait_rows(jnp.where(f == 0, b, 1 - b), rps)

    nxt = jnp.minimum(phase + 1, n_phase - 1)
    base = nxt * cap + f * rps
    for r in range(rps):
        row_copy(idx_s[base + r], 1 - b, f * rps + r).start()

    wg = wg_ref[0].astype(BF16)
    wu = wu_ref[0].astype(BF16)
    half = wg.shape[0] // 2
    rc = MOE_RC
    for ch in range(tm // rc):
        x = xbuf[b, pl.ds(row0 + ch * rc, rc), :]
        x_lo = pltpu.bitcast(x << 16, F32).astype(BF16)
        x_hi = pltpu.bitcast(x & jnp.uint32(0xFFFF0000), F32).astype(BF16)
        g = (jnp.dot(x_lo, wg[:half], preferred_element_type=F32)
             + jnp.dot(x_hi, wg[half:], preferred_element_type=F32))
        u = (jnp.dot(x_lo, wu[:half], preferred_element_type=F32)
             + jnp.dot(x_hi, wu[half:], preferred_element_type=F32))
        h_ref[0, ch * rc:(ch + 1) * rc, :] = (g * jax.nn.sigmoid(g) * u).astype(h_ref.dtype)

    @pl.when(step == n_phase * nf - 1)
    def _drain():
        wait_rows(1 - b, rps)


MOE_RC = 256


def _moe_up(idx, xf_packed, w_gate, w_up, cap):
    e_n, d, ff = w_gate.shape
    tf = 256
    tm = min(1024, cap)
    nf, nh = ff // tf, cap // tm
    rps = cap // (nf * nh)
    assert rps * nf * nh == cap and tm % MOE_RC == 0
    return pl.pallas_call(
        functools.partial(_moe_up_kernel, cap=cap, tm=tm, rps=rps),
        grid=(e_n, nf, nh),
        in_specs=[pl.BlockSpec(memory_space=pl.ANY),
                  pl.BlockSpec(memory_space=pl.ANY),
                  pl.BlockSpec((1, d, tf), lambda e, f, m: (e, 0, f)),
                  pl.BlockSpec((1, d, tf), lambda e, f, m: (e, 0, f))],
        out_specs=pl.BlockSpec((1, tm, tf), lambda e, f, m: (e, m, f)),
        out_shape=jax.ShapeDtypeStruct((e_n, cap, ff), BF16),
        scratch_shapes=[pltpu.VMEM((2, cap, d // 2), jnp.uint32),
                        pltpu.SMEM((e_n * cap,), I32),
                        pltpu.SemaphoreType.DMA((2,)),
                        pltpu.SemaphoreType.DMA((1,))],
        compiler_params=_params(("arbitrary", "arbitrary", "arbitrary"), 60),
        name="moe_gather_up",
    )(idx.reshape(e_n * cap), xf_packed, w_gate, w_up)


def _moe_down_kernel(h_ref, wd_ref, gate_ref, y_ref):
    y = jnp.dot(h_ref[0], wd_ref[0].astype(BF16), preferred_element_type=F32)
    y_ref[0] = (y * gate_ref[0]).astype(y_ref.dtype)


def _moe_down(hdn, w_down, gates):
    e_n, cap, ff = hdn.shape
    d = w_down.shape[2]
    tn = 512
    return pl.pallas_call(
        _moe_down_kernel,
        grid=(e_n, d // tn),
        in_specs=[pl.BlockSpec((1, cap, ff), lambda e, n: (e, 0, 0)),
                  pl.BlockSpec((1, ff, tn), lambda e, n: (e, 0, n)),
                  pl.BlockSpec((1, cap, 1), lambda e, n: (e, 0, 0))],
        out_specs=pl.BlockSpec((1, cap, tn), lambda e, n: (e, 0, n)),
        out_shape=jax.ShapeDtypeStruct((e_n, cap, d), BF16),
        compiler_params=_params(("parallel", "arbitrary"), 48),
        name="moe_down",
    )(hdn, w_down, gates)


COMBINE_WINDOW = 64
SLOT_ALIGN = 16


def _combine_kernel(lo_ref, h_ref, sel_ref, pos_ref, gf_ref, gn_ref, y_hbm, o_ref,
                    ybuf, yextra, acc_ref, sem, sem_x, *, cap, tm):
    i = pl.program_id(0)
    win = COMBINE_WINDOW
    e_n = N_EXPERTS

    n_steps = pl.num_programs(0)
    cur = i % 2

    def window(e, j, step=None):
        step = i if step is None else step
        nominal = (lo_ref[e, step] // SLOT_ALIGN) * SLOT_ALIGN + j * win
        start = pl.multiple_of(jnp.minimum(nominal, cap - win), SLOT_ALIGN)
        return nominal, start

    def first_copy(e, step, buf):
        _, start = window(e, 0, step)
        return pltpu.make_async_copy(y_hbm.at[e, pl.ds(start, win), :],
                                     ybuf.at[buf, pl.ds(e * win, win), :], sem.at[buf, e])

    @pl.when(i == 0)
    def _():
        for e in range(e_n):
            first_copy(e, 0, 0).start()

    @pl.when(i + 1 < n_steps)
    def _():
        for e in range(e_n):
            first_copy(e, i + 1, 1 - cur).start()

    sel = sel_ref[...]
    pos = pos_ref[...]
    lane = lax.broadcasted_iota(I32, (tm, LANES), 1)
    left = lane < win
    pieces = []
    for e2 in range(e_n // 2):
        ea, eb = 2 * e2, 2 * e2 + 1
        na, sa = window(ea, 0)
        nb, sb = window(eb, 0)
        slot = jnp.where(left, sa + lane, sb + lane - win)
        nominal = jnp.where(left, na, nb)
        pe = jnp.where(left, pos[:, ea:ea + 1], pos[:, eb:eb + 1])
        se = jnp.where(left, sel[:, ea:ea + 1], sel[:, eb:eb + 1])
        pieces.append(((pe == slot) & (se > 0) & (slot >= nominal)).astype(BF16))
    onehot = jnp.concatenate(pieces, axis=1)
    for e in range(e_n):
        first_copy(e, i, cur).wait()
    acc_ref[...] = jnp.dot(onehot, ybuf[cur], preferred_element_type=F32)

    lane_w = lax.broadcasted_iota(I32, (tm, win), 1)
    for e in range(e_n):
        base = (lo_ref[e, i] // SLOT_ALIGN) * SLOT_ALIGN
        nwin = (lo_ref[e, i + 1] - base + win - 1) // win

        def extra(j, c, e=e):
            nominal, start = window(e, j)
            cp = pltpu.make_async_copy(y_hbm.at[e, pl.ds(start, win), :], yextra, sem_x.at[0])
            cp.start()
            cp.wait()
            slot = start + lane_w
            oh = ((pos[:, e:e + 1] == slot) & (sel[:, e:e + 1] > 0) & (slot >= nominal)).astype(BF16)
            acc_ref[...] += jnp.dot(oh, yextra[...], preferred_element_type=F32)
            return c

        lax.fori_loop(1, nwin, extra, 0)

    h2 = h_ref[...] + gf_ref[...] * acc_ref[...]
    ms = jnp.mean(h2 * h2, axis=-1, keepdims=True)
    o_ref[...] = h2 * lax.rsqrt(ms + EPS) * gn_ref[...]


def _combine(h1, y, sel_tm, pos_tm, lo, g_f, gain, cap):
    s, d = h1.shape
    tm = 256
    win = COMBINE_WINDOW
    row = pl.BlockSpec((1, d), lambda i, lo_r: (0, 0))
    tok = pl.BlockSpec((tm, N_EXPERTS), lambda i, lo_r: (i, 0))
    grid_spec = pltpu.PrefetchScalarGridSpec(
        num_scalar_prefetch=1,
        grid=(s // tm,),
        in_specs=[pl.BlockSpec((tm, d), lambda i, lo_r: (i, 0)), tok, tok, row, row,
                  pl.BlockSpec(memory_space=pl.ANY)],
        out_specs=pl.BlockSpec((tm, d), lambda i, lo_r: (i, 0)),
        scratch_shapes=[pltpu.VMEM((2, N_EXPERTS * win, d), BF16),
                        pltpu.VMEM((win, d), BF16),
                        pltpu.VMEM((tm, d), F32),
                        pltpu.SemaphoreType.DMA((2, N_EXPERTS)),
                        pltpu.SemaphoreType.DMA((1,))],
    )
    return pl.pallas_call(
        functools.partial(_combine_kernel, cap=cap, tm=tm),
        grid_spec=grid_spec,
        out_shape=jax.ShapeDtypeStruct((s, d), F32),
        compiler_params=_params(("arbitrary",), 56),
        name="moe_combine_norm",
    )(lo, h1, sel_tm, pos_tm, g_f, gain, y)


def kernel(x, c, positions, w_ada, b_ada, norm_mix_gain, w_in, lam_re_fwd, lam_im_fwd, log_dt_fwd, lam_re_bwd, lam_im_bwd, log_dt_bwd, ssm_b_re, ssm_b_im, ssm_c_re, ssm_c_im, ssm_d, w_glu, b_glu, norm_attn_out_gain, norm_ssm_out_gain, w_out, norm_ffn_gain, w_router, w_exp_gate, w_exp_up, w_exp_down, norm_final_gain):
    batch, s, d = x.shape
    depth = w_ada.shape[0]
    assert batch == 1 and d == D_MODEL and depth == 1
    cap = max(1, CAPACITY_FACTOR * s // N_EXPERTS)
    h = x[0]
    pos = positions[0]
    cos_t, sin1_t, sin2_t = _rope_tables(pos)
    for layer in range(depth):
        mod = _adaln(c, w_ada[layer], b_ada[layer])
        sh_m, sc_m, g_m, sh_f, sc_f, g_f = jnp.split(mod, 6, axis=-1)

        xm = _norm_mod_call(h, norm_mix_gain[layer][None], sc_m, sh_m, BF16)
        qkv, qkv4, qkv16 = _qkv_proj(xm, w_in[layer, :, :3 * ATTN_WIDTH].astype(BF16), cos_t, sin1_t, sin2_t)
        outs, lses = [], []
        for dil, arr in zip(DILATIONS, (qkv[None], qkv4, qkv16)):
            o, l = _dilated_attention(arr, dil)
            outs.append(o)
            lses.append(l)
        attn_n = _merge_patterns(outs, lses, norm_attn_out_gain[layer][None])

        u_t = _u_proj(xm, w_in[layer, :, 3 * ATTN_WIDTH:].astype(BF16))
        w_t, p_all, q_all, a_all = _ssm_operators(
            lam_re_fwd[layer], lam_im_fwd[layer], log_dt_fwd[layer],
            lam_re_bwd[layer], lam_im_bwd[layer], log_dt_bwd[layer],
            ssm_b_re[layer], ssm_b_im[layer], ssm_c_re[layer], ssm_c_im[layer])
        d_t = jnp.tile(ssm_d[layer].reshape(SSM_GROUPS, 1, SSM_GROUP), (1, 1, SSM_CHUNK))
        g_t = _ssm_mixer(u_t, w_t, p_all, q_all, a_all, d_t)
        ssm_n = _glu_norm(g_t, w_glu[layer].astype(BF16), b_glu[layer][None], norm_ssm_out_gain[layer][None])

        h = _out_proj(attn_n, ssm_n, w_out[layer].astype(BF16), h, g_m)

        xf, logits_t = _ffn_prep(h, norm_ffn_gain[layer][None], sc_f, sh_f, jnp.transpose(w_router[layer]))
        sel, slot, aff = _route(logits_t, cap)
        idx, gates = _compact(sel, slot, aff, cap)
        hdn = _moe_up(idx, xf, w_exp_gate[layer], w_exp_up[layer], cap)
        y = _moe_down(hdn, w_exp_down[layer], gates)
        tm = 256
        lo = jnp.concatenate([slot[:, ::tm], jnp.full((N_EXPERTS, 1), cap, I32)], axis=1)
        h = _combine(h, y, jnp.transpose(sel), jnp.transpose(slot), lo, g_f, norm_final_gain[None], cap)
    return h[None]
```
